```python
import math
import jax, jax.numpy as jnp
from jax import lax
import numpy as np


D_MODEL = 2048
BATCH = 16
SEQ = 2048
DEPTH = 2

N_EVEN = (DEPTH + 1) // 2
N_ODD = DEPTH // 2
EPS = 1e-6
CONV_K = 4

LRU_WIDTH = D_MODEL // 2
LRU_BLOCKS = 8
LRU_BLOCK = LRU_WIDTH // LRU_BLOCKS
LRU_C = 8.0

MLSTM_WIDTH = D_MODEL // 2
MLSTM_HEADS = 4
MLSTM_HD = MLSTM_WIDTH // MLSTM_HEADS
MLSTM_CHUNK = 64

IN_COLS = 2 * LRU_WIDTH + 4 * MLSTM_WIDTH + 2 * MLSTM_HEADS
IN_SPLITS = (LRU_WIDTH, 2 * LRU_WIDTH, 2 * LRU_WIDTH + 2 * MLSTM_WIDTH,
             2 * LRU_WIDTH + 3 * MLSTM_WIDTH, 2 * LRU_WIDTH + 4 * MLSTM_WIDTH,
             2 * LRU_WIDTH + 4 * MLSTM_WIDTH + MLSTM_HEADS)

S5_WIDTH = D_MODEL
S5_GROUP = 16
S5_GROUPS = S5_WIDTH // S5_GROUP
S5_STATE = 64
S5_CHUNK = 64

MOE_GROUPS = 4
MOE_PER_GROUP = 4
MOE_EXPERTS = MOE_GROUPS * MOE_PER_GROUP
MOE_TOP_K = 2
MOE_FF = 512

kernel_name = 'hybrid_rglru_mlstm_s5_hmoe'


def rmsnorm(x, g):
    xf = x.astype(jnp.float32)
    y = xf * lax.rsqrt(jnp.mean(xf * xf, axis=-1, keepdims=True) + EPS)
    return y.astype(x.dtype) * g


def causal_dwconv(x, w, b):
    k, s = w.shape[0], x.shape[1]
    xp = jnp.pad(x, ((0, 0), (k - 1, 0), (0, 0)))
    out = b
    for j in range(k):
        out = out + w[j] * xp[:, j:j + s]
    return out


def _lin_combine(e1, e2):
    a1, b1 = e1
    a2, b2 = e2
    return a1 * a2, a2 * b1 + b2


def cmul(ar, ai, br, bi):
    return ar * br - ai * bi, ar * bi + ai * br


def _cplx_combine(e1, e2):
    ar1, ai1, br1, bi1 = e1
    ar2, ai2, br2, bi2 = e2
    ar, ai = cmul(ar1, ai1, ar2, ai2)
    pr, pi = cmul(ar2, ai2, br1, bi1)
    return ar, ai, pr + br2, pi + bi2


def rg_lru(x, w_a, b_a, w_x, b_x, lam):
    bsz, s, width = x.shape
    xf = x.astype(jnp.float32)
    xb = xf.reshape(bsz, s, LRU_BLOCKS, LRU_BLOCK)
    r = jax.nn.sigmoid(jnp.einsum('bsni,nij->bsnj', xb, w_a).reshape(bsz, s, width) + b_a)
    i = jax.nn.sigmoid(jnp.einsum('bsni,nij->bsnj', xb, w_x).reshape(bsz, s, width) + b_x)
    log_a = -LRU_C * r * jax.nn.softplus(-lam.astype(jnp.float32))
    a = jnp.exp(log_a)
    gated = jnp.sqrt(-jnp.expm1(2.0 * log_a)) * (i * xf)
    _, h = lax.associative_scan(_lin_combine, (a, gated), axis=1)
    return h


def mlstm_chunkwise(q, k, v, i_pre, f_pre):
    bsz, s, nh, dh = q.shape
    L = MLSTM_CHUNK
    nc = s // L
    f32 = jnp.float32

    def to_chunks(t):
        t = t.astype(f32).reshape((bsz, nc, L) + t.shape[2:])
        return jnp.moveaxis(t, (1, 3), (0, 2))

    qc = to_chunks(q)
    kc = to_chunks(k) * (dh ** -0.5)
    vc = to_chunks(v)
    ic = to_chunks(i_pre)
    lfc = to_chunks(jax.nn.log_sigmoid(f_pre.astype(f32)))
    causal = jnp.tril(jnp.ones((L, L), dtype=bool))

    def step(carry, inp):
        c_st, n_st, m_st = carry
        qb, kb, vb, ib, lfb = inp
        bcum = jnp.cumsum(lfb, axis=-1)
        d = bcum[..., :, None] - bcum[..., None, :] + ib[..., None, :]
        d = jnp.where(causal, d, -jnp.inf)
        g = bcum + m_st[..., None]
        m_t = jnp.maximum(g, jnp.max(d, axis=-1))
        w_intra = jnp.exp(d - m_t[..., None])
        w_inter = jnp.exp(g - m_t)
        s_qk = jnp.einsum('bhtd,bhjd->bhtj', qb, kb) * w_intra
        num = (jnp.einsum('bhtj,bhjd->bhtd', s_qk, vb)
               + w_inter[..., None] * jnp.einsum('bhde,bhte->bhtd', c_st, qb))
        den = jnp.sum(s_qk, axis=-1) + w_inter * jnp.einsum('bhd,bhtd->bht', n_st, qb)
        h = num / jnp.maximum(jnp.abs(den), jnp.exp(-m_t))[..., None]
        b_last = bcum[..., -1]
        d_end = b_last[..., None] - bcum + ib
        m_new = jnp.maximum(b_last + m_st, jnp.max(d_end, axis=-1))
        w_end = jnp.exp(d_end - m_new[..., None])
        decay = jnp.exp(b_last + m_st - m_new)
        c_new = decay[..., None, None] * c_st + jnp.einsum('bhj,bhjd,bhje->bhde', w_end, vb, kb)
        n_new = decay[..., None] * n_st + jnp.einsum('bhj,bhjd->bhd', w_end, kb)
        return (c_new, n_new, m_new), h

    init = (jnp.zeros((bsz, nh, dh, dh), f32), jnp.zeros((bsz, nh, dh), f32),
            jnp.zeros((bsz, nh), f32))
    _, h = lax.scan(step, init, (qc, kc, vc, ic, lfc))
    return jnp.transpose(h, (1, 0, 3, 2, 4)).reshape(bsz, s, nh, dh)


def rglru_mlstm_mixer(h, w_in, lru_conv_w, lru_conv_b, lru_w_a, lru_b_a, lru_w_x, lru_b_x,
                      lru_lam, m_conv_w, m_conv_b, m_i_bias, m_f_bias, m_head_g, w_out):
    bsz, s, _ = h.shape
    f32 = jnp.float32
    p = h @ w_in
    lru_x, lru_z, m_qk, m_v, m_o, m_i, m_f = jnp.split(p, IN_SPLITS, axis=-1)
    y_a = rg_lru(causal_dwconv(lru_x, lru_conv_w, lru_conv_b), lru_w_a, lru_b_a,
                 lru_w_x, lru_b_x, lru_lam) * jax.nn.gelu(lru_z.astype(f32))
    qk = jax.nn.silu(causal_dwconv(m_qk, m_conv_w, m_conv_b))
    q, k = jnp.split(qk, 2, axis=-1)

    def heads(t):
        return t.reshape(bsz, s, MLSTM_HEADS, MLSTM_HD)

    i_pre = m_i.astype(f32) + m_i_bias
    f_pre = m_f.astype(f32) + m_f_bias
    hm = mlstm_chunkwise(heads(q), heads(k), heads(m_v), i_pre, f_pre)
    hm = jax.nn.sigmoid(heads(m_o).astype(f32)) * hm
    hm = hm * lax.rsqrt(jnp.mean(hm * hm, axis=-1, keepdims=True) + EPS)
    y_b = hm.reshape(bsz, s, MLSTM_WIDTH) * m_head_g
    y = jnp.concatenate([y_a, y_b], axis=-1).astype(h.dtype)
    return (y @ w_out).astype(h.dtype)


def s5_mixer(h, w_in, a_re, a_im, log_step, b_re, b_im, c_re, c_im, d_skip, w_glu_v, w_glu_g):
    bsz, s, _ = h.shape
    f32 = jnp.float32
    nc = s // S5_CHUNK
    u = (h @ w_in).astype(f32).reshape(bsz, nc, S5_CHUNK, S5_GROUPS, S5_GROUP)
    u = jnp.swapaxes(u, 0, 1)
    a_re = a_re.astype(f32)
    a_im = a_im.astype(f32)
    dt = jnp.exp(log_step.astype(f32))[:, None]
    mag = jnp.exp(a_re * dt)
    lr, li = mag * jnp.cos(a_im * dt), mag * jnp.sin(a_im * dt)
    den = a_re * a_re + a_im * a_im
    fr = ((lr - 1.0) * a_re + li * a_im) / den
    fi = (li * a_re - (lr - 1.0) * a_im) / den
    bbr, bbi = cmul(fr[..., None], fi[..., None], b_re.astype(f32), b_im.astype(f32))
    c_re = c_re.astype(f32)
    c_im = c_im.astype(f32)
    dsk = d_skip.astype(f32).reshape(S5_GROUPS, S5_GROUP)

    def step(carry, ub):
        xr0, xi0 = carry
        bur = jnp.einsum('blgc,gpc->blgp', ub, bbr)
        bui = jnp.einsum('blgc,gpc->blgp', ub, bbi)
        ar = jnp.broadcast_to(lr, bur.shape)
        ai = jnp.broadcast_to(li, bur.shape)
        pr, pi, sr, si = lax.associative_scan(_cplx_combine, (ar, ai, bur, bui), axis=1)
        cr, ci = cmul(pr, pi, xr0[:, None], xi0[:, None])
        xr = sr + cr
        xi = si + ci
        y = jnp.einsum('blgp,gcp->blgc', xr, c_re) - jnp.einsum('blgp,gcp->blgc', xi, c_im)
        return (xr[:, -1], xi[:, -1]), y + dsk * ub

    init = (jnp.zeros((bsz, S5_GROUPS, S5_STATE), f32), jnp.zeros((bsz, S5_GROUPS, S5_STATE), f32))
    _, y = lax.scan(step, init, u)
    y = jax.nn.gelu(jnp.swapaxes(y, 0, 1).reshape(bsz, s, S5_WIDTH)).astype(h.dtype)
    return ((y @ w_glu_v) * jax.nn.sigmoid(y @ w_glu_g)).astype(h.dtype)


def hier_moe(h, w_coarse, b_coarse, w_fine, b_fine, w_gate, w_up, w_down):
    bsz, s, d = h.shape
    f32 = jnp.float32
    t = h.reshape(-1, d)
    lc = (t @ w_coarse).astype(f32) + b_coarse
    pc = jax.nn.softmax(lc, axis=-1)
    g_idx = jnp.argmax(lc, axis=-1)
    p_g = jnp.take_along_axis(pc, g_idx[:, None], axis=-1)
    lf = ((t @ w_fine).astype(f32) + b_fine).reshape(-1, MOE_GROUPS, MOE_PER_GROUP)
    lf_sel = jnp.take_along_axis(lf, g_idx[:, None, None], axis=1)[:, 0]
    top_v, top_i = lax.top_k(lf_sel, MOE_TOP_K)
    w2 = jax.nn.softmax(top_v, axis=-1) * p_g
    expert_id = g_idx[:, None] * MOE_PER_GROUP + top_i
    gates = jnp.sum(jax.nn.one_hot(expert_id, MOE_EXPERTS, dtype=f32) * w2[..., None], axis=1)
    y = jnp.zeros(t.shape, f32)
    for e in range(MOE_EXPERTS):
        he = jax.nn.silu(t @ w_gate[e]) * (t @ w_up[e])
        y = y + gates[:, e:e + 1] * (he @ w_down[e])
    return y.reshape(bsz, s, d).astype(h.dtype)


def setup_inputs(seed: int = 0) -> dict:
    key = jax.random.key(seed)
    ks = iter(jax.random.split(key, 64))
    f32 = jnp.float32
    NE, NO = N_EVEN, N_ODD

    def nrm(shape, scale):
        return jax.random.normal(next(ks), shape, f32) * scale

    x = nrm((BATCH, SEQ, D_MODEL), 1.0)
    norm_mix = 1.0 + nrm((DEPTH, D_MODEL), 0.02)
    norm_ffn = 1.0 + nrm((DEPTH, D_MODEL), 0.02)
    norm_final = 1.0 + nrm((D_MODEL,), 0.02)
    ab_w_in = nrm((NE, D_MODEL, IN_COLS), D_MODEL ** -0.5)
    lru_conv_w = nrm((NE, CONV_K, LRU_WIDTH), CONV_K ** -0.5)
    lru_conv_b = nrm((NE, LRU_WIDTH), 0.01)
    lru_w_a = nrm((NE, LRU_BLOCKS, LRU_BLOCK, LRU_BLOCK), LRU_BLOCK ** -0.5)
    lru_b_a = nrm((NE, LRU_WIDTH), 0.01)
    lru_w_x = nrm((NE, LRU_BLOCKS, LRU_BLOCK, LRU_BLOCK), LRU_BLOCK ** -0.5)
    lru_b_x = nrm((NE, LRU_WIDTH), 0.01)
    u = jax.random.uniform(next(ks), (NE, LRU_WIDTH), f32, 0.9, 0.999)
    a0 = u ** (1.0 / LRU_C)
    lru_lam = jnp.log(a0) - jnp.log1p(-a0)
    m_conv_w = nrm((NE, CONV_K, 2 * MLSTM_WIDTH), CONV_K ** -0.5)
    m_conv_b = nrm((NE, 2 * MLSTM_WIDTH), 0.01)
    m_i_bias = nrm((NE, MLSTM_HEADS), 0.1)
    m_f_bias = jnp.linspace(3.0, 6.0, MLSTM_HEADS, dtype=f32)[None] + nrm((NE, MLSTM_HEADS), 0.1)
    m_head_g = 1.0 + nrm((NE, MLSTM_WIDTH), 0.02)
    ab_w_out = nrm((NE, D_MODEL, D_MODEL), D_MODEL ** -0.5)
    s5_w_in = nrm((NO, D_MODEL, S5_WIDTH), D_MODEL ** -0.5)
    n_idx = jnp.arange(S5_STATE, dtype=f32)
    s5_a_re = -0.5 + nrm((NO, S5_GROUPS, S5_STATE), 0.01)
    s5_a_im = math.pi * n_idx + nrm((NO, S5_GROUPS, S5_STATE), 0.01)
    s5_log_step = jax.random.uniform(next(ks), (NO, S5_GROUPS), f32, math.log(0.001), math.log(0.1))
    s5_b_re = nrm((NO, S5_GROUPS, S5_STATE, S5_GROUP), S5_GROUP ** -0.5)
    s5_b_im = nrm((NO, S5_GROUPS, S5_STATE, S5_GROUP), S5_GROUP ** -0.5)
    s5_c_re = nrm((NO, S5_GROUPS, S5_GROUP, S5_STATE), S5_STATE ** -0.5)
    s5_c_im = nrm((NO, S5_GROUPS, S5_GROUP, S5_STATE), S5_STATE ** -0.5)
    s5_d = nrm((NO, S5_WIDTH), 1.0)
    s5_w_glu_v = nrm((NO, S5_WIDTH, D_MODEL), S5_WIDTH ** -0.5)
    s5_w_glu_g = nrm((NO, S5_WIDTH, D_MODEL), S5_WIDTH ** -0.5)
    moe_w_coarse = nrm((DEPTH, D_MODEL, MOE_GROUPS), D_MODEL ** -0.5)
    moe_b_coarse = nrm((DEPTH, MOE_GROUPS), 0.01)
    moe_w_fine = nrm((DEPTH, D_MODEL, MOE_EXPERTS), D_MODEL ** -0.5)
    moe_b_fine = nrm((DEPTH, MOE_EXPERTS), 0.01)
    moe_w_gate = nrm((DEPTH, MOE_EXPERTS, D_MODEL, MOE_FF), D_MODEL ** -0.5)
    moe_w_up = nrm((DEPTH, MOE_EXPERTS, D_MODEL, MOE_FF), D_MODEL ** -0.5)
    moe_w_down = nrm((DEPTH, MOE_EXPERTS, MOE_FF, D_MODEL), MOE_FF ** -0.5)
    return {'x': x, 'norm_mix': norm_mix, 'norm_ffn': norm_ffn, 'norm_final': norm_final,
            'ab_w_in': ab_w_in, 'lru_conv_w': lru_conv_w, 'lru_conv_b': lru_conv_b,
            'lru_w_a': lru_w_a, 'lru_b_a': lru_b_a, 'lru_w_x': lru_w_x, 'lru_b_x': lru_b_x,
            'lru_lam': lru_lam, 'm_conv_w': m_conv_w, 'm_conv_b': m_conv_b,
            'm_i_bias': m_i_bias, 'm_f_bias': m_f_bias, 'm_head_g': m_head_g, 'ab_w_out': ab_w_out,
            's5_w_in': s5_w_in, 's5_a_re': s5_a_re, 's5_a_im': s5_a_im, 's5_log_step': s5_log_step,
            's5_b_re': s5_b_re, 's5_b_im': s5_b_im, 's5_c_re': s5_c_re, 's5_c_im': s5_c_im,
            's5_d': s5_d, 's5_w_glu_v': s5_w_glu_v, 's5_w_glu_g': s5_w_glu_g,
            'moe_w_coarse': moe_w_coarse, 'moe_b_coarse': moe_b_coarse,
            'moe_w_fine': moe_w_fine, 'moe_b_fine': moe_b_fine,
            'moe_w_gate': moe_w_gate, 'moe_w_up': moe_w_up, 'moe_w_down': moe_w_down}


def reference(x, norm_mix, norm_ffn, norm_final, ab_w_in, lru_conv_w, lru_conv_b, lru_w_a, lru_b_a,
              lru_w_x, lru_b_x, lru_lam, m_conv_w, m_conv_b, m_i_bias, m_f_bias, m_head_g, ab_w_out,
              s5_w_in, s5_a_re, s5_a_im, s5_log_step, s5_b_re, s5_b_im, s5_c_re, s5_c_im, s5_d,
              s5_w_glu_v, s5_w_glu_g, moe_w_coarse, moe_b_coarse, moe_w_fine, moe_b_fine,
              moe_w_gate, moe_w_up, moe_w_down):
    for layer in range(DEPTH):
        j = layer // 2
        h = rmsnorm(x, norm_mix[layer])
        if layer % 2 == 0:
            x = x + rglru_mlstm_mixer(h, ab_w_in[j], lru_conv_w[j], lru_conv_b[j], lru_w_a[j],
                                      lru_b_a[j], lru_w_x[j], lru_b_x[j], lru_lam[j], m_conv_w[j],
                                      m_conv_b[j], m_i_bias[j], m_f_bias[j], m_head_g[j], ab_w_out[j])
        else:
            x = x + s5_mixer(h, s5_w_in[j], s5_a_re[j], s5_a_im[j], s5_log_step[j], s5_b_re[j],
                             s5_b_im[j], s5_c_re[j], s5_c_im[j], s5_d[j], s5_w_glu_v[j], s5_w_glu_g[j])
        x = x + hier_moe(rmsnorm(x, norm_ffn[layer]), moe_w_coarse[layer], moe_b_coarse[layer],
                         moe_w_fine[layer], moe_b_fine[layer], moe_w_gate[layer], moe_w_up[layer],
                         moe_w_down[layer])
    return rmsnorm(x, norm_final)
```

```python
import functools
import math

import jax
import jax.numpy as jnp
from jax import lax
from jax.experimental import pallas as pl
from jax.experimental.pallas import tpu as pltpu

F32 = jnp.float32
BF16 = jnp.bfloat16
EPS = 1e-6
LANES = 128
VMEM_LIMIT = 56 * 1024 * 1024

CONV_K = 4
LRU_BLOCKS = 8
LRU_C = 8.0
MLSTM_HEADS = 4
S5_GROUP = 16
S5_STATE = 64
S5_L = 16
MOE_GROUPS = 4
MOE_PER_GROUP = 4
N_PAIRS = 6
N_CLASSES = MOE_GROUPS * N_PAIRS
PAIR_LO = (0, 0, 0, 1, 1, 2)
PAIR_HI = (1, 2, 3, 2, 3, 3)


def _tile(n, pref):
    return pref if n % pref == 0 else n


def _cparams(*sem):
    return pltpu.CompilerParams(dimension_semantics=sem, vmem_limit_bytes=VMEM_LIMIT)


def _rms(x, g):
    return x * lax.rsqrt(jnp.mean(x * x, axis=-1, keepdims=True) + EPS) * g


def _dot(a, b):
    return jnp.dot(a, b, preferred_element_type=F32)


def _dot_nt(a, b):
    return lax.dot_general(a, b, (((1,), (1,)), ((), ())), preferred_element_type=F32)


def _dot_tn(a, b):
    return lax.dot_general(a, b, (((0,), (0,)), ((), ())), preferred_element_type=F32)


def _softplus(y):
    return jnp.maximum(y, 0.0) + jnp.log1p(jnp.exp(-jnp.abs(y)))


def _split3(x):
    p1 = x.astype(BF16)
    r1 = x - p1.astype(F32)
    p2 = r1.astype(BF16)
    p3 = (r1 - p2.astype(F32)).astype(BF16)
    return p1, p2, p3


def _causal_conv(x, tail, w, b):
    n = x.shape[0]
    xp = jnp.concatenate([tail, x], axis=0)
    out = b + w[CONV_K - 1:CONV_K, :] * x
    for j in range(CONV_K - 1):
        off = 8 - (CONV_K - 1) + j
        out = out + w[j:j + 1, :] * xp[off:off + n, :]
    return out


def _in0_body(x_ref, g_ref, w_ref, wg_ref, p_ref, gate_ref, hn_ref):
    @pl.when(pl.program_id(1) == 0)
    def _():
        hb = _rms(x_ref[...], g_ref[...]).astype(BF16)
        hn_ref[...] = hb
        gate_ref[...] = _dot(hb, wg_ref[...])

    p_ref[...] = _dot(hn_ref[...], w_ref[...])


def _in_proj0(x2, g, w_main, w_gate):
    t, d = x2.shape
    nm = w_main.shape[1]
    tm, tn = _tile(t, 1024), _tile(nm, 1024)
    return pl.pallas_call(
        _in0_body,
        grid=(t // tm, nm // tn),
        in_specs=[pl.BlockSpec((tm, d), lambda i, j: (i, 0)),
                  pl.BlockSpec((1, d), lambda i, j: (0, 0)),
                  pl.BlockSpec((d, tn), lambda i, j: (0, j)),
                  pl.BlockSpec((d, LANES), lambda i, j: (0, 0))],
        out_specs=[pl.BlockSpec((tm, tn), lambda i, j: (i, j)),
                   pl.BlockSpec((tm, LANES), lambda i, j: (i, 0))],
        out_shape=[jax.ShapeDtypeStruct((t, nm), F32), jax.ShapeDtypeStruct((t, LANES), F32)],
        scratch_shapes=[pltpu.VMEM((tm, d), BF16)],
        compiler_params=_cparams("parallel", "arbitrary"),
        name="in_proj0",
    )(x2, g, w_main, w_gate)


def _lru_body(x_ref, z_ref, cw_ref, cb_ref, wa_ref, ba_ref, wx_ref, bx_ref, lam_ref, y_ref,
              tail_ref, h_ref, a_s, g_s):
    ts, c = x_ref.shape
    blk = c // LRU_BLOCKS

    @pl.when(pl.program_id(1) == 0)
    def _():
        tail_ref[...] = jnp.zeros_like(tail_ref)
        h_ref[...] = jnp.zeros_like(h_ref)

    x = x_ref[...]
    conv = _causal_conv(x, tail_ref[...], cw_ref[...], cb_ref[...])
    tail_ref[...] = x[ts - 8:ts, :]
    cb16 = conv.astype(BF16)
    ra = jnp.concatenate([_dot(cb16[:, n * blk:(n + 1) * blk], wa_ref[n]) for n in range(LRU_BLOCKS)], axis=1)
    rx = jnp.concatenate([_dot(cb16[:, n * blk:(n + 1) * blk], wx_ref[n]) for n in range(LRU_BLOCKS)], axis=1)
    r = jax.nn.sigmoid(ra + ba_ref[...])
    ig = jax.nn.sigmoid(rx + bx_ref[...])
    log_a = -LRU_C * r * _softplus(-lam_ref[...])
    a = jnp.exp(log_a)
    a_s[...] = a
    g_s[...] = jnp.sqrt(1.0 - a * a) * (ig * conv)

    def step(t, h):
        h = a_s[pl.ds(t, 1), :] * h + g_s[pl.ds(t, 1), :]
        g_s[pl.ds(t, 1), :] = h
        return h

    h_ref[...] = lax.fori_loop(0, ts, step, h_ref[...], unroll=8)
    y_ref[...] = (g_s[...] * jax.nn.gelu(z_ref[...])).astype(y_ref.dtype)


def _rg_lru(p, bsz, seq, cw, cb, wa, ba, wx, bx, lam):
    c = cw.shape[1]
    ts = _tile(seq, 512)
    ns = seq // ts
    row = lambda b, s: (0, 0)
    return pl.pallas_call(
        _lru_body,
        grid=(bsz, ns),
        in_specs=[pl.BlockSpec((ts, c), lambda b, s: (b * ns + s, 0)),
                  pl.BlockSpec((ts, c), lambda b, s: (b * ns + s, 1)),
                  pl.BlockSpec((CONV_K, c), row),
                  pl.BlockSpec((1, c), row),
                  pl.BlockSpec(wa.shape, lambda b, s: (0, 0, 0)),
                  pl.BlockSpec((1, c), row),
                  pl.BlockSpec(wx.shape, lambda b, s: (0, 0, 0)),
                  pl.BlockSpec((1, c), row),
                  pl.BlockSpec((1, c), row)],
        out_specs=pl.BlockSpec((ts, c), lambda b, s: (b * ns + s, 0)),
        out_shape=jax.ShapeDtypeStruct((bsz * seq, c), BF16),
        scratch_shapes=[pltpu.VMEM((8, c), F32), pltpu.VMEM((1, c), F32),
                        pltpu.VMEM((ts, c), F32), pltpu.VMEM((ts, c), F32)],
        compiler_params=_cparams("parallel", "arbitrary"),
        name="rg_lru",
    )(p, p, cw, cb, wa, ba, wx, bx, lam)


def _mlstm_body(q_ref, k_ref, v_ref, o_ref, gt_ref, gb_ref, cwq_ref, cbq_ref, cwk_ref, cbk_ref, hg_ref,
                tri_ref, y_ref, ct_ref, n_ref, m_ref, tq_ref, tk_ref):
    L, dh = q_ref.shape
    head = pl.program_id(1)

    @pl.when(pl.program_id(2) == 0)
    def _():
        ct_ref[...] = jnp.zeros_like(ct_ref)
        n_ref[...] = jnp.zeros_like(n_ref)
        m_ref[...] = jnp.zeros_like(m_ref)
        tq_ref[...] = jnp.zeros_like(tq_ref)
        tk_ref[...] = jnp.zeros_like(tk_ref)

    gts = gt_ref[...] + gb_ref[...]
    lane = lax.broadcasted_iota(jnp.int32, gts.shape, 1)
    lsig = jnp.minimum(gts, 0.0) - jnp.log1p(jnp.exp(-jnp.abs(gts)))
    tri = tri_ref[...]
    p1, p2, p3 = _split3(lsig)
    bc_all = _dot(tri, p1) + _dot(tri, p2) + _dot(tri, p3)
    i_col = jnp.sum(jnp.where(lane == head, gts, 0.0), axis=-1, keepdims=True)
    bcum = jnp.sum(jnp.where(lane == head + MLSTM_HEADS, bc_all, 0.0), axis=-1, keepdims=True)
    a_col = i_col - bcum
    a_row = jnp.transpose(jnp.broadcast_to(a_col, (L, LANES)))[0:1, :]
    m_st = m_ref[0:1, 0:1]

    rows = lax.broadcasted_iota(jnp.int32, (L, L), 0)
    cols = lax.broadcasted_iota(jnp.int32, (L, L), 1)
    d = jnp.where(rows >= cols, bcum + a_row, -jnp.inf)
    g = bcum + m_st
    m_t = jnp.maximum(g, jnp.max(d, axis=-1, keepdims=True))
    w_intra = jnp.exp(d - m_t)
    w_inter = jnp.exp(g - m_t)

    q_raw = q_ref[...]
    k_raw = k_ref[...]
    q = _causal_conv(q_raw, tq_ref[...], cwq_ref[...], cbq_ref[...])
    k = _causal_conv(k_raw, tk_ref[...], cwk_ref[...], cbk_ref[...])
    tq_ref[...] = q_raw[L - 8:L, :]
    tk_ref[...] = k_raw[L - 8:L, :]
    q = q * jax.nn.sigmoid(q)
    k = k * jax.nn.sigmoid(k) * (dh ** -0.5)
    qb = q.astype(BF16)
    kb = k.astype(BF16)
    vb = v_ref[...].astype(BF16)

    s_qk = _dot_nt(qb, kb) * w_intra
    ct = ct_ref[...]
    n_row = n_ref[...]
    num = _dot(s_qk.astype(BF16), vb) + w_inter * _dot(qb, ct.astype(BF16))
    den = jnp.sum(s_qk, axis=-1, keepdims=True) + w_inter * jnp.sum(q * n_row, axis=-1, keepdims=True)
    hcell = num / jnp.maximum(jnp.abs(den), jnp.exp(-m_t))

    b_last = bcum[L - 1:L, :]
    d_end = b_last + a_col
    m_new = jnp.maximum(b_last + m_st, jnp.max(d_end, axis=0, keepdims=True))
    w_end = jnp.exp(d_end - m_new)
    decay = jnp.exp(b_last + m_st - m_new)
    kw = k * w_end
    ct_ref[...] = decay * ct + _dot_tn(kw.astype(BF16), vb)
    n_ref[...] = decay * n_row + jnp.sum(kw, axis=0, keepdims=True)
    m_ref[...] = jnp.broadcast_to(m_new, m_ref.shape)

    hm = jax.nn.sigmoid(o_ref[...]) * hcell
    hm = hm * lax.rsqrt(jnp.mean(hm * hm, axis=-1, keepdims=True) + EPS)
    y_ref[...] = (hm * hg_ref[...]).astype(y_ref.dtype)


def _mlstm(p, gates, gate_bias, bsz, seq, cw, cb, head_g, col0):
    nh = MLSTM_HEADS
    dh = head_g.shape[1] // nh
    L = _tile(seq, 256)
    nc = seq // L
    cq = col0 // dh
    tri = jnp.tril(jnp.ones((L, L), BF16))

    def pblock(off):
        return pl.BlockSpec((L, dh), lambda b, h, c: (b * nc + c, off + h))

    return pl.pallas_call(
        _mlstm_body,
        grid=(bsz, nh, nc),
        in_specs=[pblock(cq), pblock(cq + nh), pblock(cq + 2 * nh), pblock(cq + 3 * nh),
                  pl.BlockSpec((L, LANES), lambda b, h, c: (b * nc + c, 0)),
                  pl.BlockSpec((1, LANES), lambda b, h, c: (0, 0)),
                  pl.BlockSpec((CONV_K, dh), lambda b, h, c: (0, h)),
                  pl.BlockSpec((1, dh), lambda b, h, c: (0, h)),
                  pl.BlockSpec((CONV_K, dh), lambda b, h, c: (0, nh + h)),
                  pl.BlockSpec((1, dh), lambda b, h, c: (0, nh + h)),
                  pl.BlockSpec((1, dh), lambda b, h, c: (0, h)),
                  pl.BlockSpec((L, L), lambda b, h, c: (0, 0))],
        out_specs=pl.BlockSpec((L, dh), lambda b, h, c: (b * nc + c, h)),
        out_shape=jax.ShapeDtypeStruct((bsz * seq, nh * dh), BF16),
        scratch_shapes=[pltpu.VMEM((dh, dh), F32), pltpu.VMEM((1, dh), F32), pltpu.VMEM((8, LANES), F32),
                        pltpu.VMEM((8, dh), F32), pltpu.VMEM((8, dh), F32)],
        compiler_params=_cparams("parallel", "parallel", "arbitrary"),
        name="mlstm",
    )(p, p, p, p, gates, gate_bias, cw, cb, cw, cb, head_g, tri)


def _outproj_body(ya_ref, yb_ref, wa_ref, wb_ref, x_ref, o_ref):
    o_ref[...] = x_ref[...] + _dot(ya_ref[...], wa_ref[...]) + _dot(yb_ref[...], wb_ref[...])


def _out_proj(ya, yb, wa, wb, x2):
    t, d = x2.shape
    ka, kb = ya.shape[1], yb.shape[1]
    tm, tn = _tile(t, 1024), _tile(d, 1024)
    return pl.pallas_call(
        _outproj_body,
        grid=(t // tm, d // tn),
        in_specs=[pl.BlockSpec((tm, ka), lambda i, j: (i, 0)),
                  pl.BlockSpec((tm, kb), lambda i, j: (i, 0)),
                  pl.BlockSpec((ka, tn), lambda i, j: (0, j)),
                  pl.BlockSpec((kb, tn), lambda i, j: (0, j)),
                  pl.BlockSpec((tm, tn), lambda i, j: (i, j))],
        out_specs=pl.BlockSpec((tm, tn), lambda i, j: (i, j)),
        out_shape=jax.ShapeDtypeStruct((t, d), F32),
        compiler_params=_cparams("parallel", "arbitrary"),
        name="out_proj",
    )(ya, yb, wa, wb, x2)


def _s5in_body(x_ref, g_ref, wt_ref, z_ref):
    hb = _rms(x_ref[...], g_ref[...]).astype(BF16)
    res = _dot_nt(wt_ref[...], hb)
    z_ref[...] = res.reshape(z_ref.shape)


def _s5_in(x2, g, wt):
    t, d = x2.shape
    ch = wt.shape[0]
    ngrp = ch // S5_GROUP
    nr = t // S5_L
    r = _tile(nr, 512)
    xv = x2.reshape(nr, S5_L * d)
    return pl.pallas_call(
        _s5in_body,
        grid=(nr // r, S5_L),
        in_specs=[pl.BlockSpec((r, d), lambda i, l: (i, l)),
                  pl.BlockSpec((1, d), lambda i, l: (0, 0)),
                  pl.BlockSpec((ch, d), lambda i, l: (0, 0))],
        out_specs=pl.BlockSpec((ngrp, S5_GROUP, r), lambda i, l: (0, l, i)),
        out_shape=jax.ShapeDtypeStruct((ngrp, S5_L * S5_GROUP, nr), F32),
        compiler_params=_cparams("parallel", "arbitrary"),
        name="s5_in",
    )(xv, g, wt)


def _s5core_body(z_ref, min_ref, mintra_ref, cst_ref, lp_ref, y_ref, *, nseg, nlev):
    p = S5_STATE
    zb = z_ref[0].astype(BF16)
    e = _dot(min_ref[0], zb)
    sr, si = e[:p, :], e[p:, :]
    nr = sr.shape[1]
    lane = lax.broadcasted_iota(jnp.int32, (1, nr), 1) % nseg
    lp = lp_ref[0]
    for j in range(nlev):
        s = 1 << j
        cr = lp[:, j:j + 1]
        ci = lp[:, nlev + j:nlev + j + 1]
        keep = lane >= s
        tr = jnp.where(keep, pltpu.roll(sr, s, axis=1), 0.0)
        ti = jnp.where(keep, pltpu.roll(si, s, axis=1), 0.0)
        sr, si = sr + cr * tr - ci * ti, si + cr * ti + ci * tr
    keep = lane >= 1
    xr = jnp.where(keep, pltpu.roll(sr, 1, axis=1), 0.0)
    xi = jnp.where(keep, pltpu.roll(si, 1, axis=1), 0.0)
    xb = jnp.concatenate([xr, xi], axis=0).astype(BF16)
    y = _dot(mintra_ref[0], zb) + _dot(cst_ref[0], xb)
    y_ref[0] = jax.nn.gelu(y).astype(y_ref.dtype)


def _s5_core(z, m_in, m_intra, c_st, lpow, seq):
    ngrp, kk, nr = z.shape
    nseg = seq // S5_L
    nlev = lpow.shape[2] // 2
    return pl.pallas_call(
        functools.partial(_s5core_body, nseg=nseg, nlev=nlev),
        grid=(ngrp,),
        in_specs=[pl.BlockSpec((1, kk, nr), lambda g: (g, 0, 0)),
                  pl.BlockSpec((1,) + m_in.shape[1:], lambda g: (g, 0, 0)),
                  pl.BlockSpec((1,) + m_intra.shape[1:], lambda g: (g, 0, 0)),
                  pl.BlockSpec((1,) + c_st.shape[1:], lambda g: (g, 0, 0)),
                  pl.BlockSpec((1,) + lpow.shape[1:], lambda g: (g, 0, 0))],
        out_specs=pl.BlockSpec((1, kk, nr), lambda g: (g, 0, 0)),
        out_shape=jax.ShapeDtypeStruct((ngrp, kk, nr), BF16),
        compiler_params=_cparams("parallel"),
        name="s5_core",
    )(z, m_in, m_intra, c_st, lpow)


def _glu_body(y_ref, wv_ref, wg_ref, x_ref, o_ref):
    ngrp, grp, r = y_ref.shape
    y = y_ref[...].reshape(ngrp * grp, r)
    v = _dot(wv_ref[...], y)
    g = _dot(wg_ref[...], y)
    o_ref[...] = x_ref[...] + jnp.transpose(v * jax.nn.sigmoid(g))


def _s5_glu(y, wvt, wgt, x2):
    t, d = x2.shape
    ngrp, _, nr = y.shape
    r = _tile(nr, 512)
    tn = _tile(d, 1024)
    nj = d // tn
    xv = x2.reshape(nr, S5_L * d)
    out = pl.pallas_call(
        _glu_body,
        grid=(nr // r, S5_L, nj),
        in_specs=[pl.BlockSpec((ngrp, S5_GROUP, r), lambda i, l, j: (0, l, i)),
                  pl.BlockSpec((tn, ngrp * S5_GROUP), lambda i, l, j: (j, 0)),
                  pl.BlockSpec((tn, ngrp * S5_GROUP), lambda i, l, j: (j, 0)),
                  pl.BlockSpec((r, tn), lambda i, l, j: (i, l * nj + j))],
        out_specs=pl.BlockSpec((r, tn), lambda i, l, j: (i, l * nj + j)),
        out_shape=jax.ShapeDtypeStruct((nr, S5_L * d), F32),
        compiler_params=_cparams("parallel", "arbitrary", "arbitrary"),
        name="s5_glu",
    )(y, wvt, wgt, xv)
    return out.reshape(t, d)


def _s5_operators(a_re, a_im, log_step, b_re, b_im, c_re, c_im, d_skip, nseg):
    hi = lax.Precision.HIGHEST
    L, p, grp = S5_L, S5_STATE, S5_GROUP
    ngrp = a_re.shape[0]
    dt = jnp.exp(log_step)[:, None]
    mag = jnp.exp(a_re * dt)
    lr, li = mag * jnp.cos(a_im * dt), mag * jnp.sin(a_im * dt)
    den = a_re * a_re + a_im * a_im
    fr = ((lr - 1.0) * a_re + li * a_im) / den
    fi = (li * a_re - (lr - 1.0) * a_im) / den
    bbr = fr[..., None] * b_re - fi[..., None] * b_im
    bbi = fr[..., None] * b_im + fi[..., None] * b_re
    pr, pi = [jnp.ones_like(lr)], [jnp.zeros_like(li)]
    for _ in range(L):
        pr_n = pr[-1] * lr - pi[-1] * li
        pi_n = pr[-1] * li + pi[-1] * lr
        pr.append(pr_n)
        pi.append(pi_n)
    pwr, pwi = jnp.stack(pr, 1), jnp.stack(pi, 1)
    lbr = pwr[:, :L, :, None] * bbr[:, None] - pwi[:, :L, :, None] * bbi[:, None]
    lbi = pwr[:, :L, :, None] * bbi[:, None] + pwi[:, :L, :, None] * bbr[:, None]
    m_in = jnp.concatenate([jnp.transpose(lbr[:, ::-1], (0, 2, 1, 3)).reshape(ngrp, p, L * grp),
                            jnp.transpose(lbi[:, ::-1], (0, 2, 1, 3)).reshape(ngrp, p, L * grp)], axis=1)
    taps = (jnp.einsum('gcp,gkpd->gkcd', c_re, lbr, precision=hi)
            - jnp.einsum('gcp,gkpd->gkcd', c_im, lbi, precision=hi))
    taps = taps.at[:, 0].add(jax.vmap(jnp.diag)(d_skip.reshape(ngrp, grp)))
    lag = jnp.arange(L)[:, None] - jnp.arange(L)[None, :]
    blocks = jnp.where((lag >= 0)[None, :, :, None, None], taps[:, jnp.clip(lag, 0, L - 1)], 0.0)
    m_intra = jnp.transpose(blocks, (0, 1, 3, 2, 4)).reshape(ngrp, L * grp, L * grp)
    qr, qi = pwr[:, 1:], pwi[:, 1:]
    cs_r = c_re[:, None] * qr[:, :, None, :] - c_im[:, None] * qi[:, :, None, :]
    cs_i = -c_re[:, None] * qi[:, :, None, :] - c_im[:, None] * qr[:, :, None, :]
    c_st = jnp.concatenate([cs_r, cs_i], axis=-1).reshape(ngrp, L * grp, 2 * p)
    nlev = max(1, int(math.log2(nseg)))
    sr_, si_ = [pwr[:, L]], [pwi[:, L]]
    for _ in range(nlev - 1):
        sr_.append(sr_[-1] * sr_[-1] - si_[-1] * si_[-1])
        si_.append(2.0 * sr_[-2] * si_[-1])
    lpow = jnp.stack(sr_ + si_, axis=-1)
    return m_in.astype(BF16), m_intra.astype(BF16), c_st.astype(BF16), lpow


def _router_body(x_ref, g_ref, whi_ref, wlo_ref, b_ref, tri_ref, o_ref, cnt_ref, carry_ref):
    @pl.when(pl.program_id(0) == 0)
    def _():
        carry_ref[...] = jnp.zeros_like(carry_ref)

    h = _rms(x_ref[...], g_ref[...])
    h_hi = h.astype(BF16)
    h_lo = (h - h_hi.astype(F32)).astype(BF16)
    whi = whi_ref[...]
    logits = _dot(h_hi, whi) + _dot(h_lo, whi) + _dot(h_hi, wlo_ref[...]) + b_ref[...]
    lane = lax.broadcasted_iota(jnp.int32, logits.shape, 1).astype(F32)
    neg, big = -jnp.inf, 1e9

    def first_argmax(vals):
        mx = jnp.max(vals, axis=-1, keepdims=True)
        return mx, jnp.min(jnp.where(vals == mx, lane, big), axis=-1, keepdims=True)

    is_c = lane < MOE_GROUPS
    mx, gi = first_argmax(jnp.where(is_c, logits, neg))
    p_g = 1.0 / jnp.sum(jnp.where(is_c, jnp.exp(logits - mx), 0.0), axis=-1, keepdims=True)
    base = MOE_GROUPS + MOE_PER_GROUP * gi
    lf = jnp.where((lane >= base) & (lane < base + MOE_PER_GROUP), logits, neg)
    v1, i1 = first_argmax(lf)
    v2, i2 = first_argmax(jnp.where(lane == i1, neg, lf))
    e = jnp.exp(v2 - v1)
    w1 = p_g / (1.0 + e)
    w2 = p_g * e / (1.0 + e)
    k1, k2 = i1 - base, i2 - base
    first_low = k1 < k2
    lo, hi = jnp.minimum(k1, k2), jnp.maximum(k1, k2)
    w_lo, w_hi = jnp.where(first_low, w1, w2), jnp.where(first_low, w2, w1)
    cls = gi * N_PAIRS + lo * (7.0 - lo) * 0.5 + (hi - lo - 1.0)
    onehot = jnp.where(lane == cls, 1.0, 0.0)
    before = _dot(tri_ref[...], onehot.astype(BF16))
    carry = carry_ref[...]
    rank = jnp.sum(onehot * (before + carry), axis=-1, keepdims=True)
    carry = carry + jnp.sum(onehot, axis=0, keepdims=True)
    carry_ref[...] = carry
    cnt_ref[...] = carry
    out = jnp.where(lane == 0, cls, jnp.where(lane == 1, rank, jnp.where(lane == 2, w_lo,
                    jnp.where(lane == 3, w_hi, 0.0))))
    o_ref[...] = out


def _router(x2, g, w_hi, w_lo, bias):
    t, d = x2.shape
    tm = _tile(t, 512)
    tri = jnp.tril(jnp.ones((tm, tm), BF16), -1)
    return pl.pallas_call(
        _router_body,
        grid=(t // tm,),
        in_specs=[pl.BlockSpec((tm, d), lambda i: (i, 0)),
                  pl.BlockSpec((1, d), lambda i: (0, 0)),
                  pl.BlockSpec((d, LANES), lambda i: (0, 0)),
                  pl.BlockSpec((d, LANES), lambda i: (0, 0)),
                  pl.BlockSpec((1, LANES), lambda i: (0, 0)),
                  pl.BlockSpec((tm, tm), lambda i: (0, 0))],
        out_specs=[pl.BlockSpec((tm, LANES), lambda i: (i, 0)),
                   pl.BlockSpec((1, LANES), lambda i: (0, 0))],
        out_shape=[jax.ShapeDtypeStruct((t, LANES), F32), jax.ShapeDtypeStruct((1, LANES), F32)],
        scratch_shapes=[pltpu.VMEM((1, LANES), F32)],
        compiler_params=_cparams("arbitrary"),
        name="router",
    )(x2, g, w_hi, w_lo, bias, tri)


def _moe_body(src_ref, nv_ref, elo_ref, ehi_ref,
              x_hbm, rw_ref, g_ref, gfin_ref, wg_lo, wu_lo, wd_lo, wg_hi, wu_hi, wd_hi,
              o_hbm, xbuf, obuf, sem_in, sem_out, *, final):
    i = pl.program_id(0)
    tm = xbuf.shape[0]
    nv = nv_ref[i]

    def row_copy_in(r):
        tok = src_ref[i * tm + r]
        return pltpu.make_async_copy(x_hbm.at[pl.ds(tok, 1), :], xbuf.at[pl.ds(r, 1), :], sem_in)

    def row_copy_out(r):
        tok = src_ref[i * tm + r]
        return pltpu.make_async_copy(obuf.at[pl.ds(r, 1), :], o_hbm.at[pl.ds(tok, 1), :], sem_out)

    @pl.when(nv > 0)
    def _():
        def issue_in(r, c):
            row_copy_in(r).start()
            return c

        lax.fori_loop(0, tm, issue_in, 0)

        def wait_in(r, c):
            row_copy_in(r).wait()
            return c

        lax.fori_loop(0, tm, wait_in, 0)

        x = xbuf[...]
        hb = _rms(x, g_ref[...]).astype(BF16)

        def expert(wg, wu, wd):
            a = _dot(hb, wg[0])
            u = _dot(hb, wu[0])
            he = (a * jax.nn.sigmoid(a) * u).astype(BF16)
            return _dot(he, wd[0])

        rw = rw_ref[...]
        out = x + rw[:, 2:3] * expert(wg_lo, wu_lo, wd_lo) + rw[:, 3:4] * expert(wg_hi, wu_hi, wd_hi)
        if final:
            out = _rms(out, gfin_ref[...])
        obuf[...] = out

        def issue_out(r, c):
            row_copy_out(r).start()
            return c

        lax.fori_loop(0, nv, issue_out, 0)

        def wait_out(r, c):
            row_copy_out(r).wait()
            return c

        lax.fori_loop(0, nv, wait_out, 0)


def _moe(x2, rws, src, nvalid, elo, ehi, g, gfin, wg, wu, wd, tm, final):
    t, d = x2.shape
    nt = nvalid.shape[0]
    ff = wg.shape[2]
    cm = lambda i, *_: (0, 0)
    lo3 = lambda i, src, nv, elo, ehi: (elo[i], 0, 0)
    hi3 = lambda i, src, nv, elo, ehi: (ehi[i], 0, 0)
    return pl.pallas_call(
        functools.partial(_moe_body, final=final),
        grid_spec=pltpu.PrefetchScalarGridSpec(
            num_scalar_prefetch=4,
            grid=(nt,),
            in_specs=[pl.BlockSpec(memory_space=pl.ANY),
                      pl.BlockSpec((tm, LANES), lambda i, *_: (i, 0)),
                      pl.BlockSpec((1, d), cm),
                      pl.BlockSpec((1, d), cm),
                      pl.BlockSpec((1, d, ff), lo3), pl.BlockSpec((1, d, ff), lo3), pl.BlockSpec((1, ff, d), lo3),
                      pl.BlockSpec((1, d, ff), hi3), pl.BlockSpec((1, d, ff), hi3), pl.BlockSpec((1, ff, d), hi3)],
            out_specs=pl.BlockSpec(memory_space=pl.ANY),
            scratch_shapes=[pltpu.VMEM((tm, d), F32), pltpu.VMEM((tm, d), F32),
                            pltpu.SemaphoreType.DMA(()), pltpu.SemaphoreType.DMA(())],
        ),
        out_shape=jax.ShapeDtypeStruct((t, d), F32),
        compiler_params=_cparams("arbitrary"),
        name="moe_final" if final else "moe",
    )(src, nvalid, elo, ehi, x2, rws, g, gfin, wg, wu, wd, wg, wu, wd)


def _moe_layer(x2, g, gfin, w_coarse, b_coarse, w_fine, b_fine, wg, wu, wd, final):
    t, d = x2.shape
    wr = jnp.zeros((d, LANES), F32).at[:, :MOE_GROUPS].set(w_coarse)
    wr = wr.at[:, MOE_GROUPS:MOE_GROUPS + w_fine.shape[1]].set(w_fine)
    wr_hi = wr.astype(BF16)
    wr_lo = (wr - wr_hi.astype(F32)).astype(BF16)
    bias = jnp.zeros((1, LANES), F32).at[0, :MOE_GROUPS].set(b_coarse)
    bias = bias.at[0, MOE_GROUPS:MOE_GROUPS + b_fine.shape[0]].set(b_fine)
    rw, cnt = _router(x2, g, wr_hi, wr_lo, bias)

    tm = _tile(t, 256)
    nt = t // tm + N_CLASSES
    cls = rw[:, 0].astype(jnp.int32)
    rank = rw[:, 1].astype(jnp.int32)
    counts = cnt[0, :N_CLASSES].astype(jnp.int32)
    tiles_per = (counts + tm - 1) // tm
    tile_end = jnp.cumsum(tiles_per)
    tile_start = tile_end - tiles_per
    dest = tile_start[cls] * tm + rank
    src = jnp.zeros((nt * tm,), jnp.int32).at[dest].set(jnp.arange(t, dtype=jnp.int32))
    tile_ids = jnp.arange(nt, dtype=jnp.int32)
    total = tile_end[-1]
    tcls = jnp.minimum(jnp.searchsorted(tile_end, tile_ids, side='right'), N_CLASSES - 1).astype(jnp.int32)
    last_cls = tcls[jnp.maximum(total - 1, 0)]
    used = tile_ids < total
    tcls = jnp.where(used, tcls, last_cls)
    nvalid = jnp.where(used, jnp.clip(counts[tcls] - (tile_ids - tile_start[tcls]) * tm, 0, tm), 0).astype(jnp.int32)
    grp, pair = tcls // N_PAIRS, tcls % N_PAIRS
    elo = (grp * MOE_PER_GROUP + jnp.array(PAIR_LO, jnp.int32)[pair]).astype(jnp.int32)
    ehi = (grp * MOE_PER_GROUP + jnp.array(PAIR_HI, jnp.int32)[pair]).astype(jnp.int32)
    rws = jnp.take(rw, src, axis=0)
    return _moe(x2, rws, src, nvalid, elo, ehi, g, gfin, wg, wu, wd, tm, final)


def _mixer0(x2, bsz, seq, g, w_in, lru_conv_w, lru_conv_b, lru_w_a, lru_b_a, lru_w_x, lru_b_x, lru_lam,
            m_conv_w, m_conv_b, m_i_bias, m_f_bias, m_head_g, w_out):
    d = x2.shape[1]
    lru_w = lru_conv_w.shape[1]
    mw = m_head_g.shape[0]
    nmain = 2 * lru_w + 4 * mw
    w_main = w_in[:, :nmain].astype(BF16)
    w_gate = jnp.zeros((d, LANES), F32).at[:, :2 * MLSTM_HEADS].set(w_in[:, nmain:]).astype(BF16)
    p, gates = _in_proj0(x2, g[None], w_main, w_gate)
    ya = _rg_lru(p, bsz, seq, lru_conv_w, lru_conv_b[None], lru_w_a.astype(BF16), lru_b_a[None],
                 lru_w_x.astype(BF16), lru_b_x[None], lru_lam[None])
    gate_bias = jnp.zeros((1, LANES), F32).at[0, :MLSTM_HEADS].set(m_i_bias)
    gate_bias = gate_bias.at[0, MLSTM_HEADS:2 * MLSTM_HEADS].set(m_f_bias)
    yb = _mlstm(p, gates, gate_bias, bsz, seq, m_conv_w, m_conv_b[None], m_head_g[None], 2 * lru_w)
    w_out16 = w_out.astype(BF16)
    return _out_proj(ya, yb, w_out16[:lru_w], w_out16[lru_w:], x2)


def _mixer1(x2, seq, g, w_in, a_re, a_im, log_step, b_re, b_im, c_re, c_im, d_skip, w_glu_v, w_glu_g):
    ops = _s5_operators(a_re, a_im, log_step, b_re, b_im, c_re, c_im, d_skip, seq // S5_L)
    z = _s5_in(x2, g[None], jnp.transpose(w_in).astype(BF16))
    y = _s5_core(z, *ops, seq)
    return _s5_glu(y, jnp.transpose(w_glu_v).astype(BF16), jnp.transpose(w_glu_g).astype(BF16), x2)


def kernel(x, norm_mix, norm_ffn, norm_final, ab_w_in, lru_conv_w, lru_conv_b, lru_w_a, lru_b_a, lru_w_x, lru_b_x,
           lru_lam, m_conv_w, m_conv_b, m_i_bias, m_f_bias, m_head_g, ab_w_out, s5_w_in, s5_a_re, s5_a_im,
           s5_log_step, s5_b_re, s5_b_im, s5_c_re, s5_c_im, s5_d, s5_w_glu_v, s5_w_glu_g, moe_w_coarse,
           moe_b_coarse, moe_w_fine, moe_b_fine, moe_w_gate, moe_w_up, moe_w_down):
    bsz, seq, d = x.shape
    depth = norm_mix.shape[0]
    x2 = x.reshape(bsz * seq, d)
    gfin = norm_final[None]
    for layer in range(depth):
        j = layer // 2
        if layer % 2 == 0:
            x2 = _mixer0(x2, bsz, seq, norm_mix[layer], ab_w_in[j], lru_conv_w[j], lru_conv_b[j], lru_w_a[j],
                         lru_b_a[j], lru_w_x[j], lru_b_x[j], lru_lam[j], m_conv_w[j], m_conv_b[j], m_i_bias[j],
                         m_f_bias[j], m_head_g[j], ab_w_out[j])
        else:
            x2 = _mixer1(x2, seq, norm_mix[layer], s5_w_in[j], s5_a_re[j], s5_a_im[j], s5_log_step[j], s5_b_re[j],
                         s5_b_im[j], s5_c_re[j], s5_c_im[j], s5_d[j], s5_w_glu_v[j], s5_w_glu_g[j])
        x2 = _moe_layer(x2, norm_ffn[layer][None], gfin, moe_w_coarse[layer], moe_b_coarse[layer],
                        moe_w_fine[layer], moe_b_fine[layer], moe_w_gate[layer].astype(BF16),
                        moe_w_up[layer].astype(BF16), moe_w_down[layer].astype(BF16), layer == depth - 1)
    return x2.reshape(bsz, seq, d)
```

```python
import functools
import math

import jax
import jax.numpy as jnp
from jax import lax
from jax.experimental import pallas as pl
from jax.experimental.pallas import tpu as pltpu

F32 = jnp.float32
BF16 = jnp.bfloat16
EPS = 1e-6
LANES = 128
VMEM_LIMIT = 56 * 1024 * 1024

CONV_K = 4
LRU_BLOCKS = 8
LRU_C = 8.0
MLSTM_HEADS = 4
S5_GROUP = 16
S5_STATE = 64
S5_L = 16
MOE_GROUPS = 4
MOE_PER_GROUP = 4
N_PAIRS = 6
N_CLASSES = MOE_GROUPS * N_PAIRS
PAIR_LO = (0, 0, 0, 1, 1, 2)
PAIR_HI = (1, 2, 3, 2, 3, 3)


def _tile(n, pref):
    return pref if n % pref == 0 else n


def _cparams(*sem):
    return pltpu.CompilerParams(dimension_semantics=sem, vmem_limit_bytes=VMEM_LIMIT)


def _rms(x, g):
    return x * lax.rsqrt(jnp.mean(x * x, axis=-1, keepdims=True) + EPS) * g


def _dot(a, b):
    return jnp.dot(a, b, preferred_element_type=F32)


def _dot_nt(a, b):
    return lax.dot_general(a, b, (((1,), (1,)), ((), ())), preferred_element_type=F32)


def _dot_tn(a, b):
    return lax.dot_general(a, b, (((0,), (0,)), ((), ())), preferred_element_type=F32)


def _softplus(y):
    return jnp.maximum(y, 0.0) + jnp.log1p(jnp.exp(-jnp.abs(y)))


def _split3(x):
    p1 = x.astype(BF16)
    r1 = x - p1.astype(F32)
    p2 = r1.astype(BF16)
    p3 = (r1 - p2.astype(F32)).astype(BF16)
    return p1, p2, p3


def _causal_conv(x, tail, w, b):
    n = x.shape[0]
    xp = jnp.concatenate([tail, x], axis=0)
    out = b + w[CONV_K - 1:CONV_K, :] * x
    for j in range(CONV_K - 1):
        off = 8 - (CONV_K - 1) + j
        out = out + w[j:j + 1, :] * xp[off:off + n, :]
    return out


def _in0_body(x_ref, g_ref, w_ref, wg_ref, p_ref, gate_ref, hn_ref):
    @pl.when(pl.program_id(1) == 0)
    def _():
        hb = _rms(x_ref[...], g_ref[...]).astype(BF16)
        hn_ref[...] = hb
        gate_ref[...] = _dot(hb, wg_ref[...])

    p_ref[...] = _dot(hn_ref[...], w_ref[...])


def _in_proj0(x2, g, w_main, w_gate):
    t, d = x2.shape
    nm = w_main.shape[1]
    tm, tn = _tile(t, 1024), _tile(nm, 1024)
    return pl.pallas_call(
        _in0_body,
        grid=(t // tm, nm // tn),
        in_specs=[pl.BlockSpec((tm, d), lambda i, j: (i, 0)),
                  pl.BlockSpec((1, d), lambda i, j: (0, 0)),
                  pl.BlockSpec((d, tn), lambda i, j: (0, j)),
                  pl.BlockSpec((d, LANES), lambda i, j: (0, 0))],
        out_specs=[pl.BlockSpec((tm, tn), lambda i, j: (i, j)),
                   pl.BlockSpec((tm, LANES), lambda i, j: (i, 0))],
        out_shape=[jax.ShapeDtypeStruct((t, nm), F32), jax.ShapeDtypeStruct((t, LANES), F32)],
        scratch_shapes=[pltpu.VMEM((tm, d), BF16)],
        compiler_params=_cparams("parallel", "arbitrary"),
        name="in_proj0",
    )(x2, g, w_main, w_gate)


def _lru_body(x_ref, z_ref, cw_ref, cb_ref, wa_ref, ba_ref, wx_ref, bx_ref, lam_ref, y_ref,
              tail_ref, h_ref, a_s, g_s):
    ts, c = x_ref.shape
    blk = c // LRU_BLOCKS

    @pl.when(pl.program_id(1) == 0)
    def _():
        tail_ref[...] = jnp.zeros_like(tail_ref)
        h_ref[...] = jnp.zeros_like(h_ref)

    x = x_ref[...]
    conv = _causal_conv(x, tail_ref[...], cw_ref[...], cb_ref[...])
    tail_ref[...] = x[ts - 8:ts, :]
    cb16 = conv.astype(BF16)
    ra = jnp.concatenate([_dot(cb16[:, n * blk:(n + 1) * blk], wa_ref[n]) for n in range(LRU_BLOCKS)], axis=1)
    rx = jnp.concatenate([_dot(cb16[:, n * blk:(n + 1) * blk], wx_ref[n]) for n in range(LRU_BLOCKS)], axis=1)
    r = jax.nn.sigmoid(ra + ba_ref[...])
    ig = jax.nn.sigmoid(rx + bx_ref[...])
    log_a = -LRU_C * r * _softplus(-lam_ref[...])
    a = jnp.exp(log_a)
    pa = a
    pg = jnp.sqrt(1.0 - a * a) * (ig * conv)
    row8 = lax.broadcasted_iota(jnp.int32, (ts, c), 0) & 7
    for s in (1, 2, 4):
        keep = row8 >= s
        ga = jnp.where(keep, pltpu.roll(pg, s, axis=0), 0.0)
        aa = jnp.where(keep, pltpu.roll(pa, s, axis=0), 1.0)
        pg = pa * ga + pg
        pa = pa * aa
    a_s[...] = pa
    g_s[...] = pg

    def group(k, h):
        r0 = pl.multiple_of(k * 8, 8)
        h8 = a_s[pl.ds(r0, 8), :] * h + g_s[pl.ds(r0, 8), :]
        g_s[pl.ds(r0, 8), :] = h8
        return h8[7:8, :]

    h_ref[...] = lax.fori_loop(0, ts // 8, group, h_ref[...], unroll=4)
    y_ref[...] = (g_s[...] * jax.nn.gelu(z_ref[...])).astype(y_ref.dtype)


def _rg_lru(p, bsz, seq, cw, cb, wa, ba, wx, bx, lam):
    c = cw.shape[1]
    ts = _tile(seq, 512)
    ns = seq // ts
    row = lambda b, s: (0, 0)
    return pl.pallas_call(
        _lru_body,
        grid=(bsz, ns),
        in_specs=[pl.BlockSpec((ts, c), lambda b, s: (b * ns + s, 0)),
                  pl.BlockSpec((ts, c), lambda b, s: (b * ns + s, 1)),
                  pl.BlockSpec((CONV_K, c), row),
                  pl.BlockSpec((1, c), row),
                  pl.BlockSpec(wa.shape, lambda b, s: (0, 0, 0)),
                  pl.BlockSpec((1, c), row),
                  pl.BlockSpec(wx.shape, lambda b, s: (0, 0, 0)),
                  pl.BlockSpec((1, c), row),
                  pl.BlockSpec((1, c), row)],
        out_specs=pl.BlockSpec((ts, c), lambda b, s: (b * ns + s, 0)),
        out_shape=jax.ShapeDtypeStruct((bsz * seq, c), BF16),
        scratch_shapes=[pltpu.VMEM((8, c), F32), pltpu.VMEM((1, c), F32),
                        pltpu.VMEM((ts, c), F32), pltpu.VMEM((ts, c), F32)],
        compiler_params=_cparams("parallel", "arbitrary"),
        name="rg_lru",
    )(p, p, cw, cb, wa, ba, wx, bx, lam)


def _mlstm_body(q_ref, k_ref, v_ref, o_ref, gt_ref, gb_ref, cwq_ref, cbq_ref, cwk_ref, cbk_ref, hg_ref,
                tri_ref, y_ref, ct_ref, n_ref, m_ref, tq_ref, tk_ref):
    L, dh = q_ref.shape
    head = pl.program_id(1)

    @pl.when(pl.program_id(2) == 0)
    def _():
        ct_ref[...] = jnp.zeros_like(ct_ref)
        n_ref[...] = jnp.zeros_like(n_ref)
        m_ref[...] = jnp.zeros_like(m_ref)
        tq_ref[...] = jnp.zeros_like(tq_ref)
        tk_ref[...] = jnp.zeros_like(tk_ref)

    gts = gt_ref[...] + gb_ref[...]
    lane = lax.broadcasted_iota(jnp.int32, gts.shape, 1)
    lsig = jnp.minimum(gts, 0.0) - jnp.log1p(jnp.exp(-jnp.abs(gts)))
    tri = tri_ref[...]
    p1, p2, p3 = _split3(lsig)
    bc_all = _dot(tri, p1) + _dot(tri, p2) + _dot(tri, p3)
    i_col = jnp.sum(jnp.where(lane == head, gts, 0.0), axis=-1, keepdims=True)
    bcum = jnp.sum(jnp.where(lane == head + MLSTM_HEADS, bc_all, 0.0), axis=-1, keepdims=True)
    a_col = i_col - bcum
    a_row = jnp.transpose(jnp.broadcast_to(a_col, (L, LANES)))[0:1, :]
    m_st = m_ref[0:1, 0:1]

    rows = lax.broadcasted_iota(jnp.int32, (L, L), 0)
    cols = lax.broadcasted_iota(jnp.int32, (L, L), 1)
    d = jnp.where(rows >= cols, bcum + a_row, -jnp.inf)
    g = bcum + m_st
    m_t = jnp.maximum(g, jnp.max(d, axis=-1, keepdims=True))
    w_intra = jnp.exp(d - m_t)
    w_inter = jnp.exp(g - m_t)

    q_raw = q_ref[...]
    k_raw = k_ref[...]
    q = _causal_conv(q_raw, tq_ref[...], cwq_ref[...], cbq_ref[...])
    k = _causal_conv(k_raw, tk_ref[...], cwk_ref[...], cbk_ref[...])
    tq_ref[...] = q_raw[L - 8:L, :]
    tk_ref[...] = k_raw[L - 8:L, :]
    q = q * jax.nn.sigmoid(q)
    k = k * jax.nn.sigmoid(k) * (dh ** -0.5)
    qb = q.astype(BF16)
    kb = k.astype(BF16)
    vb = v_ref[...].astype(BF16)

    s_qk = _dot_nt(qb, kb) * w_intra
    ct = ct_ref[...]
    n_row = n_ref[...]
    num = _dot(s_qk.astype(BF16), vb) + w_inter * _dot(qb, ct.astype(BF16))
    den = jnp.sum(s_qk, axis=-1, keepdims=True) + w_inter * jnp.sum(q * n_row, axis=-1, keepdims=True)
    hcell = num / jnp.maximum(jnp.abs(den), jnp.exp(-m_t))

    b_last = bcum[L - 1:L, :]
    d_end = b_last + a_col
    m_new = jnp.maximum(b_last + m_st, jnp.max(d_end, axis=0, keepdims=True))
    w_end = jnp.exp(d_end - m_new)
    decay = jnp.exp(b_last + m_st - m_new)
    kw = k * w_end
    ct_ref[...] = decay * ct + _dot_tn(kw.astype(BF16), vb)
    n_ref[...] = decay * n_row + jnp.sum(kw, axis=0, keepdims=True)
    m_ref[...] = jnp.broadcast_to(m_new, m_ref.shape)

    hm = jax.nn.sigmoid(o_ref[...]) * hcell
    hm = hm * lax.rsqrt(jnp.mean(hm * hm, axis=-1, keepdims=True) + EPS)
    y_ref[...] = (hm * hg_ref[...]).astype(y_ref.dtype)


def _mlstm(p, gates, gate_bias, bsz, seq, cw, cb, head_g, col0):
    nh = MLSTM_HEADS
    dh = head_g.shape[1] // nh
    L = _tile(seq, 256)
    nc = seq // L
    cq = col0 // dh
    tri = jnp.tril(jnp.ones((L, L), BF16))

    def pblock(off):
        return pl.BlockSpec((L, dh), lambda b, h, c: (b * nc + c, off + h))

    return pl.pallas_call(
        _mlstm_body,
        grid=(bsz, nh, nc),
        in_specs=[pblock(cq), pblock(cq + nh), pblock(cq + 2 * nh), pblock(cq + 3 * nh),
                  pl.BlockSpec((L, LANES), lambda b, h, c: (b * nc + c, 0)),
                  pl.BlockSpec((1, LANES), lambda b, h, c: (0, 0)),
                  pl.BlockSpec((CONV_K, dh), lambda b, h, c: (0, h)),
                  pl.BlockSpec((1, dh), lambda b, h, c: (0, h)),
                  pl.BlockSpec((CONV_K, dh), lambda b, h, c: (0, nh + h)),
                  pl.BlockSpec((1, dh), lambda b, h, c: (0, nh + h)),
                  pl.BlockSpec((1, dh), lambda b, h, c: (0, h)),
                  pl.BlockSpec((L, L), lambda b, h, c: (0, 0))],
        out_specs=pl.BlockSpec((L, dh), lambda b, h, c: (b * nc + c, h)),
        out_shape=jax.ShapeDtypeStruct((bsz * seq, nh * dh), BF16),
        scratch_shapes=[pltpu.VMEM((dh, dh), F32), pltpu.VMEM((1, dh), F32), pltpu.VMEM((8, LANES), F32),
                        pltpu.VMEM((8, dh), F32), pltpu.VMEM((8, dh), F32)],
        compiler_params=_cparams("parallel", "parallel", "arbitrary"),
        name="mlstm",
    )(p, p, p, p, gates, gate_bias, cw, cb, cw, cb, head_g, tri)


def _outproj_body(ya_ref, yb_ref, wa_ref, wb_ref, x_ref, o_ref):
    o_ref[...] = x_ref[...] + _dot(ya_ref[...], wa_ref[...]) + _dot(yb_ref[...], wb_ref[...])


def _out_proj(ya, yb, wa, wb, x2):
    t, d = x2.shape
    ka, kb = ya.shape[1], yb.shape[1]
    tm, tn = _tile(t, 1024), _tile(d, 1024)
    return pl.pallas_call(
        _outproj_body,
        grid=(t // tm, d // tn),
        in_specs=[pl.BlockSpec((tm, ka), lambda i, j: (i, 0)),
                  pl.BlockSpec((tm, kb), lambda i, j: (i, 0)),
                  pl.BlockSpec((ka, tn), lambda i, j: (0, j)),
                  pl.BlockSpec((kb, tn), lambda i, j: (0, j)),
                  pl.BlockSpec((tm, tn), lambda i, j: (i, j))],
        out_specs=pl.BlockSpec((tm, tn), lambda i, j: (i, j)),
        out_shape=jax.ShapeDtypeStruct((t, d), F32),
        compiler_params=_cparams("parallel", "arbitrary"),
        name="out_proj",
    )(ya, yb, wa, wb, x2)


def _s5in_body(x_ref, g_ref, wt_ref, z_ref):
    hb = _rms(x_ref[...], g_ref[...]).astype(BF16)
    res = _dot_nt(wt_ref[...], hb)
    z_ref[...] = res.reshape(z_ref.shape)


def _s5_in(xl, g, wt):
    t, d = xl.shape
    ch = wt.shape[0]
    ngrp = ch // S5_GROUP
    nr = t // S5_L
    r = _tile(nr, 512)
    nri = nr // r
    return pl.pallas_call(
        _s5in_body,
        grid=(nri, S5_L),
        in_specs=[pl.BlockSpec((r, d), lambda i, l: (l * nri + i, 0)),
                  pl.BlockSpec((1, d), lambda i, l: (0, 0)),
                  pl.BlockSpec((ch, d), lambda i, l: (0, 0))],
        out_specs=pl.BlockSpec((ngrp, S5_GROUP, r), lambda i, l: (0, l, i)),
        out_shape=jax.ShapeDtypeStruct((ngrp, S5_L * S5_GROUP, nr), F32),
        compiler_params=_cparams("parallel", "arbitrary"),
        name="s5_in",
    )(xl, g, wt)


def _s5core_body(z_ref, min_ref, mintra_ref, cst_ref, lp_ref, y_ref, *, nseg, nlev):
    p = S5_STATE
    zb = z_ref[0].astype(BF16)
    e = _dot(min_ref[0], zb)
    sr, si = e[:p, :], e[p:, :]
    nr = sr.shape[1]
    lane = lax.broadcasted_iota(jnp.int32, (1, nr), 1) % nseg
    lp = lp_ref[0]
    for j in range(nlev):
        s = 1 << j
        cr = lp[:, j:j + 1]
        ci = lp[:, nlev + j:nlev + j + 1]
        keep = lane >= s
        tr = jnp.where(keep, pltpu.roll(sr, s, axis=1), 0.0)
        ti = jnp.where(keep, pltpu.roll(si, s, axis=1), 0.0)
        sr, si = sr + cr * tr - ci * ti, si + cr * ti + ci * tr
    keep = lane >= 1
    xr = jnp.where(keep, pltpu.roll(sr, 1, axis=1), 0.0)
    xi = jnp.where(keep, pltpu.roll(si, 1, axis=1), 0.0)
    xb = jnp.concatenate([xr, xi], axis=0).astype(BF16)
    y = _dot(mintra_ref[0], zb) + _dot(cst_ref[0], xb)
    y_ref[0] = jax.nn.gelu(y).astype(y_ref.dtype)


def _s5_core(z, m_in, m_intra, c_st, lpow, seq):
    ngrp, kk, nr = z.shape
    nseg = seq // S5_L
    nlev = lpow.shape[2] // 2
    return pl.pallas_call(
        functools.partial(_s5core_body, nseg=nseg, nlev=nlev),
        grid=(ngrp,),
        in_specs=[pl.BlockSpec((1, kk, nr), lambda g: (g, 0, 0)),
                  pl.BlockSpec((1,) + m_in.shape[1:], lambda g: (g, 0, 0)),
                  pl.BlockSpec((1,) + m_intra.shape[1:], lambda g: (g, 0, 0)),
                  pl.BlockSpec((1,) + c_st.shape[1:], lambda g: (g, 0, 0)),
                  pl.BlockSpec((1,) + lpow.shape[1:], lambda g: (g, 0, 0))],
        out_specs=pl.BlockSpec((1, kk, nr), lambda g: (g, 0, 0)),
        out_shape=jax.ShapeDtypeStruct((ngrp, kk, nr), BF16),
        compiler_params=_cparams("parallel"),
        name="s5_core",
    )(z, m_in, m_intra, c_st, lpow)


def _glu_body(y_ref, wv_ref, wg_ref, x_ref, o_ref):
    ngrp, grp, r = y_ref.shape
    y = y_ref[...].reshape(ngrp * grp, r)
    v = _dot(wv_ref[...], y)
    g = _dot(wg_ref[...], y)
    o_ref[...] = x_ref[...] + jnp.transpose(v * jax.nn.sigmoid(g))


def _s5_glu(y, wvt, wgt, xl):
    t, d = xl.shape
    ngrp, _, nr = y.shape
    r = _tile(nr, 512)
    nri = nr // r
    tn = _tile(d, 1024)
    return pl.pallas_call(
        _glu_body,
        grid=(nri, S5_L, d // tn),
        in_specs=[pl.BlockSpec((ngrp, S5_GROUP, r), lambda i, l, j: (0, l, i)),
                  pl.BlockSpec((tn, ngrp * S5_GROUP), lambda i, l, j: (j, 0)),
                  pl.BlockSpec((tn, ngrp * S5_GROUP), lambda i, l, j: (j, 0)),
                  pl.BlockSpec((r, tn), lambda i, l, j: (l * nri + i, j))],
        out_specs=pl.BlockSpec((r, tn), lambda i, l, j: (l * nri + i, j)),
        out_shape=jax.ShapeDtypeStruct((t, d), F32),
        compiler_params=_cparams("parallel", "arbitrary", "arbitrary"),
        name="s5_glu",
    )(y, wvt, wgt, xl)


def _s5_operators(a_re, a_im, log_step, b_re, b_im, c_re, c_im, d_skip, nseg):
    hi = lax.Precision.HIGHEST
    L, p, grp = S5_L, S5_STATE, S5_GROUP
    ngrp = a_re.shape[0]
    dt = jnp.exp(log_step)[:, None]
    mag = jnp.exp(a_re * dt)
    lr, li = mag * jnp.cos(a_im * dt), mag * jnp.sin(a_im * dt)
    den = a_re * a_re + a_im * a_im
    fr = ((lr - 1.0) * a_re + li * a_im) / den
    fi = (li * a_re - (lr - 1.0) * a_im) / den
    bbr = fr[..., None] * b_re - fi[..., None] * b_im
    bbi = fr[..., None] * b_im + fi[..., None] * b_re
    pr, pi = [jnp.ones_like(lr)], [jnp.zeros_like(li)]
    for _ in range(L):
        pr_n = pr[-1] * lr - pi[-1] * li
        pi_n = pr[-1] * li + pi[-1] * lr
        pr.append(pr_n)
        pi.append(pi_n)
    pwr, pwi = jnp.stack(pr, 1), jnp.stack(pi, 1)
    lbr = pwr[:, :L, :, None] * bbr[:, None] - pwi[:, :L, :, None] * bbi[:, None]
    lbi = pwr[:, :L, :, None] * bbi[:, None] + pwi[:, :L, :, None] * bbr[:, None]
    m_in = jnp.concatenate([jnp.transpose(lbr[:, ::-1], (0, 2, 1, 3)).reshape(ngrp, p, L * grp),
                            jnp.transpose(lbi[:, ::-1], (0, 2, 1, 3)).reshape(ngrp, p, L * grp)], axis=1)
    taps = (jnp.einsum('gcp,gkpd->gkcd', c_re, lbr, precision=hi)
            - jnp.einsum('gcp,gkpd->gkcd', c_im, lbi, precision=hi))
    taps = taps.at[:, 0].add(jax.vmap(jnp.diag)(d_skip.reshape(ngrp, grp)))
    lag = jnp.arange(L)[:, None] - jnp.arange(L)[None, :]
    blocks = jnp.where((lag >= 0)[None, :, :, None, None], taps[:, jnp.clip(lag, 0, L - 1)], 0.0)
    m_intra = jnp.transpose(blocks, (0, 1, 3, 2, 4)).reshape(ngrp, L * grp, L * grp)
    qr, qi = pwr[:, 1:], pwi[:, 1:]
    cs_r = c_re[:, None] * qr[:, :, None, :] - c_im[:, None] * qi[:, :, None, :]
    cs_i = -c_re[:, None] * qi[:, :, None, :] - c_im[:, None] * qr[:, :, None, :]
    c_st = jnp.concatenate([cs_r, cs_i], axis=-1).reshape(ngrp, L * grp, 2 * p)
    nlev = max(1, int(math.log2(nseg)))
    sr_, si_ = [pwr[:, L]], [pwi[:, L]]
    for _ in range(nlev - 1):
        sr_.append(sr_[-1] * sr_[-1] - si_[-1] * si_[-1])
        si_.append(2.0 * sr_[-2] * si_[-1])
    lpow = jnp.stack(sr_ + si_, axis=-1)
    return m_in.astype(BF16), m_intra.astype(BF16), c_st.astype(BF16), lpow


def _router_body(x_ref, g_ref, whi_ref, wlo_ref, b_ref, tri_ref, o_ref, cnt_ref, carry_ref):
    @pl.when(pl.program_id(0) == 0)
    def _():
        carry_ref[...] = jnp.zeros_like(carry_ref)

    h = _rms(x_ref[...], g_ref[...])
    h_hi = h.astype(BF16)
    h_lo = (h - h_hi.astype(F32)).astype(BF16)
    whi = whi_ref[...]
    logits = _dot(h_hi, whi) + _dot(h_lo, whi) + _dot(h_hi, wlo_ref[...]) + b_ref[...]
    lane = lax.broadcasted_iota(jnp.int32, logits.shape, 1).astype(F32)
    neg, big = -jnp.inf, 1e9

    def first_argmax(vals):
        mx = jnp.max(vals, axis=-1, keepdims=True)
        return mx, jnp.min(jnp.where(vals == mx, lane, big), axis=-1, keepdims=True)

    is_c = lane < MOE_GROUPS
    mx, gi = first_argmax(jnp.where(is_c, logits, neg))
    p_g = 1.0 / jnp.sum(jnp.where(is_c, jnp.exp(logits - mx), 0.0), axis=-1, keepdims=True)
    base = MOE_GROUPS + MOE_PER_GROUP * gi
    lf = jnp.where((lane >= base) & (lane < base + MOE_PER_GROUP), logits, neg)
    v1, i1 = first_argmax(lf)
    v2, i2 = first_argmax(jnp.where(lane == i1, neg, lf))
    e = jnp.exp(v2 - v1)
    w1 = p_g / (1.0 + e)
    w2 = p_g * e / (1.0 + e)
    k1, k2 = i1 - base, i2 - base
    first_low = k1 < k2
    lo, hi = jnp.minimum(k1, k2), jnp.maximum(k1, k2)
    w_lo, w_hi = jnp.where(first_low, w1, w2), jnp.where(first_low, w2, w1)
    cls = gi * N_PAIRS + lo * (7.0 - lo) * 0.5 + (hi - lo - 1.0)
    onehot = jnp.where(lane == cls, 1.0, 0.0)
    before = _dot(tri_ref[...], onehot.astype(BF16))
    carry = carry_ref[...]
    rank = jnp.sum(onehot * (before + carry), axis=-1, keepdims=True)
    carry = carry + jnp.sum(onehot, axis=0, keepdims=True)
    carry_ref[...] = carry
    cnt_ref[...] = carry
    out = jnp.where(lane == 0, cls, jnp.where(lane == 1, rank, jnp.where(lane == 2, w_lo,
                    jnp.where(lane == 3, w_hi, 0.0))))
    o_ref[...] = jnp.transpose(out)[0:8, :]


def _router(x2, g, w_hi, w_lo, bias):
    t, d = x2.shape
    tm = _tile(t, 512)
    tri = jnp.tril(jnp.ones((tm, tm), BF16), -1)
    return pl.pallas_call(
        _router_body,
        grid=(t // tm,),
        in_specs=[pl.BlockSpec((tm, d), lambda i: (i, 0)),
                  pl.BlockSpec((1, d), lambda i: (0, 0)),
                  pl.BlockSpec((d, LANES), lambda i: (0, 0)),
                  pl.BlockSpec((d, LANES), lambda i: (0, 0)),
                  pl.BlockSpec((1, LANES), lambda i: (0, 0)),
                  pl.BlockSpec((tm, tm), lambda i: (0, 0))],
        out_specs=[pl.BlockSpec((8, tm), lambda i: (0, i)),
                   pl.BlockSpec((1, LANES), lambda i: (0, 0))],
        out_shape=[jax.ShapeDtypeStruct((8, t), F32), jax.ShapeDtypeStruct((1, LANES), F32)],
        scratch_shapes=[pltpu.VMEM((1, LANES), F32)],
        compiler_params=_cparams("arbitrary"),
        name="router",
    )(x2, g, w_hi, w_lo, bias, tri)


def _moe_body(src_ref, nv_ref, elo_ref, ehi_ref,
              x_hbm, rw_ref, g_ref, gfin_ref, wg_lo, wu_lo, wd_lo, wg_hi, wu_hi, wd_hi,
              o_hbm, xbuf, obuf, sem_in, sem_out, *, final, out_order, nr):
    i = pl.program_id(0)
    nt = pl.num_programs(0)
    tm = xbuf.shape[1]
    slot = i % 2
    nv = nv_ref[i]

    def divmod_nonneg(v, m):
        if m & (m - 1) == 0:
            sh = m.bit_length() - 1
            return lax.shift_right_logical(v, sh), v & (m - 1)
        return v // m, v % m

    def out_row(row):
        if out_order == "fold16":
            q, r = divmod_nonneg(row, S5_L)
            return r * nr + q
        if out_order == "unfold16":
            q, r = divmod_nonneg(row, nr)
            return r * S5_L + q
        return row

    def gather_row(tile, s, r):
        row = src_ref[tile * tm + r]
        return pltpu.make_async_copy(x_hbm.at[pl.ds(row, 1), :], xbuf.at[s, pl.ds(r, 1), :], sem_in.at[s])

    def issue_gather(tile, s):
        def body(r, c):
            gather_row(tile, s, r).start()
            return c

        lax.fori_loop(0, tm, body, 0, unroll=8)

    def scatter_row(tile, s, r):
        row = out_row(src_ref[tile * tm + r])
        return pltpu.make_async_copy(obuf.at[s, pl.ds(r, 1), :], o_hbm.at[pl.ds(row, 1), :], sem_out.at[s])

    def issue_scatter(tile, s, n):
        def block(b, c):
            for u in range(8):
                scatter_row(tile, s, b * 8 + u).start()
            return c

        lax.fori_loop(0, n // 8, block, 0)

        def single(r, c):
            scatter_row(tile, s, r).start()
            return c

        lax.fori_loop((n // 8) * 8, n, single, 0)

    def wait_scatter(s, n):
        p = 1
        while p <= tm:
            @pl.when((n & p) != 0)
            def _(p=p):
                pltpu.make_async_copy(obuf.at[s, pl.ds(0, p), :], o_hbm.at[pl.ds(0, p), :], sem_out.at[s]).wait()
            p *= 2

    @pl.when((i == 0) & (nv > 0))
    def _():
        issue_gather(0, 0)

    nxt = jnp.minimum(i + 1, nt - 1)

    @pl.when((i + 1 < nt) & (nv_ref[nxt] > 0))
    def _():
        issue_gather(i + 1, 1 - slot)

    @pl.when(nv > 0)
    def _():
        pltpu.make_async_copy(x_hbm.at[pl.ds(0, tm), :], xbuf.at[slot], sem_in.at[slot]).wait()
        x = xbuf[slot]
        hb = _rms(x, g_ref[...]).astype(BF16)

        def expert(wg, wu, wd):
            a = _dot(hb, wg[0])
            u = _dot(hb, wu[0])
            he = (a * jax.nn.sigmoid(a) * u).astype(BF16)
            return _dot(he, wd[0])

        rw = rw_ref[...]
        eye = lax.broadcasted_iota(jnp.int32, (tm, tm), 0) == lax.broadcasted_iota(jnp.int32, (tm, tm), 1)
        w_lo = jnp.sum(jnp.where(eye, rw[2:3, :], 0.0), axis=1, keepdims=True)
        w_hi = jnp.sum(jnp.where(eye, rw[3:4, :], 0.0), axis=1, keepdims=True)
        out = x + w_lo * expert(wg_lo, wu_lo, wd_lo) + w_hi * expert(wg_hi, wu_hi, wd_hi)
        if final:
            out = _rms(out, gfin_ref[...])
        obuf[slot] = out
        issue_scatter(i, slot, nv)

    prev = jnp.maximum(i - 1, 0)

    @pl.when((i > 0) & (nv_ref[prev] > 0))
    def _():
        wait_scatter(1 - slot, nv_ref[prev])

    @pl.when((i == nt - 1) & (nv > 0))
    def _():
        wait_scatter(slot, nv)


def _moe(x2, rws, src, nvalid, elo, ehi, g, gfin, wg, wu, wd, tm, final, out_order):
    t, d = x2.shape
    nt = nvalid.shape[0]
    ff = wg.shape[2]
    cm = lambda i, *_: (0, 0)
    lo3 = lambda i, src, nv, elo, ehi: (elo[i], 0, 0)
    hi3 = lambda i, src, nv, elo, ehi: (ehi[i], 0, 0)
    return pl.pallas_call(
        functools.partial(_moe_body, final=final, out_order=out_order, nr=t // S5_L),
        grid_spec=pltpu.PrefetchScalarGridSpec(
            num_scalar_prefetch=4,
            grid=(nt,),
            in_specs=[pl.BlockSpec(memory_space=pl.ANY),
                      pl.BlockSpec((8, tm), lambda i, *_: (0, i)),
                      pl.BlockSpec((1, d), cm),
                      pl.BlockSpec((1, d), cm),
                      pl.BlockSpec((1, d, ff), lo3), pl.BlockSpec((1, d, ff), lo3), pl.BlockSpec((1, ff, d), lo3),
                      pl.BlockSpec((1, d, ff), hi3), pl.BlockSpec((1, d, ff), hi3), pl.BlockSpec((1, ff, d), hi3)],
            out_specs=pl.BlockSpec(memory_space=pl.ANY),
            scratch_shapes=[pltpu.VMEM((2, tm, d), F32), pltpu.VMEM((2, tm, d), F32),
                            pltpu.SemaphoreType.DMA((2,)), pltpu.SemaphoreType.DMA((2,))],
        ),
        out_shape=jax.ShapeDtypeStruct((t, d), F32),
        compiler_params=_cparams("arbitrary"),
        name="moe_final" if final else "moe",
    )(src, nvalid, elo, ehi, x2, rws, g, gfin, wg, wu, wd, wg, wu, wd)


def _moe_layer(x2, g, gfin, w_coarse, b_coarse, w_fine, b_fine, wg, wu, wd, final, out_order):
    t, d = x2.shape
    wr = jnp.zeros((d, LANES), F32).at[:, :MOE_GROUPS].set(w_coarse)
    wr = wr.at[:, MOE_GROUPS:MOE_GROUPS + w_fine.shape[1]].set(w_fine)
    wr_hi = wr.astype(BF16)
    wr_lo = (wr - wr_hi.astype(F32)).astype(BF16)
    bias = jnp.zeros((1, LANES), F32).at[0, :MOE_GROUPS].set(b_coarse)
    bias = bias.at[0, MOE_GROUPS:MOE_GROUPS + b_fine.shape[0]].set(b_fine)
    rw, cnt = _router(x2, g, wr_hi, wr_lo, bias)

    tm = _tile(t, 256)
    nt = t // tm + N_CLASSES
    i32 = jnp.int32
    cls = rw[0].astype(i32)
    rank = rw[1].astype(i32)
    counts = cnt[0, :N_CLASSES].astype(i32)
    tiles_per = (counts + tm - 1) // tm
    tile_end = jnp.cumsum(tiles_per)
    tile_start = tile_end - tiles_per
    class_ids = jnp.arange(N_CLASSES, dtype=i32)
    dest = jnp.sum(jnp.where(cls[:, None] == class_ids[None, :], (tile_start * tm)[None, :], 0), axis=1) + rank
    src = jnp.zeros((nt * tm,), i32).at[dest].set(jnp.arange(t, dtype=i32))
    tile_ids = jnp.arange(nt, dtype=i32)
    total = tile_end[-1]
    used = tile_ids < total
    tcls = jnp.sum((jnp.minimum(tile_ids, total - 1)[:, None] >= tile_end[None, :]).astype(i32), axis=1)
    tcls = jnp.clip(tcls, 0, N_CLASSES - 1)
    sel = tcls[:, None] == class_ids[None, :]
    tcount = jnp.sum(jnp.where(sel, counts[None, :], 0), axis=1)
    tstart = jnp.sum(jnp.where(sel, tile_start[None, :], 0), axis=1)
    nvalid = jnp.where(used, jnp.clip(tcount - (tile_ids - tstart) * tm, 0, tm), 0).astype(i32)
    grp, pair = tcls // N_PAIRS, tcls % N_PAIRS
    pair_ids = jnp.arange(N_PAIRS, dtype=i32)
    psel = pair[:, None] == pair_ids[None, :]
    elo = (grp * MOE_PER_GROUP + jnp.sum(jnp.where(psel, jnp.array(PAIR_LO, i32)[None, :], 0), axis=1)).astype(i32)
    ehi = (grp * MOE_PER_GROUP + jnp.sum(jnp.where(psel, jnp.array(PAIR_HI, i32)[None, :], 0), axis=1)).astype(i32)
    rws = jnp.take(rw, src, axis=1)
    return _moe(x2, rws, src, nvalid, elo, ehi, g, gfin, wg, wu, wd, tm, final, out_order)


def _mixer0(x2, bsz, seq, g, w_in, lru_conv_w, lru_conv_b, lru_w_a, lru_b_a, lru_w_x, lru_b_x, lru_lam,
            m_conv_w, m_conv_b, m_i_bias, m_f_bias, m_head_g, w_out):
    d = x2.shape[1]
    lru_w = lru_conv_w.shape[1]
    mw = m_head_g.shape[0]
    nmain = 2 * lru_w + 4 * mw
    w_main = w_in[:, :nmain].astype(BF16)
    w_gate = jnp.zeros((d, LANES), F32).at[:, :2 * MLSTM_HEADS].set(w_in[:, nmain:]).astype(BF16)
    p, gates = _in_proj0(x2, g[None], w_main, w_gate)
    ya = _rg_lru(p, bsz, seq, lru_conv_w, lru_conv_b[None], lru_w_a.astype(BF16), lru_b_a[None],
                 lru_w_x.astype(BF16), lru_b_x[None], lru_lam[None])
    gate_bias = jnp.zeros((1, LANES), F32).at[0, :MLSTM_HEADS].set(m_i_bias)
    gate_bias = gate_bias.at[0, MLSTM_HEADS:2 * MLSTM_HEADS].set(m_f_bias)
    yb = _mlstm(p, gates, gate_bias, bsz, seq, m_conv_w, m_conv_b[None], m_head_g[None], 2 * lru_w)
    w_out16 = w_out.astype(BF16)
    return _out_proj(ya, yb, w_out16[:lru_w], w_out16[lru_w:], x2)


def _mixer1(x2, seq, g, w_in, a_re, a_im, log_step, b_re, b_im, c_re, c_im, d_skip, w_glu_v, w_glu_g):
    ops = _s5_operators(a_re, a_im, log_step, b_re, b_im, c_re, c_im, d_skip, seq // S5_L)
    z = _s5_in(x2, g[None], jnp.transpose(w_in).astype(BF16))
    y = _s5_core(z, *ops, seq)
    return _s5_glu(y, jnp.transpose(w_glu_v).astype(BF16), jnp.transpose(w_glu_g).astype(BF16), x2)


def kernel(x, norm_mix, norm_ffn, norm_final, ab_w_in, lru_conv_w, lru_conv_b, lru_w_a, lru_b_a, lru_w_x, lru_b_x,
           lru_lam, m_conv_w, m_conv_b, m_i_bias, m_f_bias, m_head_g, ab_w_out, s5_w_in, s5_a_re, s5_a_im,
           s5_log_step, s5_b_re, s5_b_im, s5_c_re, s5_c_im, s5_d, s5_w_glu_v, s5_w_glu_g, moe_w_coarse,
           moe_b_coarse, moe_w_fine, moe_b_fine, moe_w_gate, moe_w_up, moe_w_down):
    bsz, seq, d = x.shape
    depth = norm_mix.shape[0]
    x2 = x.reshape(bsz * seq, d)
    gfin = norm_final[None]
    for layer in range(depth):
        j = layer // 2
        last = layer == depth - 1
        if layer % 2 == 0:
            x2 = _mixer0(x2, bsz, seq, norm_mix[layer], ab_w_in[j], lru_conv_w[j], lru_conv_b[j], lru_w_a[j],
                         lru_b_a[j], lru_w_x[j], lru_b_x[j], lru_lam[j], m_conv_w[j], m_conv_b[j], m_i_bias[j],
                         m_f_bias[j], m_head_g[j], ab_w_out[j])
            out_order = "same" if last else "fold16"
        else:
            x2 = _mixer1(x2, seq, norm_mix[layer], s5_w_in[j], s5_a_re[j], s5_a_im[j], s5_log_step[j], s5_b_re[j],
                         s5_b_im[j], s5_c_re[j], s5_c_im[j], s5_d[j], s5_w_glu_v[j], s5_w_glu_g[j])
            out_order = "unfold16"
        x2 = _moe_layer(x2, norm_ffn[layer][None], gfin, moe_w_coarse[layer], moe_b_coarse[layer],
                        moe_w_fine[layer], moe_b_fine[layer], moe_w_gate[layer].astype(BF16),
                        moe_w_up[layer].astype(BF16), moe_w_down[layer].astype(BF16), last, out_order)
    return x2.reshape(bsz, seq, d)
```

```python
import functools
import math

import jax
import jax.numpy as jnp
from jax import lax
from jax.experimental import pallas as pl
from jax.experimental.pallas import tpu as pltpu

F32 = jnp.float32
BF16 = jnp.bfloat16
EPS = 1e-6
LANES = 128
VMEM_LIMIT = 56 * 1024 * 1024

CONV_K = 4
LRU_BLOCKS = 8
LRU_C = 8.0
MLSTM_HEADS = 4
S5_GROUP = 16
S5_STATE = 64
S5_L = 16
MOE_GROUPS = 4
MOE_PER_GROUP = 4
N_PAIRS = 6
N_CLASSES = MOE_GROUPS * N_PAIRS
PAIR_LO = (0, 0, 0, 1, 1, 2)
PAIR_HI = (1, 2, 3, 2, 3, 3)


def _tile(n, pref):
    return pref if n % pref == 0 else n


def _cparams(*sem):
    return pltpu.CompilerParams(dimension_semantics=sem, vmem_limit_bytes=VMEM_LIMIT)


def _rms(x, g):
    return x * lax.rsqrt(jnp.mean(x * x, axis=-1, keepdims=True) + EPS) * g


def _dot(a, b):
    return jnp.dot(a, b, preferred_element_type=F32)


def _dot_nt(a, b):
    return lax.dot_general(a, b, (((1,), (1,)), ((), ())), preferred_element_type=F32)


def _dot_tn(a, b):
    return lax.dot_general(a, b, (((0,), (0,)), ((), ())), preferred_element_type=F32)


def _softplus(y):
    return jnp.maximum(y, 0.0) + jnp.log1p(jnp.exp(-jnp.abs(y)))


def _split3(x):
    p1 = x.astype(BF16)
    r1 = x - p1.astype(F32)
    p2 = r1.astype(BF16)
    p3 = (r1 - p2.astype(F32)).astype(BF16)
    return p1, p2, p3


def _causal_conv(x, tail, w, b):
    n = x.shape[0]
    xp = jnp.concatenate([tail, x], axis=0)
    out = b + w[CONV_K - 1:CONV_K, :] * x
    for j in range(CONV_K - 1):
        off = 8 - (CONV_K - 1) + j
        out = out + w[j:j + 1, :] * xp[off:off + n, :]
    return out


def _in0_body(x_ref, g_ref, w_ref, wg_ref, cw_ref, cb_ref, pc_ref, pb_ref, gate_ref, tail_ref, *, seq_tiles, dh):
    tm, c = pc_ref.shape

    @pl.when(pl.program_id(0) == 0)
    def _():
        tail_ref[...] = jnp.zeros_like(tail_ref)

    hb = _rms(x_ref[...], g_ref[...]).astype(BF16)
    gate_ref[...] = _dot(hb, wg_ref[...])
    seq_start = (pl.program_id(0) % seq_tiles) == 0

    def proj(jt):
        return _dot(hb, w_ref[:, jt * c:(jt + 1) * c])

    def conv(k, raw):
        tail = jnp.where(seq_start, 0.0, tail_ref[k])
        tail_ref[k] = raw[tm - 8:tm, :]
        return _causal_conv(raw, tail, cw_ref[:, k * c:(k + 1) * c], cb_ref[:, k * c:(k + 1) * c])

    pc_ref[...] = conv(0, proj(0))
    pb_ref[:, 0:c] = jax.nn.gelu(proj(1)).astype(BF16)
    q = conv(1, proj(2))
    pb_ref[:, c:2 * c] = (q * jax.nn.sigmoid(q)).astype(BF16)
    k = conv(2, proj(3))
    pb_ref[:, 2 * c:3 * c] = (k * jax.nn.sigmoid(k) * (dh ** -0.5)).astype(BF16)
    pb_ref[:, 3 * c:4 * c] = proj(4).astype(BF16)
    pb_ref[:, 4 * c:5 * c] = jax.nn.sigmoid(proj(5)).astype(BF16)


def _in_proj0(x2, g, w, w_gate, conv_w, conv_b, seq, c, dh):
    t, d = x2.shape
    tm = _tile(seq, 512)
    assert seq % tm == 0
    return pl.pallas_call(
        functools.partial(_in0_body, seq_tiles=seq // tm, dh=dh),
        grid=(t // tm,),
        in_specs=[pl.BlockSpec((tm, d), lambda i: (i, 0)),
                  pl.BlockSpec((1, d), lambda i: (0, 0)),
                  pl.BlockSpec(w.shape, lambda i: (0, 0), pipeline_mode=pl.Buffered(1)),
                  pl.BlockSpec((d, LANES), lambda i: (0, 0)),
                  pl.BlockSpec(conv_w.shape, lambda i: (0, 0)),
                  pl.BlockSpec(conv_b.shape, lambda i: (0, 0))],
        out_specs=[pl.BlockSpec((tm, c), lambda i: (i, 0)),
                   pl.BlockSpec((tm, 5 * c), lambda i: (i, 0)),
                   pl.BlockSpec((tm, LANES), lambda i: (i, 0))],
        out_shape=[jax.ShapeDtypeStruct((t, c), F32), jax.ShapeDtypeStruct((t, 5 * c), BF16),
                   jax.ShapeDtypeStruct((t, LANES), F32)],
        scratch_shapes=[pltpu.VMEM((3, 8, c), F32)],
        compiler_params=_cparams("arbitrary"),
        name="in_proj0",
    )(x2, g, w, w_gate, conv_w, conv_b)


def _lru_body(x_ref, gz_ref, wa_ref, ba_ref, wx_ref, bx_ref, lam_ref, y_ref, h_ref, a_s, g_s):
    ts, c = x_ref.shape
    blk = c // LRU_BLOCKS

    @pl.when(pl.program_id(1) == 0)
    def _():
        h_ref[...] = jnp.zeros_like(h_ref)

    conv = x_ref[...]
    cb16 = conv.astype(BF16)
    ra = jnp.concatenate([_dot(cb16[:, n * blk:(n + 1) * blk], wa_ref[n]) for n in range(LRU_BLOCKS)], axis=1)
    rx = jnp.concatenate([_dot(cb16[:, n * blk:(n + 1) * blk], wx_ref[n]) for n in range(LRU_BLOCKS)], axis=1)
    r = jax.nn.sigmoid(ra + ba_ref[...])
    ig = jax.nn.sigmoid(rx + bx_ref[...])
    log_a = -LRU_C * r * _softplus(-lam_ref[...])
    a = jnp.exp(log_a)
    pa = a
    pg = jnp.sqrt(1.0 - a * a) * (ig * conv)
    row8 = lax.broadcasted_iota(jnp.int32, (ts, c), 0) & 7
    for s in (1, 2, 4):
        keep = row8 >= s
        ga = jnp.where(keep, pltpu.roll(pg, s, axis=0), 0.0)
        aa = jnp.where(keep, pltpu.roll(pa, s, axis=0), 1.0)
        pg = pa * ga + pg
        pa = pa * aa
    a_s[...] = pa
    g_s[...] = pg

    def group(k, h):
        r0 = pl.multiple_of(k * 8, 8)
        h8 = a_s[pl.ds(r0, 8), :] * h + g_s[pl.ds(r0, 8), :]
        g_s[pl.ds(r0, 8), :] = h8
        return h8[7:8, :]

    h_ref[...] = lax.fori_loop(0, ts // 8, group, h_ref[...], unroll=4)
    y_ref[...] = (g_s[...] * gz_ref[...].astype(F32)).astype(y_ref.dtype)


def _rg_lru(pc, pb, bsz, seq, wa, ba, wx, bx, lam):
    c = pc.shape[1]
    ts = _tile(seq, 512)
    ns = seq // ts
    row = lambda b, s: (0, 0)
    return pl.pallas_call(
        _lru_body,
        grid=(bsz, ns),
        in_specs=[pl.BlockSpec((ts, c), lambda b, s: (b * ns + s, 0)),
                  pl.BlockSpec((ts, c), lambda b, s: (b * ns + s, 0)),
                  pl.BlockSpec(wa.shape, lambda b, s: (0, 0, 0)),
                  pl.BlockSpec((1, c), row),
                  pl.BlockSpec(wx.shape, lambda b, s: (0, 0, 0)),
                  pl.BlockSpec((1, c), row),
                  pl.BlockSpec((1, c), row)],
        out_specs=pl.BlockSpec((ts, c), lambda b, s: (b * ns + s, 0)),
        out_shape=jax.ShapeDtypeStruct((bsz * seq, c), BF16),
        scratch_shapes=[pltpu.VMEM((1, c), F32), pltpu.VMEM((ts, c), F32), pltpu.VMEM((ts, c), F32)],
        compiler_params=_cparams("parallel", "arbitrary"),
        name="rg_lru",
    )(pc, pb, wa, ba, wx, bx, lam)


def _mlstm_body(q_ref, k_ref, v_ref, so_ref, gt_ref, gb_ref, hg_ref, tri_ref, y_ref, ct_ref, n_ref, m_ref):
    L = q_ref.shape[0]
    nh = ct_ref.shape[0]
    dh = ct_ref.shape[1]

    @pl.when(pl.program_id(1) == 0)
    def _():
        ct_ref[...] = jnp.zeros_like(ct_ref)
        n_ref[...] = jnp.zeros_like(n_ref)
        m_ref[...] = jnp.zeros_like(m_ref)

    gts = gt_ref[...] + gb_ref[...]
    lane = lax.broadcasted_iota(jnp.int32, gts.shape, 1)
    lsig = jnp.minimum(gts, 0.0) - jnp.log1p(jnp.exp(-jnp.abs(gts)))
    tri = tri_ref[...]
    p1, p2, p3 = _split3(lsig)
    bc_all = _dot(tri, p1) + _dot(tri, p2) + _dot(tri, p3)
    rows = lax.broadcasted_iota(jnp.int32, (L, L), 0)
    cols = lax.broadcasted_iota(jnp.int32, (L, L), 1)
    causal = rows >= cols

    for head in range(nh):
        hs = slice(head * dh, (head + 1) * dh)
        i_col = jnp.sum(jnp.where(lane == head, gts, 0.0), axis=-1, keepdims=True)
        bcum = jnp.sum(jnp.where(lane == head + nh, bc_all, 0.0), axis=-1, keepdims=True)
        a_col = i_col - bcum
        a_row = jnp.transpose(jnp.broadcast_to(a_col, (L, LANES)))[0:1, :]
        m_st = m_ref[head, 0:1, 0:1]

        d = jnp.where(causal, bcum + a_row, -jnp.inf)
        g = bcum + m_st
        m_t = jnp.maximum(g, jnp.max(d, axis=-1, keepdims=True))
        w_intra = jnp.exp(d - m_t)
        w_inter = jnp.exp(g - m_t)

        qb = q_ref[:, hs]
        kb = k_ref[:, hs]
        vb = v_ref[:, hs]
        s_qk = _dot_nt(qb, kb) * w_intra
        ct = ct_ref[head]
        n_row = n_ref[head]
        num = _dot(s_qk.astype(BF16), vb) + w_inter * _dot(qb, ct.astype(BF16))
        den = (jnp.sum(s_qk, axis=-1, keepdims=True)
               + w_inter * jnp.sum(qb.astype(F32) * n_row, axis=-1, keepdims=True))
        hcell = num / jnp.maximum(jnp.abs(den), jnp.exp(-m_t))

        b_last = bcum[L - 1:L, :]
        d_end = b_last + a_col
        m_new = jnp.maximum(b_last + m_st, jnp.max(d_end, axis=0, keepdims=True))
        w_end = jnp.exp(d_end - m_new)
        decay = jnp.exp(b_last + m_st - m_new)
        kw = kb.astype(F32) * w_end
        ct_ref[head] = decay * ct + _dot_tn(kw.astype(BF16), vb)
        n_ref[head] = decay * n_row + jnp.sum(kw, axis=0, keepdims=True)
        m_ref[head] = jnp.broadcast_to(m_new, m_ref.shape[1:])

        hm = so_ref[:, hs].astype(F32) * hcell
        hm = hm * lax.rsqrt(jnp.mean(hm * hm, axis=-1, keepdims=True) + EPS)
        y_ref[:, hs] = (hm * hg_ref[:, hs]).astype(y_ref.dtype)


def _mlstm(pb, gates, gate_bias, bsz, seq, head_g, col0):
    nh = MLSTM_HEADS
    w = head_g.shape[1]
    dh = w // nh
    L = _tile(seq, 256)
    nc = seq // L
    cq = col0 // w
    tri = jnp.tril(jnp.ones((L, L), BF16))

    def pblock(off):
        return pl.BlockSpec((L, w), lambda b, c: (b * nc + c, off))

    return pl.pallas_call(
        _mlstm_body,
        grid=(bsz, nc),
        in_specs=[pblock(cq), pblock(cq + 1), pblock(cq + 2), pblock(cq + 3),
                  pl.BlockSpec((L, LANES), lambda b, c: (b * nc + c, 0)),
                  pl.BlockSpec((1, LANES), lambda b, c: (0, 0)),
                  pl.BlockSpec((1, w), lambda b, c: (0, 0)),
                  pl.BlockSpec((L, L), lambda b, c: (0, 0))],
        out_specs=pl.BlockSpec((L, w), lambda b, c: (b * nc + c, 0)),
        out_shape=jax.ShapeDtypeStruct((bsz * seq, w), BF16),
        scratch_shapes=[pltpu.VMEM((nh, dh, dh), F32), pltpu.VMEM((nh, 1, dh), F32),
                        pltpu.VMEM((nh, 8, LANES), F32)],
        compiler_params=_cparams("parallel", "arbitrary"),
        name="mlstm",
    )(pb, pb, pb, pb, gates, gate_bias, head_g, tri)


def _outproj_body(ya_ref, yb_ref, wa_ref, wb_ref, x_ref, o_ref):
    o_ref[...] = x_ref[...] + _dot(ya_ref[...], wa_ref[...]) + _dot(yb_ref[...], wb_ref[...])


def _out_proj(ya, yb, w, x2):
    t, d = x2.shape
    kh = ya.shape[1]
    assert yb.shape[1] == kh and w.shape[0] == 2 * kh
    tm, tn = _tile(t, 1024), _tile(d, 1024)
    return pl.pallas_call(
        _outproj_body,
        grid=(t // tm, d // tn),
        in_specs=[pl.BlockSpec((tm, kh), lambda i, j: (i, 0)),
                  pl.BlockSpec((tm, kh), lambda i, j: (i, 0)),
                  pl.BlockSpec((kh, tn), lambda i, j: (0, j)),
                  pl.BlockSpec((kh, tn), lambda i, j: (1, j)),
                  pl.BlockSpec((tm, tn), lambda i, j: (i, j))],
        out_specs=pl.BlockSpec((tm, tn), lambda i, j: (i, j)),
        out_shape=jax.ShapeDtypeStruct((t, d), F32),
        compiler_params=_cparams("parallel", "arbitrary"),
        name="out_proj",
    )(ya, yb, w, w, x2)


def _s5in_body(x_ref, g_ref, wt_ref, z_ref):
    hb = _rms(x_ref[...], g_ref[...]).astype(BF16)
    res = _dot_nt(wt_ref[...], hb)
    z_ref[...] = res.reshape(z_ref.shape)


def _s5_in(xl, g, wt):
    t, d = xl.shape
    ch = wt.shape[0]
    ngrp = ch // S5_GROUP
    nr = t // S5_L
    r = _tile(nr, 512)
    nri = nr // r
    return pl.pallas_call(
        _s5in_body,
        grid=(nri, S5_L),
        in_specs=[pl.BlockSpec((r, d), lambda i, l: (l * nri + i, 0)),
                  pl.BlockSpec((1, d), lambda i, l: (0, 0)),
                  pl.BlockSpec((ch, d), lambda i, l: (0, 0))],
        out_specs=pl.BlockSpec((ngrp, S5_GROUP, r), lambda i, l: (0, l, i)),
        out_shape=jax.ShapeDtypeStruct((ngrp, S5_L * S5_GROUP, nr), F32),
        compiler_params=_cparams("parallel", "arbitrary"),
        name="s5_in",
    )(xl, g, wt)


def _s5core_body(z_ref, min_ref, mintra_ref, cst_ref, lp_ref, y_ref, *, nseg, nlev):
    p = S5_STATE
    zb = z_ref[0].astype(BF16)
    e = _dot(min_ref[0], zb)
    sr, si = e[:p, :], e[p:, :]
    nr = sr.shape[1]
    lane = lax.broadcasted_iota(jnp.int32, (1, nr), 1) % nseg
    lp = lp_ref[0]
    for j in range(nlev):
        s = 1 << j
        cr = lp[:, j:j + 1]
        ci = lp[:, nlev + j:nlev + j + 1]
        keep = lane >= s
        tr = jnp.where(keep, pltpu.roll(sr, s, axis=1), 0.0)
        ti = jnp.where(keep, pltpu.roll(si, s, axis=1), 0.0)
        sr, si = sr + cr * tr - ci * ti, si + cr * ti + ci * tr
    keep = lane >= 1
    xr = jnp.where(keep, pltpu.roll(sr, 1, axis=1), 0.0)
    xi = jnp.where(keep, pltpu.roll(si, 1, axis=1), 0.0)
    xb = jnp.concatenate([xr, xi], axis=0).astype(BF16)
    y = _dot(mintra_ref[0], zb) + _dot(cst_ref[0], xb)
    y_ref[0] = jax.nn.gelu(y).astype(y_ref.dtype)


def _s5_core(z, m_in, m_intra, c_st, lpow, seq):
    ngrp, kk, nr = z.shape
    nseg = seq // S5_L
    nlev = lpow.shape[2] // 2
    return pl.pallas_call(
        functools.partial(_s5core_body, nseg=nseg, nlev=nlev),
        grid=(ngrp,),
        in_specs=[pl.BlockSpec((1, kk, nr), lambda g: (g, 0, 0)),
                  pl.BlockSpec((1,) + m_in.shape[1:], lambda g: (g, 0, 0)),
                  pl.BlockSpec((1,) + m_intra.shape[1:], lambda g: (g, 0, 0)),
                  pl.BlockSpec((1,) + c_st.shape[1:], lambda g: (g, 0, 0)),
                  pl.BlockSpec((1,) + lpow.shape[1:], lambda g: (g, 0, 0))],
        out_specs=pl.BlockSpec((1, kk, nr), lambda g: (g, 0, 0)),
        out_shape=jax.ShapeDtypeStruct((ngrp, kk, nr), BF16),
        compiler_params=_cparams("parallel"),
        name="s5_core",
    )(z, m_in, m_intra, c_st, lpow)


def _glu_body(y_ref, wv_ref, wg_ref, x_ref, o_ref):
    ngrp, grp, r = y_ref.shape
    y = y_ref[...].reshape(ngrp * grp, r)
    v = _dot(wv_ref[...], y)
    g = _dot(wg_ref[...], y)
    o_ref[...] = x_ref[...] + jnp.transpose(v * jax.nn.sigmoid(g))


def _s5_glu(y, wvt, wgt, xl):
    t, d = xl.shape
    ngrp, _, nr = y.shape
    r = _tile(nr, 512)
    nri = nr // r
    tn = _tile(d, 1024)
    return pl.pallas_call(
        _glu_body,
        grid=(nri, S5_L, d // tn),
        in_specs=[pl.BlockSpec((ngrp, S5_GROUP, r), lambda i, l, j: (0, l, i)),
                  pl.BlockSpec((tn, ngrp * S5_GROUP), lambda i, l, j: (j, 0)),
                  pl.BlockSpec((tn, ngrp * S5_GROUP), lambda i, l, j: (j, 0)),
                  pl.BlockSpec((r, tn), lambda i, l, j: (l * nri + i, j))],
        out_specs=pl.BlockSpec((r, tn), lambda i, l, j: (l * nri + i, j)),
        out_shape=jax.ShapeDtypeStruct((t, d), F32),
        compiler_params=_cparams("parallel", "arbitrary", "arbitrary"),
        name="s5_glu",
    )(y, wvt, wgt, xl)


def _s5_operators(a_re, a_im, log_step, b_re, b_im, c_re, c_im, d_skip, nseg):
    hi = lax.Precision.HIGHEST
    L, p, grp = S5_L, S5_STATE, S5_GROUP
    ngrp = a_re.shape[0]
    dt = jnp.exp(log_step)[:, None]
    mag = jnp.exp(a_re * dt)
    lr, li = mag * jnp.cos(a_im * dt), mag * jnp.sin(a_im * dt)
    den = a_re * a_re + a_im * a_im
    fr = ((lr - 1.0) * a_re + li * a_im) / den
    fi = (li * a_re - (lr - 1.0) * a_im) / den
    bbr = fr[..., None] * b_re - fi[..., None] * b_im
    bbi = fr[..., None] * b_im + fi[..., None] * b_re
    pr, pi = [jnp.ones_like(lr)], [jnp.zeros_like(li)]
    for _ in range(L):
        pr_n = pr[-1] * lr - pi[-1] * li
        pi_n = pr[-1] * li + pi[-1] * lr
        pr.append(pr_n)
        pi.append(pi_n)
    pwr, pwi = jnp.stack(pr, 1), jnp.stack(pi, 1)
    lbr = pwr[:, :L, :, None] * bbr[:, None] - pwi[:, :L, :, None] * bbi[:, None]
    lbi = pwr[:, :L, :, None] * bbi[:, None] + pwi[:, :L, :, None] * bbr[:, None]
    m_in = jnp.concatenate([jnp.transpose(lbr[:, ::-1], (0, 2, 1, 3)).reshape(ngrp, p, L * grp),
                            jnp.transpose(lbi[:, ::-1], (0, 2, 1, 3)).reshape(ngrp, p, L * grp)], axis=1)
    taps = (jnp.einsum('gcp,gkpd->gkcd', c_re, lbr, precision=hi)
            - jnp.einsum('gcp,gkpd->gkcd', c_im, lbi, precision=hi))
    taps = taps.at[:, 0].add(jax.vmap(jnp.diag)(d_skip.reshape(ngrp, grp)))
    lag = jnp.arange(L)[:, None] - jnp.arange(L)[None, :]
    blocks = jnp.where((lag >= 0)[None, :, :, None, None], taps[:, jnp.clip(lag, 0, L - 1)], 0.0)
    m_intra = jnp.transpose(blocks, (0, 1, 3, 2, 4)).reshape(ngrp, L * grp, L * grp)
    qr, qi = pwr[:, 1:], pwi[:, 1:]
    cs_r = c_re[:, None] * qr[:, :, None, :] - c_im[:, None] * qi[:, :, None, :]
    cs_i = -c_re[:, None] * qi[:, :, None, :] - c_im[:, None] * qr[:, :, None, :]
    c_st = jnp.concatenate([cs_r, cs_i], axis=-1).reshape(ngrp, L * grp, 2 * p)
    nlev = max(1, int(math.log2(nseg)))
    sr_, si_ = [pwr[:, L]], [pwi[:, L]]
    for _ in range(nlev - 1):
        sr_.append(sr_[-1] * sr_[-1] - si_[-1] * si_[-1])
        si_.append(2.0 * sr_[-2] * si_[-1])
    lpow = jnp.stack(sr_ + si_, axis=-1)
    return m_in.astype(BF16), m_intra.astype(BF16), c_st.astype(BF16), lpow


def _router_body(x_ref, g_ref, whi_ref, wlo_ref, b_ref, tri_ref, o_ref, cnt_ref, carry_ref):
    @pl.when(pl.program_id(0) == 0)
    def _():
        carry_ref[...] = jnp.zeros_like(carry_ref)

    h = _rms(x_ref[...], g_ref[...])
    h_hi = h.astype(BF16)
    h_lo = (h - h_hi.astype(F32)).astype(BF16)
    whi = whi_ref[...]
    logits = _dot(h_hi, whi) + _dot(h_lo, whi) + _dot(h_hi, wlo_ref[...]) + b_ref[...]
    lane = lax.broadcasted_iota(jnp.int32, logits.shape, 1).astype(F32)
    neg, big = -jnp.inf, 1e9

    def first_argmax(vals):
        mx = jnp.max(vals, axis=-1, keepdims=True)
        return mx, jnp.min(jnp.where(vals == mx, lane, big), axis=-1, keepdims=True)

    is_c = lane < MOE_GROUPS
    mx, gi = first_argmax(jnp.where(is_c, logits, neg))
    p_g = 1.0 / jnp.sum(jnp.where(is_c, jnp.exp(logits - mx), 0.0), axis=-1, keepdims=True)
    base = MOE_GROUPS + MOE_PER_GROUP * gi
    lf = jnp.where((lane >= base) & (lane < base + MOE_PER_GROUP), logits, neg)
    v1, i1 = first_argmax(lf)
    v2, i2 = first_argmax(jnp.where(lane == i1, neg, lf))
    e = jnp.exp(v2 - v1)
    w1 = p_g / (1.0 + e)
    w2 = p_g * e / (1.0 + e)
    k1, k2 = i1 - base, i2 - base
    first_low = k1 < k2
    lo, hi = jnp.minimum(k1, k2), jnp.maximum(k1, k2)
    w_lo, w_hi = jnp.where(first_low, w1, w2), jnp.where(first_low, w2, w1)
    cls = gi * N_PAIRS + lo * (7.0 - lo) * 0.5 + (hi - lo - 1.0)
    onehot = jnp.where(lane == cls, 1.0, 0.0)
    before = _dot(tri_ref[...], onehot.astype(BF16))
    carry = carry_ref[...]
    rank = jnp.sum(onehot * (before + carry), axis=-1, keepdims=True)
    carry = carry + jnp.sum(onehot, axis=0, keepdims=True)
    carry_ref[...] = carry
    cnt_ref[...] = carry
    out = jnp.where(lane == 0, cls, jnp.where(lane == 1, rank, jnp.where(lane == 2, w_lo,
                    jnp.where(lane == 3, w_hi, 0.0))))
    o_ref[...] = jnp.transpose(out)[0:8, :]


def _router(x2, g, w_hi, w_lo, bias):
    t, d = x2.shape
    tm = _tile(t, 512)
    tri = jnp.tril(jnp.ones((tm, tm), BF16), -1)
    return pl.pallas_call(
        _router_body,
        grid=(t // tm,),
        in_specs=[pl.BlockSpec((tm, d), lambda i: (i, 0)),
                  pl.BlockSpec((1, d), lambda i: (0, 0)),
                  pl.BlockSpec((d, LANES), lambda i: (0, 0)),
                  pl.BlockSpec((d, LANES), lambda i: (0, 0)),
                  pl.BlockSpec((1, LANES), lambda i: (0, 0)),
                  pl.BlockSpec((tm, tm), lambda i: (0, 0))],
        out_specs=[pl.BlockSpec((8, tm), lambda i: (0, i)),
                   pl.BlockSpec((1, LANES), lambda i: (0, 0))],
        out_shape=[jax.ShapeDtypeStruct((8, t), F32), jax.ShapeDtypeStruct((1, LANES), F32)],
        scratch_shapes=[pltpu.VMEM((1, LANES), F32)],
        compiler_params=_cparams("arbitrary"),
        name="router",
    )(x2, g, w_hi, w_lo, bias, tri)


def _moe_body(src_ref, nv_ref, elo_ref, ehi_ref,
              x_hbm, rw_ref, g_ref, gfin_ref, wg_lo, wu_lo, wd_lo, wg_hi, wu_hi, wd_hi,
              o_hbm, xbuf, obuf, sem_in, sem_out, *, final, out_order, nr):
    i = pl.program_id(0)
    nt = pl.num_programs(0)
    tm = xbuf.shape[1]
    slot = i % 2
    nv = nv_ref[i]

    def divmod_nonneg(v, m):
        if m & (m - 1) == 0:
            sh = m.bit_length() - 1
            return lax.shift_right_logical(v, sh), v & (m - 1)
        return v // m, v % m

    def out_row(row):
        if out_order == "fold16":
            q, r = divmod_nonneg(row, S5_L)
            return r * nr + q
        if out_order == "unfold16":
            q, r = divmod_nonneg(row, nr)
            return r * S5_L + q
        return row

    def gather_row(tile, s, r):
        row = src_ref[tile * tm + r]
        return pltpu.make_async_copy(x_hbm.at[pl.ds(row, 1), :], xbuf.at[s, pl.ds(r, 1), :], sem_in.at[s])

    def wait_gather(s):
        pltpu.make_async_copy(x_hbm.at[pl.ds(0, tm), :], xbuf.at[s], sem_in.at[s]).wait()

    def scatter_row(tile, s, r):
        row = out_row(src_ref[tile * tm + r])
        return pltpu.make_async_copy(obuf.at[s, pl.ds(r, 1), :], o_hbm.at[pl.ds(row, 1), :], sem_out.at[s])

    def issue_scatter(tile, s, n):
        for b in range(tm // 8):
            @pl.when(8 * b + 8 <= n)
            def _(b=b):
                for u in range(8):
                    scatter_row(tile, s, 8 * b + u).start()

        def single(r, c):
            scatter_row(tile, s, r).start()
            return c

        lax.fori_loop((n // 8) * 8, n, single, 0)

    def wait_scatter(s, n):
        p = 1
        while p <= tm:
            @pl.when((n & p) != 0)
            def _(p=p):
                pltpu.make_async_copy(obuf.at[s, pl.ds(0, p), :], o_hbm.at[pl.ds(0, p), :], sem_out.at[s]).wait()
            p *= 2

    @pl.when((i == 0) & (nv > 0))
    def _():
        def body(r, c):
            gather_row(0, 0, r).start()
            return c

        lax.fori_loop(0, tm, body, 0, unroll=8)

    prev_used = (i > 0) & (nv_ref[jnp.maximum(i - 1, 0)] > 0)

    def tile_step(s):
        wait_gather(s)
        x = xbuf[s]
        hb = _rms(x, g_ref[...]).astype(BF16)
        nxt = jnp.minimum(i + 1, nt - 1)
        for r in range(tm):
            gather_row(nxt, 1 - s, r).start()

        def expert(wg, wu, wd):
            a = _dot(hb, wg[0])
            u = _dot(hb, wu[0])
            he = (a * jax.nn.sigmoid(a) * u).astype(BF16)
            return _dot(he, wd[0])

        rw = rw_ref[...]
        out = x + rw[:, 2:3] * expert(wg_lo, wu_lo, wd_lo) + rw[:, 3:4] * expert(wg_hi, wu_hi, wd_hi)
        if final:
            out = _rms(out, gfin_ref[...])
        obuf[s] = out
        issue_scatter(i, s, nv)

    for s in (0, 1):
        @pl.when((nv > 0) & (slot == s))
        def _(s=s):
            tile_step(s)

        @pl.when((nv == 0) & prev_used & (slot == s))
        def _(s=s):
            wait_gather(s)

        @pl.when(prev_used & (slot == s))
        def _(s=s):
            wait_scatter(1 - s, nv_ref[jnp.maximum(i - 1, 0)])

        @pl.when((i == nt - 1) & (nv > 0) & (slot == s))
        def _(s=s):
            wait_gather(1 - s)
            wait_scatter(s, nv)


def _moe(x2, rws, src, nvalid, elo, ehi, g, gfin, wg, wu, wd, tm, final, out_order):
    t, d = x2.shape
    nt = nvalid.shape[0]
    ff = wg.shape[2]
    cm = lambda i, *_: (0, 0)
    lo3 = lambda i, src, nv, elo, ehi: (elo[i], 0, 0)
    hi3 = lambda i, src, nv, elo, ehi: (ehi[i], 0, 0)
    return pl.pallas_call(
        functools.partial(_moe_body, final=final, out_order=out_order, nr=t // S5_L),
        grid_spec=pltpu.PrefetchScalarGridSpec(
            num_scalar_prefetch=4,
            grid=(nt,),
            in_specs=[pl.BlockSpec(memory_space=pl.ANY),
                      pl.BlockSpec((tm, 8), lambda i, *_: (i, 0)),
                      pl.BlockSpec((1, d), cm),
                      pl.BlockSpec((1, d), cm),
                      pl.BlockSpec((1, d, ff), lo3), pl.BlockSpec((1, d, ff), lo3), pl.BlockSpec((1, ff, d), lo3),
                      pl.BlockSpec((1, d, ff), hi3), pl.BlockSpec((1, d, ff), hi3), pl.BlockSpec((1, ff, d), hi3)],
            out_specs=pl.BlockSpec(memory_space=pl.ANY),
            scratch_shapes=[pltpu.VMEM((2, tm, d), F32), pltpu.VMEM((2, tm, d), F32),
                            pltpu.SemaphoreType.DMA((2,)), pltpu.SemaphoreType.DMA((2,))],
        ),
        out_shape=jax.ShapeDtypeStruct((t, d), F32),
        compiler_params=_cparams("arbitrary"),
        name="moe_final" if final else "moe",
    )(src, nvalid, elo, ehi, x2, rws, g, gfin, wg, wu, wd, wg, wu, wd)


def _moe_layer(x2, g, gfin, w_coarse, b_coarse, w_fine, b_fine, wg, wu, wd, final, out_order):
    t, d = x2.shape
    wr = jnp.zeros((d, LANES), F32).at[:, :MOE_GROUPS].set(w_coarse)
    wr = wr.at[:, MOE_GROUPS:MOE_GROUPS + w_fine.shape[1]].set(w_fine)
    wr_hi = wr.astype(BF16)
    wr_lo = (wr - wr_hi.astype(F32)).astype(BF16)
    bias = jnp.zeros((1, LANES), F32).at[0, :MOE_GROUPS].set(b_coarse)
    bias = bias.at[0, MOE_GROUPS:MOE_GROUPS + b_fine.shape[0]].set(b_fine)
    rw, cnt = _router(x2, g, wr_hi, wr_lo, bias)

    tm = _tile(t, 256)
    nt = t // tm + N_CLASSES
    i32 = jnp.int32
    cls = rw[0].astype(i32)
    rank = rw[1].astype(i32)
    counts = cnt[0, :N_CLASSES].astype(i32)
    tiles_per = (counts + tm - 1) // tm
    tile_end = jnp.cumsum(tiles_per)
    tile_start = tile_end - tiles_per
    class_ids = jnp.arange(N_CLASSES, dtype=i32)
    dest = jnp.sum(jnp.where(cls[:, None] == class_ids[None, :], (tile_start * tm)[None, :], 0), axis=1) + rank
    src = jnp.zeros((nt * tm,), i32).at[dest].set(jnp.arange(t, dtype=i32))
    tile_ids = jnp.arange(nt, dtype=i32)
    total = tile_end[-1]
    used = tile_ids < total
    tcls = jnp.sum((jnp.minimum(tile_ids, total - 1)[:, None] >= tile_end[None, :]).astype(i32), axis=1)
    tcls = jnp.clip(tcls, 0, N_CLASSES - 1)
    sel = tcls[:, None] == class_ids[None, :]
    tcount = jnp.sum(jnp.where(sel, counts[None, :], 0), axis=1)
    tstart = jnp.sum(jnp.where(sel, tile_start[None, :], 0), axis=1)
    nvalid = jnp.where(used, jnp.clip(tcount - (tile_ids - tstart) * tm, 0, tm), 0).astype(i32)
    grp, pair = tcls // N_PAIRS, tcls % N_PAIRS
    pair_ids = jnp.arange(N_PAIRS, dtype=i32)
    psel = pair[:, None] == pair_ids[None, :]
    elo = (grp * MOE_PER_GROUP + jnp.sum(jnp.where(psel, jnp.array(PAIR_LO, i32)[None, :], 0), axis=1)).astype(i32)
    ehi = (grp * MOE_PER_GROUP + jnp.sum(jnp.where(psel, jnp.array(PAIR_HI, i32)[None, :], 0), axis=1)).astype(i32)
    rws = jnp.take(jnp.transpose(rw), src, axis=0)
    return _moe(x2, rws, src, nvalid, elo, ehi, g, gfin, wg, wu, wd, tm, final, out_order)


def _mixer0(x2, bsz, seq, g, w_in, lru_conv_w, lru_conv_b, lru_w_a, lru_b_a, lru_w_x, lru_b_x, lru_lam,
            m_conv_w, m_conv_b, m_i_bias, m_f_bias, m_head_g, w_out):
    d = x2.shape[1]
    c = lru_conv_w.shape[1]
    mw = m_head_g.shape[0]
    assert mw == c and m_conv_w.shape[1] == 2 * c
    nmain = 2 * c + 4 * mw
    w_gate = jnp.zeros((d, LANES), F32).at[:, :2 * MLSTM_HEADS].set(w_in[:, nmain:]).astype(BF16)
    conv_w = jnp.concatenate([lru_conv_w, m_conv_w], axis=1)
    conv_b = jnp.concatenate([lru_conv_b, m_conv_b])[None]
    pc, pb, gates = _in_proj0(x2, g[None], w_in.astype(BF16), w_gate, conv_w, conv_b, seq, c, mw // MLSTM_HEADS)
    ya = _rg_lru(pc, pb, bsz, seq, lru_w_a.astype(BF16), lru_b_a[None], lru_w_x.astype(BF16), lru_b_x[None],
                 lru_lam[None])
    gate_bias = jnp.zeros((1, LANES), F32).at[0, :MLSTM_HEADS].set(m_i_bias)
    gate_bias = gate_bias.at[0, MLSTM_HEADS:2 * MLSTM_HEADS].set(m_f_bias)
    yb = _mlstm(pb, gates, gate_bias, bsz, seq, m_head_g[None], c)
    return _out_proj(ya, yb, w_out.astype(BF16), x2)


def _mixer1(x2, seq, g, w_in, a_re, a_im, log_step, b_re, b_im, c_re, c_im, d_skip, w_glu_v, w_glu_g):
    ops = _s5_operators(a_re, a_im, log_step, b_re, b_im, c_re, c_im, d_skip, seq // S5_L)
    z = _s5_in(x2, g[None], jnp.transpose(w_in).astype(BF16))
    y = _s5_core(z, *ops, seq)
    return _s5_glu(y, jnp.transpose(w_glu_v).astype(BF16), jnp.transpose(w_glu_g).astype(BF16), x2)


def kernel(x, norm_mix, norm_ffn, norm_final, ab_w_in, lru_conv_w, lru_conv_b, lru_w_a, lru_b_a, lru_w_x, lru_b_x,
           lru_lam, m_conv_w, m_conv_b, m_i_bias, m_f_bias, m_head_g, ab_w_out, s5_w_in, s5_a_re, s5_a_im,
           s5_log_step, s5_b_re, s5_b_im, s5_c_re, s5_c_im, s5_d, s5_w_glu_v, s5_w_glu_g, moe_w_coarse,
           moe_b_coarse, moe_w_fine, moe_b_fine, moe_w_gate, moe_w_up, moe_w_down):
    bsz, seq, d = x.shape
    depth = norm_mix.shape[0]
    x2 = x.reshape(bsz * seq, d)
    gfin = norm_final[None]
    for layer in range(depth):
        j = layer // 2
        last = layer == depth - 1
        if layer % 2 == 0:
            x2 = _mixer0(x2, bsz, seq, norm_mix[layer], ab_w_in[j], lru_conv_w[j], lru_conv_b[j], lru_w_a[j],
                         lru_b_a[j], lru_w_x[j], lru_b_x[j], lru_lam[j], m_conv_w[j], m_conv_b[j], m_i_bias[j],
                         m_f_bias[j], m_head_g[j], ab_w_out[j])
            out_order = "same" if last else "fold16"
        else:
            x2 = _mixer1(x2, seq, norm_mix[layer], s5_w_in[j], s5_a_re[j], s5_a_im[j], s5_log_step[j], s5_b_re[j],
                         s5_b_im[j], s5_c_re[j], s5_c_im[j], s5_d[j], s5_w_glu_v[j], s5_w_glu_g[j])
            out_order = "unfold16"
        x2 = _moe_layer(x2, norm_ffn[layer][None], gfin, moe_w_coarse[layer], moe_b_coarse[layer],
                        moe_w_fine[layer], moe_b_fine[layer], moe_w_gate[layer].astype(BF16),
                        moe_w_up[layer].astype(BF16), moe_w_down[layer].astype(BF16), last, out_order)
    return x2.reshape(bsz, seq, d)
```

```python
import functools
import math

import jax
import jax.numpy as jnp
from jax import lax
from jax.experimental import pallas as pl
from jax.experimental.pallas import tpu as pltpu

F32 = jnp.float32
BF16 = jnp.bfloat16
EPS = 1e-6
LANES = 128
VMEM_LIMIT = 56 * 1024 * 1024

CONV_K = 4
LRU_BLOCKS = 8
LRU_C = 8.0
MLSTM_HEADS = 4
S5_GROUP = 16
S5_STATE = 64
S5_L = 16
MOE_GROUPS = 4
MOE_PER_GROUP = 4
N_PAIRS = 6
N_CLASSES = MOE_GROUPS * N_PAIRS
PAIR_LO = (0, 0, 0, 1, 1, 2)
PAIR_HI = (1, 2, 3, 2, 3, 3)


def _tile(n, pref):
    return pref if n % pref == 0 else n


def _cparams(*sem):
    return pltpu.CompilerParams(dimension_semantics=sem, vmem_limit_bytes=VMEM_LIMIT)


def _rms(x, g):
    return x * lax.rsqrt(jnp.mean(x * x, axis=-1, keepdims=True) + EPS) * g


def _dot(a, b):
    return jnp.dot(a, b, preferred_element_type=F32)


def _dot_nt(a, b):
    return lax.dot_general(a, b, (((1,), (1,)), ((), ())), preferred_element_type=F32)


def _dot_tn(a, b):
    return lax.dot_general(a, b, (((0,), (0,)), ((), ())), preferred_element_type=F32)


def _softplus(y):
    return jnp.maximum(y, 0.0) + jnp.log1p(jnp.exp(-jnp.abs(y)))


def _split3(x):
    p1 = x.astype(BF16)
    r1 = x - p1.astype(F32)
    p2 = r1.astype(BF16)
    p3 = (r1 - p2.astype(F32)).astype(BF16)
    return p1, p2, p3


def _causal_conv(x, tail, w, b):
    n = x.shape[0]
    xp = jnp.concatenate([tail, x], axis=0)
    out = b + w[CONV_K - 1:CONV_K, :] * x
    for j in range(CONV_K - 1):
        off = 8 - (CONV_K - 1) + j
        out = out + w[j:j + 1, :] * xp[off:off + n, :]
    return out


def _in0_body(x_ref, g_ref, w_ref, wg_ref, cw_ref, cb_ref, pc_ref, pb_ref, gate_ref, tail_ref, *, seq_tiles, dh):
    tm, c = pc_ref.shape

    @pl.when(pl.program_id(0) == 0)
    def _():
        tail_ref[...] = jnp.zeros_like(tail_ref)

    hb = _rms(x_ref[...], g_ref[...]).astype(BF16)
    gate_ref[...] = _dot(hb, wg_ref[...])
    seq_start = (pl.program_id(0) % seq_tiles) == 0

    def proj(jt):
        return _dot(hb, w_ref[:, jt * c:(jt + 1) * c])

    def conv(k, raw):
        tail = jnp.where(seq_start, 0.0, tail_ref[k])
        tail_ref[k] = raw[tm - 8:tm, :]
        return _causal_conv(raw, tail, cw_ref[:, k * c:(k + 1) * c], cb_ref[:, k * c:(k + 1) * c])

    pc_ref[...] = conv(0, proj(0))
    pb_ref[:, 0:c] = jax.nn.gelu(proj(1)).astype(BF16)
    q = conv(1, proj(2))
    pb_ref[:, c:2 * c] = (q * jax.nn.sigmoid(q)).astype(BF16)
    k = conv(2, proj(3))
    pb_ref[:, 2 * c:3 * c] = (k * jax.nn.sigmoid(k) * (dh ** -0.5)).astype(BF16)
    pb_ref[:, 3 * c:4 * c] = proj(4).astype(BF16)
    pb_ref[:, 4 * c:5 * c] = jax.nn.sigmoid(proj(5)).astype(BF16)


def _in_proj0(x2, g, w, w_gate, conv_w, conv_b, seq, c, dh):
    t, d = x2.shape
    tm = _tile(seq, 512)
    assert seq % tm == 0
    return pl.pallas_call(
        functools.partial(_in0_body, seq_tiles=seq // tm, dh=dh),
        grid=(t // tm,),
        in_specs=[pl.BlockSpec((tm, d), lambda i: (i, 0)),
                  pl.BlockSpec((1, d), lambda i: (0, 0)),
                  pl.BlockSpec(w.shape, lambda i: (0, 0), pipeline_mode=pl.Buffered(1)),
                  pl.BlockSpec((d, LANES), lambda i: (0, 0)),
                  pl.BlockSpec(conv_w.shape, lambda i: (0, 0)),
                  pl.BlockSpec(conv_b.shape, lambda i: (0, 0))],
        out_specs=[pl.BlockSpec((tm, c), lambda i: (i, 0)),
                   pl.BlockSpec((tm, 5 * c), lambda i: (i, 0)),
                   pl.BlockSpec((tm, LANES), lambda i: (i, 0))],
        out_shape=[jax.ShapeDtypeStruct((t, c), F32), jax.ShapeDtypeStruct((t, 5 * c), BF16),
                   jax.ShapeDtypeStruct((t, LANES), F32)],
        scratch_shapes=[pltpu.VMEM((3, 8, c), F32)],
        compiler_params=_cparams("arbitrary"),
        name="in_proj0",
    )(x2, g, w, w_gate, conv_w, conv_b)


def _lru_body(x_ref, gz_ref, wa_ref, ba_ref, wx_ref, bx_ref, lam_ref, y_ref, h_ref, a_s, g_s):
    ts, c = x_ref.shape
    blk = c // LRU_BLOCKS

    @pl.when(pl.program_id(1) == 0)
    def _():
        h_ref[...] = jnp.zeros_like(h_ref)

    conv = x_ref[...]
    cb16 = conv.astype(BF16)
    ra = jnp.concatenate([_dot(cb16[:, n * blk:(n + 1) * blk], wa_ref[n]) for n in range(LRU_BLOCKS)], axis=1)
    rx = jnp.concatenate([_dot(cb16[:, n * blk:(n + 1) * blk], wx_ref[n]) for n in range(LRU_BLOCKS)], axis=1)
    r = jax.nn.sigmoid(ra + ba_ref[...])
    ig = jax.nn.sigmoid(rx + bx_ref[...])
    log_a = -LRU_C * r * _softplus(-lam_ref[...])
    a = jnp.exp(log_a)
    pa = a
    pg = jnp.sqrt(1.0 - a * a) * (ig * conv)
    row8 = lax.broadcasted_iota(jnp.int32, (ts, c), 0) & 7
    for s in (1, 2, 4):
        keep = row8 >= s
        ga = jnp.where(keep, pltpu.roll(pg, s, axis=0), 0.0)
        aa = jnp.where(keep, pltpu.roll(pa, s, axis=0), 1.0)
        pg = pa * ga + pg
        pa = pa * aa
    a_s[...] = pa
    g_s[...] = pg

    def group(k, h):
        r0 = pl.multiple_of(k * 8, 8)
        h8 = a_s[pl.ds(r0, 8), :] * h + g_s[pl.ds(r0, 8), :]
        g_s[pl.ds(r0, 8), :] = h8
        return h8[7:8, :]

    h_ref[...] = lax.fori_loop(0, ts // 8, group, h_ref[...], unroll=4)
    y_ref[...] = (g_s[...] * gz_ref[...].astype(F32)).astype(y_ref.dtype)


def _rg_lru(pc, pb, bsz, seq, wa, ba, wx, bx, lam):
    c = pc.shape[1]
    ts = _tile(seq, 512)
    ns = seq // ts
    row = lambda b, s: (0, 0)
    return pl.pallas_call(
        _lru_body,
        grid=(bsz, ns),
        in_specs=[pl.BlockSpec((ts, c), lambda b, s: (b * ns + s, 0)),
                  pl.BlockSpec((ts, c), lambda b, s: (b * ns + s, 0)),
                  pl.BlockSpec(wa.shape, lambda b, s: (0, 0, 0)),
                  pl.BlockSpec((1, c), row),
                  pl.BlockSpec(wx.shape, lambda b, s: (0, 0, 0)),
                  pl.BlockSpec((1, c), row),
                  pl.BlockSpec((1, c), row)],
        out_specs=pl.BlockSpec((ts, c), lambda b, s: (b * ns + s, 0)),
        out_shape=jax.ShapeDtypeStruct((bsz * seq, c), BF16),
        scratch_shapes=[pltpu.VMEM((1, c), F32), pltpu.VMEM((ts, c), F32), pltpu.VMEM((ts, c), F32)],
        compiler_params=_cparams("parallel", "arbitrary"),
        name="rg_lru",
    )(pc, pb, wa, ba, wx, bx, lam)


def _mlstm_body(q_ref, k_ref, v_ref, so_ref, gt_ref, gb_ref, hg_ref, tri_ref, y_ref, ct_ref, n_ref, m_ref):
    L = q_ref.shape[0]
    nh = ct_ref.shape[0]
    dh = ct_ref.shape[1]

    @pl.when(pl.program_id(1) == 0)
    def _():
        ct_ref[...] = jnp.zeros_like(ct_ref)
        n_ref[...] = jnp.zeros_like(n_ref)
        m_ref[...] = jnp.zeros_like(m_ref)

    gts = gt_ref[...] + gb_ref[...]
    lane = lax.broadcasted_iota(jnp.int32, gts.shape, 1)
    lsig = jnp.minimum(gts, 0.0) - jnp.log1p(jnp.exp(-jnp.abs(gts)))
    tri = tri_ref[...]
    p1, p2, p3 = _split3(lsig)
    bc_all = _dot(tri, p1) + _dot(tri, p2) + _dot(tri, p3)
    rows = lax.broadcasted_iota(jnp.int32, (L, L), 0)
    cols = lax.broadcasted_iota(jnp.int32, (L, L), 1)
    causal = rows >= cols

    for head in range(nh):
        hs = slice(head * dh, (head + 1) * dh)
        i_col = jnp.sum(jnp.where(lane == head, gts, 0.0), axis=-1, keepdims=True)
        bcum = jnp.sum(jnp.where(lane == head + nh, bc_all, 0.0), axis=-1, keepdims=True)
        a_col = i_col - bcum
        a_row = jnp.transpose(jnp.broadcast_to(a_col, (L, LANES)))[0:1, :]
        m_st = m_ref[head, 0:1, 0:1]

        d = jnp.where(causal, bcum + a_row, -jnp.inf)
        g = bcum + m_st
        m_t = jnp.maximum(g, jnp.max(d, axis=-1, keepdims=True))
        w_intra = jnp.exp(d - m_t)
        w_inter = jnp.exp(g - m_t)

        qb = q_ref[:, hs]
        kb = k_ref[:, hs]
        vb = v_ref[:, hs]
        s_qk = _dot_nt(qb, kb) * w_intra
        ct = ct_ref[head]
        n_row = n_ref[head]
        num = _dot(s_qk.astype(BF16), vb) + w_inter * _dot(qb, ct.astype(BF16))
        den = (jnp.sum(s_qk, axis=-1, keepdims=True)
               + w_inter * jnp.sum(qb.astype(F32) * n_row, axis=-1, keepdims=True))
        hcell = num / jnp.maximum(jnp.abs(den), jnp.exp(-m_t))

        b_last = bcum[L - 1:L, :]
        d_end = b_last + a_col
        m_new = jnp.maximum(b_last + m_st, jnp.max(d_end, axis=0, keepdims=True))
        w_end = jnp.exp(d_end - m_new)
        decay = jnp.exp(b_last + m_st - m_new)
        kw = kb.astype(F32) * w_end
        ct_ref[head] = decay * ct + _dot_tn(kw.astype(BF16), vb)
        n_ref[head] = decay * n_row + jnp.sum(kw, axis=0, keepdims=True)
        m_ref[head] = jnp.broadcast_to(m_new, m_ref.shape[1:])

        hm = so_ref[:, hs].astype(F32) * hcell
        hm = hm * lax.rsqrt(jnp.mean(hm * hm, axis=-1, keepdims=True) + EPS)
        y_ref[:, hs] = (hm * hg_ref[:, hs]).astype(y_ref.dtype)


def _mlstm(pb, gates, gate_bias, bsz, seq, head_g, col0):
    nh = MLSTM_HEADS
    w = head_g.shape[1]
    dh = w // nh
    L = _tile(seq, 256)
    nc = seq // L
    cq = col0 // w
    tri = jnp.tril(jnp.ones((L, L), BF16))

    def pblock(off):
        return pl.BlockSpec((L, w), lambda b, c: (b * nc + c, off))

    return pl.pallas_call(
        _mlstm_body,
        grid=(bsz, nc),
        in_specs=[pblock(cq), pblock(cq + 1), pblock(cq + 2), pblock(cq + 3),
                  pl.BlockSpec((L, LANES), lambda b, c: (b * nc + c, 0)),
                  pl.BlockSpec((1, LANES), lambda b, c: (0, 0)),
                  pl.BlockSpec((1, w), lambda b, c: (0, 0)),
                  pl.BlockSpec((L, L), lambda b, c: (0, 0))],
        out_specs=pl.BlockSpec((L, w), lambda b, c: (b * nc + c, 0)),
        out_shape=jax.ShapeDtypeStruct((bsz * seq, w), BF16),
        scratch_shapes=[pltpu.VMEM((nh, dh, dh), F32), pltpu.VMEM((nh, 1, dh), F32),
                        pltpu.VMEM((nh, 8, LANES), F32)],
        compiler_params=_cparams("parallel", "arbitrary"),
        name="mlstm",
    )(pb, pb, pb, pb, gates, gate_bias, head_g, tri)


def _outproj_body(ya_ref, yb_ref, wa_ref, wb_ref, x_ref, o_ref):
    o_ref[...] = x_ref[...] + _dot(ya_ref[...], wa_ref[...]) + _dot(yb_ref[...], wb_ref[...])


def _out_proj(ya, yb, w, x2):
    t, d = x2.shape
    kh = ya.shape[1]
    assert yb.shape[1] == kh and w.shape[0] == 2 * kh
    tm, tn = _tile(t, 1024), _tile(d, 1024)
    return pl.pallas_call(
        _outproj_body,
        grid=(t // tm, d // tn),
        in_specs=[pl.BlockSpec((tm, kh), lambda i, j: (i, 0)),
                  pl.BlockSpec((tm, kh), lambda i, j: (i, 0)),
                  pl.BlockSpec((kh, tn), lambda i, j: (0, j)),
                  pl.BlockSpec((kh, tn), lambda i, j: (1, j)),
                  pl.BlockSpec((tm, tn), lambda i, j: (i, j))],
        out_specs=pl.BlockSpec((tm, tn), lambda i, j: (i, j)),
        out_shape=jax.ShapeDtypeStruct((t, d), F32),
        compiler_params=_cparams("parallel", "arbitrary"),
        name="out_proj",
    )(ya, yb, w, w, x2)


def _s5in_body(x_ref, g_ref, wt_ref, z_ref):
    hb = _rms(x_ref[...], g_ref[...]).astype(BF16)
    res = _dot_nt(wt_ref[...], hb)
    z_ref[...] = res.reshape(z_ref.shape)


def _s5_in(xl, g, wt):
    t, d = xl.shape
    ch = wt.shape[0]
    ngrp = ch // S5_GROUP
    nr = t // S5_L
    r = _tile(nr, 512)
    nri = nr // r
    return pl.pallas_call(
        _s5in_body,
        grid=(nri, S5_L),
        in_specs=[pl.BlockSpec((r, d), lambda i, l: (l * nri + i, 0)),
                  pl.BlockSpec((1, d), lambda i, l: (0, 0)),
                  pl.BlockSpec((ch, d), lambda i, l: (0, 0))],
        out_specs=pl.BlockSpec((ngrp, S5_GROUP, r), lambda i, l: (0, l, i)),
        out_shape=jax.ShapeDtypeStruct((ngrp, S5_L * S5_GROUP, nr), F32),
        compiler_params=_cparams("parallel", "arbitrary"),
        name="s5_in",
    )(xl, g, wt)


def _s5core_body(z_ref, min_ref, mintra_ref, cst_ref, lp_ref, y_ref, *, nseg, nlev):
    p = S5_STATE
    zb = z_ref[0].astype(BF16)
    e = _dot(min_ref[0], zb)
    sr, si = e[:p, :], e[p:, :]
    nr = sr.shape[1]
    lane = lax.broadcasted_iota(jnp.int32, (1, nr), 1) % nseg
    lp = lp_ref[0]
    for j in range(nlev):
        s = 1 << j
        cr = lp[:, j:j + 1]
        ci = lp[:, nlev + j:nlev + j + 1]
        keep = lane >= s
        tr = jnp.where(keep, pltpu.roll(sr, s, axis=1), 0.0)
        ti = jnp.where(keep, pltpu.roll(si, s, axis=1), 0.0)
        sr, si = sr + cr * tr - ci * ti, si + cr * ti + ci * tr
    keep = lane >= 1
    xr = jnp.where(keep, pltpu.roll(sr, 1, axis=1), 0.0)
    xi = jnp.where(keep, pltpu.roll(si, 1, axis=1), 0.0)
    xb = jnp.concatenate([xr, xi], axis=0).astype(BF16)
    y = _dot(mintra_ref[0], zb) + _dot(cst_ref[0], xb)
    y_ref[0] = jax.nn.gelu(y).astype(y_ref.dtype)


def _s5_core(z, m_in, m_intra, c_st, lpow, seq):
    ngrp, kk, nr = z.shape
    nseg = seq // S5_L
    nlev = lpow.shape[2] // 2
    return pl.pallas_call(
        functools.partial(_s5core_body, nseg=nseg, nlev=nlev),
        grid=(ngrp,),
        in_specs=[pl.BlockSpec((1, kk, nr), lambda g: (g, 0, 0)),
                  pl.BlockSpec((1,) + m_in.shape[1:], lambda g: (g, 0, 0)),
                  pl.BlockSpec((1,) + m_intra.shape[1:], lambda g: (g, 0, 0)),
                  pl.BlockSpec((1,) + c_st.shape[1:], lambda g: (g, 0, 0)),
                  pl.BlockSpec((1,) + lpow.shape[1:], lambda g: (g, 0, 0))],
        out_specs=pl.BlockSpec((1, kk, nr), lambda g: (g, 0, 0)),
        out_shape=jax.ShapeDtypeStruct((ngrp, kk, nr), BF16),
        compiler_params=_cparams("parallel"),
        name="s5_core",
    )(z, m_in, m_intra, c_st, lpow)


def _glu_body(y_ref, wv_ref, wg_ref, x_ref, o_ref):
    ngrp, grp, r = y_ref.shape
    y = y_ref[...].reshape(ngrp * grp, r)
    v = _dot(wv_ref[...], y)
    g = _dot(wg_ref[...], y)
    o_ref[...] = x_ref[...] + jnp.transpose(v * jax.nn.sigmoid(g))


def _s5_glu(y, wvt, wgt, xl):
    t, d = xl.shape
    ngrp, _, nr = y.shape
    r = _tile(nr, 512)
    nri = nr // r
    tn = _tile(d, 1024)
    return pl.pallas_call(
        _glu_body,
        grid=(nri, S5_L, d // tn),
        in_specs=[pl.BlockSpec((ngrp, S5_GROUP, r), lambda i, l, j: (0, l, i)),
                  pl.BlockSpec((tn, ngrp * S5_GROUP), lambda i, l, j: (j, 0)),
                  pl.BlockSpec((tn, ngrp * S5_GROUP), lambda i, l, j: (j, 0)),
                  pl.BlockSpec((r, tn), lambda i, l, j: (l * nri + i, j))],
        out_specs=pl.BlockSpec((r, tn), lambda i, l, j: (l * nri + i, j)),
        out_shape=jax.ShapeDtypeStruct((t, d), F32),
        compiler_params=_cparams("parallel", "arbitrary", "arbitrary"),
        name="s5_glu",
    )(y, wvt, wgt, xl)


def _s5_operators(a_re, a_im, log_step, b_re, b_im, c_re, c_im, d_skip, nseg):
    hi = lax.Precision.HIGHEST
    L, p, grp = S5_L, S5_STATE, S5_GROUP
    ngrp = a_re.shape[0]
    dt = jnp.exp(log_step)[:, None]
    mag = jnp.exp(a_re * dt)
    lr, li = mag * jnp.cos(a_im * dt), mag * jnp.sin(a_im * dt)
    den = a_re * a_re + a_im * a_im
    fr = ((lr - 1.0) * a_re + li * a_im) / den
    fi = (li * a_re - (lr - 1.0) * a_im) / den
    bbr = fr[..., None] * b_re - fi[..., None] * b_im
    bbi = fr[..., None] * b_im + fi[..., None] * b_re
    pr, pi = [jnp.ones_like(lr)], [jnp.zeros_like(li)]
    for _ in range(L):
        pr_n = pr[-1] * lr - pi[-1] * li
        pi_n = pr[-1] * li + pi[-1] * lr
        pr.append(pr_n)
        pi.append(pi_n)
    pwr, pwi = jnp.stack(pr, 1), jnp.stack(pi, 1)
    lbr = pwr[:, :L, :, None] * bbr[:, None] - pwi[:, :L, :, None] * bbi[:, None]
    lbi = pwr[:, :L, :, None] * bbi[:, None] + pwi[:, :L, :, None] * bbr[:, None]
    m_in = jnp.concatenate([jnp.transpose(lbr[:, ::-1], (0, 2, 1, 3)).reshape(ngrp, p, L * grp),
                            jnp.transpose(lbi[:, ::-1], (0, 2, 1, 3)).reshape(ngrp, p, L * grp)], axis=1)
    taps = (jnp.einsum('gcp,gkpd->gkcd', c_re, lbr, precision=hi)
            - jnp.einsum('gcp,gkpd->gkcd', c_im, lbi, precision=hi))
    taps = taps.at[:, 0].add(jax.vmap(jnp.diag)(d_skip.reshape(ngrp, grp)))
    lag = jnp.arange(L)[:, None] - jnp.arange(L)[None, :]
    blocks = jnp.where((lag >= 0)[None, :, :, None, None], taps[:, jnp.clip(lag, 0, L - 1)], 0.0)
    m_intra = jnp.transpose(blocks, (0, 1, 3, 2, 4)).reshape(ngrp, L * grp, L * grp)
    qr, qi = pwr[:, 1:], pwi[:, 1:]
    cs_r = c_re[:, None] * qr[:, :, None, :] - c_im[:, None] * qi[:, :, None, :]
    cs_i = -c_re[:, None] * qi[:, :, None, :] - c_im[:, None] * qr[:, :, None, :]
    c_st = jnp.concatenate([cs_r, cs_i], axis=-1).reshape(ngrp, L * grp, 2 * p)
    nlev = max(1, int(math.log2(nseg)))
    sr_, si_ = [pwr[:, L]], [pwi[:, L]]
    for _ in range(nlev - 1):
        sr_.append(sr_[-1] * sr_[-1] - si_[-1] * si_[-1])
        si_.append(2.0 * sr_[-2] * si_[-1])
    lpow = jnp.stack(sr_ + si_, axis=-1)
    return m_in.astype(BF16), m_intra.astype(BF16), c_st.astype(BF16), lpow


def _router_body(x_ref, g_ref, whi_ref, wlo_ref, b_ref, tri_ref, o_ref, cnt_ref, carry_ref):
    @pl.when(pl.program_id(0) == 0)
    def _():
        carry_ref[...] = jnp.zeros_like(carry_ref)

    h = _rms(x_ref[...], g_ref[...])
    h_hi = h.astype(BF16)
    h_lo = (h - h_hi.astype(F32)).astype(BF16)
    whi = whi_ref[...]
    logits = _dot(h_hi, whi) + _dot(h_lo, whi) + _dot(h_hi, wlo_ref[...]) + b_ref[...]
    lane = lax.broadcasted_iota(jnp.int32, logits.shape, 1).astype(F32)
    neg, big = -jnp.inf, 1e9

    def first_argmax(vals):
        mx = jnp.max(vals, axis=-1, keepdims=True)
        return mx, jnp.min(jnp.where(vals == mx, lane, big), axis=-1, keepdims=True)

    is_c = lane < MOE_GROUPS
    mx, gi = first_argmax(jnp.where(is_c, logits, neg))
    p_g = 1.0 / jnp.sum(jnp.where(is_c, jnp.exp(logits - mx), 0.0), axis=-1, keepdims=True)
    base = MOE_GROUPS + MOE_PER_GROUP * gi
    lf = jnp.where((lane >= base) & (lane < base + MOE_PER_GROUP), logits, neg)
    v1, i1 = first_argmax(lf)
    v2, i2 = first_argmax(jnp.where(lane == i1, neg, lf))
    e = jnp.exp(v2 - v1)
    w1 = p_g / (1.0 + e)
    w2 = p_g * e / (1.0 + e)
    k1, k2 = i1 - base, i2 - base
    first_low = k1 < k2
    lo, hi = jnp.minimum(k1, k2), jnp.maximum(k1, k2)
    w_lo, w_hi = jnp.where(first_low, w1, w2), jnp.where(first_low, w2, w1)
    cls = gi * N_PAIRS + lo * (7.0 - lo) * 0.5 + (hi - lo - 1.0)
    onehot = jnp.where(lane == cls, 1.0, 0.0)
    before = _dot(tri_ref[...], onehot.astype(BF16))
    carry = carry_ref[...]
    rank = jnp.sum(onehot * (before + carry), axis=-1, keepdims=True)
    carry = carry + jnp.sum(onehot, axis=0, keepdims=True)
    carry_ref[...] = carry
    cnt_ref[...] = carry
    out = jnp.where(lane == 0, cls, jnp.where(lane == 1, rank, jnp.where(lane == 2, w_lo,
                    jnp.where(lane == 3, w_hi, 0.0))))
    o_ref[...] = jnp.transpose(out)[0:8, :]


def _router(x2, g, w_hi, w_lo, bias):
    t, d = x2.shape
    tm = _tile(t, 512)
    tri = jnp.tril(jnp.ones((tm, tm), BF16), -1)
    return pl.pallas_call(
        _router_body,
        grid=(t // tm,),
        in_specs=[pl.BlockSpec((tm, d), lambda i: (i, 0)),
                  pl.BlockSpec((1, d), lambda i: (0, 0)),
                  pl.BlockSpec((d, LANES), lambda i: (0, 0)),
                  pl.BlockSpec((d, LANES), lambda i: (0, 0)),
                  pl.BlockSpec((1, LANES), lambda i: (0, 0)),
                  pl.BlockSpec((tm, tm), lambda i: (0, 0))],
        out_specs=[pl.BlockSpec((8, tm), lambda i: (0, i)),
                   pl.BlockSpec((1, LANES), lambda i: (0, 0))],
        out_shape=[jax.ShapeDtypeStruct((8, t), F32), jax.ShapeDtypeStruct((1, LANES), F32)],
        scratch_shapes=[pltpu.VMEM((1, LANES), F32)],
        compiler_params=_cparams("arbitrary"),
        name="router",
    )(x2, g, w_hi, w_lo, bias, tri)


def _moe_body(src_ref, nv_ref, elo_ref, ehi_ref,
              x_hbm, whi_ref, wlo_ref, rb_ref, g_ref, gfin_ref, wg_lo, wu_lo, wd_lo, wg_hi, wu_hi, wd_hi,
              o_hbm, xbuf, obuf, sem_in, sem_out, *, final, out_order, nr):
    i = pl.program_id(0)
    nt = pl.num_programs(0)
    tm = xbuf.shape[1]
    slot = i % 2
    nv = nv_ref[i]

    def divmod_nonneg(v, m):
        if m & (m - 1) == 0:
            sh = m.bit_length() - 1
            return lax.shift_right_logical(v, sh), v & (m - 1)
        return v // m, v % m

    def out_row(row):
        if out_order == "fold16":
            q, r = divmod_nonneg(row, S5_L)
            return r * nr + q
        if out_order == "unfold16":
            q, r = divmod_nonneg(row, nr)
            return r * S5_L + q
        return row

    def gather_row(tile, s, r):
        row = src_ref[tile * tm + r]
        return pltpu.make_async_copy(x_hbm.at[pl.ds(row, 1), :], xbuf.at[s, pl.ds(r, 1), :], sem_in.at[s])

    def wait_gather(s):
        pltpu.make_async_copy(x_hbm.at[pl.ds(0, tm), :], xbuf.at[s], sem_in.at[s]).wait()

    def scatter_row(tile, s, r):
        row = out_row(src_ref[tile * tm + r])
        return pltpu.make_async_copy(obuf.at[s, pl.ds(r, 1), :], o_hbm.at[pl.ds(row, 1), :], sem_out.at[s])

    def issue_scatter(tile, s, n):
        for b in range(tm // 8):
            @pl.when(8 * b + 8 <= n)
            def _(b=b):
                for u in range(8):
                    scatter_row(tile, s, 8 * b + u).start(priority=u % 2)

        def single(r, c):
            scatter_row(tile, s, r).start()
            return c

        lax.fori_loop((n // 8) * 8, n, single, 0)

    def wait_scatter(s, n):
        p = 1
        while p <= tm:
            @pl.when((n & p) != 0)
            def _(p=p):
                pltpu.make_async_copy(obuf.at[s, pl.ds(0, p), :], o_hbm.at[pl.ds(0, p), :], sem_out.at[s]).wait()
            p *= 2

    @pl.when((i == 0) & (nv > 0))
    def _():
        def body(r, c):
            gather_row(0, 0, r).start()
            return c

        lax.fori_loop(0, tm, body, 0, unroll=8)

    prev_used = (i > 0) & (nv_ref[jnp.maximum(i - 1, 0)] > 0)

    def tile_step(s):
        wait_gather(s)
        x = xbuf[s]
        h = _rms(x, g_ref[...])
        hb = h.astype(BF16)
        nxt = jnp.minimum(i + 1, nt - 1)
        for r in range(tm):
            gather_row(nxt, 1 - s, r).start(priority=r % 2)

        def expert(wg, wu, wd):
            a = _dot(hb, wg[0, 0])
            u = _dot(hb, wu[0, 0])
            he = (a * jax.nn.sigmoid(a) * u).astype(BF16)
            return _dot(he, wd[0, 0])

        h_lo = (h - hb.astype(F32)).astype(BF16)
        whi = whi_ref[...]
        logits = _dot(hb, whi) + _dot(h_lo, whi) + _dot(hb, wlo_ref[...]) + rb_ref[...]
        lane = lax.broadcasted_iota(jnp.int32, logits.shape, 1)
        e_lo, e_hi = elo_ref[i], ehi_ref[i]
        is_c = lane < MOE_GROUPS
        mx = jnp.max(jnp.where(is_c, logits, -jnp.inf), axis=-1, keepdims=True)
        ex = jnp.where(is_c, jnp.exp(logits - mx), 0.0)
        p_g = (jnp.sum(jnp.where(lane == e_lo // MOE_PER_GROUP, ex, 0.0), axis=-1, keepdims=True)
               / jnp.sum(ex, axis=-1, keepdims=True))
        v_lo = jnp.sum(jnp.where(lane == MOE_GROUPS + e_lo, logits, 0.0), axis=-1, keepdims=True)
        v_hi = jnp.sum(jnp.where(lane == MOE_GROUPS + e_hi, logits, 0.0), axis=-1, keepdims=True)
        w_lo = p_g / (1.0 + jnp.exp(v_hi - v_lo))
        w_hi = p_g / (1.0 + jnp.exp(v_lo - v_hi))
        out = x + w_lo * expert(wg_lo, wu_lo, wd_lo) + w_hi * expert(wg_hi, wu_hi, wd_hi)
        if final:
            out = _rms(out, gfin_ref[...])
        obuf[s] = out
        issue_scatter(i, s, nv)

    for s in (0, 1):
        @pl.when((nv > 0) & (slot == s))
        def _(s=s):
            tile_step(s)

        @pl.when((nv == 0) & prev_used & (slot == s))
        def _(s=s):
            wait_gather(s)

        @pl.when(prev_used & (slot == s))
        def _(s=s):
            wait_scatter(1 - s, nv_ref[jnp.maximum(i - 1, 0)])

        @pl.when((i == nt - 1) & (nv > 0) & (slot == s))
        def _(s=s):
            wait_gather(1 - s)
            wait_scatter(s, nv)


def _moe(x2, wr_hi, wr_lo, rbias, src, nvalid, elo, ehi, g, gfin, wg, wu, wd, layer, tm, final, out_order):
    t, d = x2.shape
    nt = nvalid.shape[0]
    ff = wg.shape[3]
    cm = lambda i, *_: (0, 0)
    lo3 = lambda i, src, nv, elo, ehi: (layer, elo[i], 0, 0)
    hi3 = lambda i, src, nv, elo, ehi: (layer, ehi[i], 0, 0)
    return pl.pallas_call(
        functools.partial(_moe_body, final=final, out_order=out_order, nr=t // S5_L),
        grid_spec=pltpu.PrefetchScalarGridSpec(
            num_scalar_prefetch=4,
            grid=(nt,),
            in_specs=[pl.BlockSpec(memory_space=pl.ANY),
                      pl.BlockSpec((d, LANES), cm),
                      pl.BlockSpec((d, LANES), cm),
                      pl.BlockSpec((1, LANES), cm),
                      pl.BlockSpec((1, d), cm),
                      pl.BlockSpec((1, d), cm),
                      pl.BlockSpec((1, 1, d, ff), lo3), pl.BlockSpec((1, 1, d, ff), lo3),
                      pl.BlockSpec((1, 1, ff, d), lo3),
                      pl.BlockSpec((1, 1, d, ff), hi3), pl.BlockSpec((1, 1, d, ff), hi3),
                      pl.BlockSpec((1, 1, ff, d), hi3)],
            out_specs=pl.BlockSpec(memory_space=pl.ANY),
            scratch_shapes=[pltpu.VMEM((2, tm, d), F32), pltpu.VMEM((2, tm, d), F32),
                            pltpu.SemaphoreType.DMA((2,)), pltpu.SemaphoreType.DMA((2,))],
        ),
        out_shape=jax.ShapeDtypeStruct((t, d), F32),
        compiler_params=_cparams("arbitrary"),
        name="moe_final" if final else "moe",
    )(src, nvalid, elo, ehi, x2, wr_hi, wr_lo, rbias, g, gfin, wg, wu, wd, wg, wu, wd)


def _invert_body(dest_ref, src_ref):
    def clear(j, c):
        src_ref[j] = 0
        return c

    lax.fori_loop(0, src_ref.shape[0], clear, 0, unroll=8)

    def place(r, c):
        src_ref[dest_ref[r]] = r
        return c

    lax.fori_loop(0, dest_ref.shape[0], place, 0, unroll=8)


def _invert(dest, n):
    return pl.pallas_call(
        _invert_body,
        in_specs=[pl.BlockSpec(memory_space=pltpu.SMEM)],
        out_specs=pl.BlockSpec(memory_space=pltpu.SMEM),
        out_shape=jax.ShapeDtypeStruct((n,), jnp.int32),
        name="invert",
    )(dest)


def _moe_layer(x2, g, gfin, w_coarse, b_coarse, w_fine, b_fine, wg, wu, wd, layer, final, out_order):
    t, d = x2.shape
    wr = jnp.zeros((d, LANES), F32).at[:, :MOE_GROUPS].set(w_coarse)
    wr = wr.at[:, MOE_GROUPS:MOE_GROUPS + w_fine.shape[1]].set(w_fine)
    wr_hi = wr.astype(BF16)
    wr_lo = (wr - wr_hi.astype(F32)).astype(BF16)
    bias = jnp.zeros((1, LANES), F32).at[0, :MOE_GROUPS].set(b_coarse)
    bias = bias.at[0, MOE_GROUPS:MOE_GROUPS + b_fine.shape[0]].set(b_fine)
    rw, cnt = _router(x2, g, wr_hi, wr_lo, bias)

    tm = _tile(t, 256)
    nt = t // tm + N_CLASSES
    i32 = jnp.int32
    cls = rw[0].astype(i32)
    rank = rw[1].astype(i32)
    counts = cnt[0, :N_CLASSES].astype(i32)
    tiles_per = (counts + tm - 1) // tm
    tile_end = jnp.cumsum(tiles_per)
    tile_start = tile_end - tiles_per
    class_ids = jnp.arange(N_CLASSES, dtype=i32)
    dest = jnp.sum(jnp.where(cls[:, None] == class_ids[None, :], (tile_start * tm)[None, :], 0), axis=1) + rank
    src = _invert(dest, nt * tm)
    tile_ids = jnp.arange(nt, dtype=i32)
    total = tile_end[-1]
    used = tile_ids < total
    tcls = jnp.sum((jnp.minimum(tile_ids, total - 1)[:, None] >= tile_end[None, :]).astype(i32), axis=1)
    tcls = jnp.clip(tcls, 0, N_CLASSES - 1)
    sel = tcls[:, None] == class_ids[None, :]
    tcount = jnp.sum(jnp.where(sel, counts[None, :], 0), axis=1)
    tstart = jnp.sum(jnp.where(sel, tile_start[None, :], 0), axis=1)
    nvalid = jnp.where(used, jnp.clip(tcount - (tile_ids - tstart) * tm, 0, tm), 0).astype(i32)
    grp, pair = tcls // N_PAIRS, tcls % N_PAIRS
    pair_ids = jnp.arange(N_PAIRS, dtype=i32)
    psel = pair[:, None] == pair_ids[None, :]
    elo = (grp * MOE_PER_GROUP + jnp.sum(jnp.where(psel, jnp.array(PAIR_LO, i32)[None, :], 0), axis=1)).astype(i32)
    ehi = (grp * MOE_PER_GROUP + jnp.sum(jnp.where(psel, jnp.array(PAIR_HI, i32)[None, :], 0), axis=1)).astype(i32)
    return _moe(x2, wr_hi, wr_lo, bias, src, nvalid, elo, ehi, g, gfin, wg, wu, wd, layer, tm, final, out_order)


def _mixer0(x2, bsz, seq, g, w_in, lru_conv_w, lru_conv_b, lru_w_a, lru_b_a, lru_w_x, lru_b_x, lru_lam,
            m_conv_w, m_conv_b, m_i_bias, m_f_bias, m_head_g, w_out):
    d = x2.shape[1]
    c = lru_conv_w.shape[1]
    mw = m_head_g.shape[0]
    assert mw == c and m_conv_w.shape[1] == 2 * c
    nmain = 2 * c + 4 * mw
    w_gate = jnp.zeros((d, LANES), F32).at[:, :2 * MLSTM_HEADS].set(w_in[:, nmain:]).astype(BF16)
    conv_w = jnp.concatenate([lru_conv_w, m_conv_w], axis=1)
    conv_b = jnp.concatenate([lru_conv_b, m_conv_b])[None]
    pc, pb, gates = _in_proj0(x2, g[None], w_in.astype(BF16), w_gate, conv_w, conv_b, seq, c, mw // MLSTM_HEADS)
    ya = _rg_lru(pc, pb, bsz, seq, lru_w_a.astype(BF16), lru_b_a[None], lru_w_x.astype(BF16), lru_b_x[None],
                 lru_lam[None])
    gate_bias = jnp.zeros((1, LANES), F32).at[0, :MLSTM_HEADS].set(m_i_bias)
    gate_bias = gate_bias.at[0, MLSTM_HEADS:2 * MLSTM_HEADS].set(m_f_bias)
    yb = _mlstm(pb, gates, gate_bias, bsz, seq, m_head_g[None], c)
    return _out_proj(ya, yb, w_out.astype(BF16), x2)


def _mixer1(x2, seq, g, w_in, a_re, a_im, log_step, b_re, b_im, c_re, c_im, d_skip, w_glu_v, w_glu_g):
    ops = _s5_operators(a_re, a_im, log_step, b_re, b_im, c_re, c_im, d_skip, seq // S5_L)
    z = _s5_in(x2, g[None], jnp.transpose(w_in).astype(BF16))
    y = _s5_core(z, *ops, seq)
    return _s5_glu(y, jnp.transpose(w_glu_v).astype(BF16), jnp.transpose(w_glu_g).astype(BF16), x2)


def kernel(x, norm_mix, norm_ffn, norm_final, ab_w_in, lru_conv_w, lru_conv_b, lru_w_a, lru_b_a, lru_w_x, lru_b_x,
           lru_lam, m_conv_w, m_conv_b, m_i_bias, m_f_bias, m_head_g, ab_w_out, s5_w_in, s5_a_re, s5_a_im,
           s5_log_step, s5_b_re, s5_b_im, s5_c_re, s5_c_im, s5_d, s5_w_glu_v, s5_w_glu_g, moe_w_coarse,
           moe_b_coarse, moe_w_fine, moe_b_fine, moe_w_gate, moe_w_up, moe_w_down):
    bsz, seq, d = x.shape
    depth = norm_mix.shape[0]
    x2 = x.reshape(bsz * seq, d)
    gfin = norm_final[None]
    wg16, wu16, wd16 = moe_w_gate.astype(BF16), moe_w_up.astype(BF16), moe_w_down.astype(BF16)
    for layer in range(depth):
        j = layer // 2
        last = layer == depth - 1
        if layer % 2 == 0:
            x2 = _mixer0(x2, bsz, seq, norm_mix[layer], ab_w_in[j], lru_conv_w[j], lru_conv_b[j], lru_w_a[j],
                         lru_b_a[j], lru_w_x[j], lru_b_x[j], lru_lam[j], m_conv_w[j], m_conv_b[j], m_i_bias[j],
                         m_f_bias[j], m_head_g[j], ab_w_out[j])
            out_order = "same" if last else "fold16"
        else:
            x2 = _mixer1(x2, seq, norm_mix[layer], s5_w_in[j], s5_a_re[j], s5_a_im[j], s5_log_step[j], s5_b_re[j],
                         s5_b_im[j], s5_c_re[j], s5_c_im[j], s5_d[j], s5_w_glu_v[j], s5_w_glu_g[j])
            out_order = "unfold16"
        x2 = _moe_layer(x2, norm_ffn[layer][None], gfin, moe_w_coarse[layer], moe_b_coarse[layer],
                        moe_w_fine[layer], moe_b_fine[layer], wg16, wu16, wd16, layer, last, out_order)
    return x2.reshape(bsz, seq, d)
```

```python
import functools
import math

import jax
import jax.numpy as jnp
from jax import lax
from jax.experimental import pallas as pl
from jax.experimental.pallas import tpu as pltpu

F32 = jnp.float32
BF16 = jnp.bfloat16
EPS = 1e-6
LANES = 128
VMEM_LIMIT = 56 * 1024 * 1024

CONV_K = 4
LRU_BLOCKS = 8
LRU_C = 8.0
MLSTM_HEADS = 4
S5_GROUP = 16
S5_STATE = 64
S5_L = 16
MOE_GROUPS = 4
MOE_PER_GROUP = 4
N_PAIRS = 6
N_CLASSES = MOE_GROUPS * N_PAIRS
PAIR_LO = (0, 0, 0, 1, 1, 2)
PAIR_HI = (1, 2, 3, 2, 3, 3)


def _tile(n, pref):
    return pref if n % pref == 0 else n


def _cparams(*sem):
    return pltpu.CompilerParams(dimension_semantics=sem, vmem_limit_bytes=VMEM_LIMIT)


def _rms(x, g):
    return x * lax.rsqrt(jnp.mean(x * x, axis=-1, keepdims=True) + EPS) * g


def _dot(a, b):
    return jnp.dot(a, b, preferred_element_type=F32)


def _dot_nt(a, b):
    return lax.dot_general(a, b, (((1,), (1,)), ((), ())), preferred_element_type=F32)


def _dot_tn(a, b):
    return lax.dot_general(a, b, (((0,), (0,)), ((), ())), preferred_element_type=F32)


def _softplus(y):
    return jnp.maximum(y, 0.0) + jnp.log1p(jnp.exp(-jnp.abs(y)))


def _split3(x):
    p1 = x.astype(BF16)
    r1 = x - p1.astype(F32)
    p2 = r1.astype(BF16)
    p3 = (r1 - p2.astype(F32)).astype(BF16)
    return p1, p2, p3


def _rows_to_slabs(y):
    k = y.shape[1] // LANES
    return jnp.swapaxes(jnp.stack([y[:, c * LANES:(c + 1) * LANES] for c in range(k)], axis=0), 0, 1)


def _slabs_to_rows(x3):
    xs = jnp.swapaxes(x3, 0, 1)
    return jnp.concatenate([xs[c] for c in range(x3.shape[1])], axis=1)


def _causal_conv(x, tail, w, b):
    n = x.shape[0]
    xp = jnp.concatenate([tail, x], axis=0)
    out = b + w[CONV_K - 1:CONV_K, :] * x
    for j in range(CONV_K - 1):
        off = 8 - (CONV_K - 1) + j
        out = out + w[j:j + 1, :] * xp[off:off + n, :]
    return out


def _in0_body(x_ref, g_ref, w_ref, wg_ref, cw_ref, cb_ref, pc_ref, pb_ref, gate_ref, tail_ref, *, seq_tiles, dh):
    tm, c = pc_ref.shape

    @pl.when(pl.program_id(0) == 0)
    def _():
        tail_ref[...] = jnp.zeros_like(tail_ref)

    hb = _rms(x_ref[...], g_ref[...]).astype(BF16)
    gate_ref[...] = _dot(hb, wg_ref[...])
    seq_start = (pl.program_id(0) % seq_tiles) == 0

    def proj(jt):
        return _dot(hb, w_ref[:, jt * c:(jt + 1) * c])

    def conv(k, raw):
        tail = jnp.where(seq_start, 0.0, tail_ref[k])
        tail_ref[k] = raw[tm - 8:tm, :]
        return _causal_conv(raw, tail, cw_ref[:, k * c:(k + 1) * c], cb_ref[:, k * c:(k + 1) * c])

    pc_ref[...] = conv(0, proj(0))
    pb_ref[:, 0:c] = jax.nn.gelu(proj(1)).astype(BF16)
    q = conv(1, proj(2))
    pb_ref[:, c:2 * c] = (q * jax.nn.sigmoid(q)).astype(BF16)
    k = conv(2, proj(3))
    pb_ref[:, 2 * c:3 * c] = (k * jax.nn.sigmoid(k) * (dh ** -0.5)).astype(BF16)
    pb_ref[:, 3 * c:4 * c] = proj(4).astype(BF16)
    pb_ref[:, 4 * c:5 * c] = jax.nn.sigmoid(proj(5)).astype(BF16)


def _in_proj0(x2, g, w, w_gate, conv_w, conv_b, seq, c, dh):
    t, d = x2.shape
    tm = _tile(seq, 512)
    assert seq % tm == 0
    return pl.pallas_call(
        functools.partial(_in0_body, seq_tiles=seq // tm, dh=dh),
        grid=(t // tm,),
        in_specs=[pl.BlockSpec((tm, d), lambda i: (i, 0)),
                  pl.BlockSpec((1, d), lambda i: (0, 0)),
                  pl.BlockSpec(w.shape, lambda i: (0, 0), pipeline_mode=pl.Buffered(1)),
                  pl.BlockSpec((d, LANES), lambda i: (0, 0)),
                  pl.BlockSpec(conv_w.shape, lambda i: (0, 0)),
                  pl.BlockSpec(conv_b.shape, lambda i: (0, 0))],
        out_specs=[pl.BlockSpec((tm, c), lambda i: (i, 0)),
                   pl.BlockSpec((tm, 5 * c), lambda i: (i, 0)),
                   pl.BlockSpec((tm, LANES), lambda i: (i, 0))],
        out_shape=[jax.ShapeDtypeStruct((t, c), F32), jax.ShapeDtypeStruct((t, 5 * c), BF16),
                   jax.ShapeDtypeStruct((t, LANES), F32)],
        scratch_shapes=[pltpu.VMEM((3, 8, c), F32)],
        compiler_params=_cparams("arbitrary"),
        name="in_proj0",
    )(x2, g, w, w_gate, conv_w, conv_b)


def _lru_body(x_ref, gz_ref, wa_ref, ba_ref, wx_ref, bx_ref, lam_ref, y_ref, h_ref, a_s, g_s):
    ts, c = x_ref.shape
    blk = c // LRU_BLOCKS

    @pl.when(pl.program_id(1) == 0)
    def _():
        h_ref[...] = jnp.zeros_like(h_ref)

    conv = x_ref[...]
    cb16 = conv.astype(BF16)
    ra = jnp.concatenate([_dot(cb16[:, n * blk:(n + 1) * blk], wa_ref[n]) for n in range(LRU_BLOCKS)], axis=1)
    rx = jnp.concatenate([_dot(cb16[:, n * blk:(n + 1) * blk], wx_ref[n]) for n in range(LRU_BLOCKS)], axis=1)
    r = jax.nn.sigmoid(ra + ba_ref[...])
    ig = jax.nn.sigmoid(rx + bx_ref[...])
    log_a = -LRU_C * r * _softplus(-lam_ref[...])
    a = jnp.exp(log_a)
    pa = a
    pg = jnp.sqrt(1.0 - a * a) * (ig * conv)
    row8 = lax.broadcasted_iota(jnp.int32, (ts, c), 0) & 7
    for s in (1, 2, 4):
        keep = row8 >= s
        ga = jnp.where(keep, pltpu.roll(pg, s, axis=0), 0.0)
        aa = jnp.where(keep, pltpu.roll(pa, s, axis=0), 1.0)
        pg = pa * ga + pg
        pa = pa * aa
    a_s[...] = pa
    g_s[...] = pg

    def group(k, h):
        r0 = pl.multiple_of(k * 8, 8)
        h8 = a_s[pl.ds(r0, 8), :] * h + g_s[pl.ds(r0, 8), :]
        g_s[pl.ds(r0, 8), :] = h8
        return h8[7:8, :]

    h_ref[...] = lax.fori_loop(0, ts // 8, group, h_ref[...], unroll=4)
    y_ref[...] = (g_s[...] * gz_ref[...].astype(F32)).astype(y_ref.dtype)


def _rg_lru(pc, pb, bsz, seq, wa, ba, wx, bx, lam):
    c = pc.shape[1]
    ts = _tile(seq, 512)
    ns = seq // ts
    row = lambda b, s: (0, 0)
    return pl.pallas_call(
        _lru_body,
        grid=(bsz, ns),
        in_specs=[pl.BlockSpec((ts, c), lambda b, s: (b * ns + s, 0)),
                  pl.BlockSpec((ts, c), lambda b, s: (b * ns + s, 0)),
                  pl.BlockSpec(wa.shape, lambda b, s: (0, 0, 0)),
                  pl.BlockSpec((1, c), row),
                  pl.BlockSpec(wx.shape, lambda b, s: (0, 0, 0)),
                  pl.BlockSpec((1, c), row),
                  pl.BlockSpec((1, c), row)],
        out_specs=pl.BlockSpec((ts, c), lambda b, s: (b * ns + s, 0)),
        out_shape=jax.ShapeDtypeStruct((bsz * seq, c), BF16),
        scratch_shapes=[pltpu.VMEM((1, c), F32), pltpu.VMEM((ts, c), F32), pltpu.VMEM((ts, c), F32)],
        compiler_params=_cparams("parallel", "arbitrary"),
        name="rg_lru",
    )(pc, pb, wa, ba, wx, bx, lam)


def _mlstm_body(q_ref, k_ref, v_ref, so_ref, gt_ref, gb_ref, hg_ref, tri_ref, y_ref, ct_ref, n_ref, m_ref):
    L = q_ref.shape[0]
    nh = ct_ref.shape[0]
    dh = ct_ref.shape[1]

    @pl.when(pl.program_id(1) == 0)
    def _():
        ct_ref[...] = jnp.zeros_like(ct_ref)
        n_ref[...] = jnp.zeros_like(n_ref)
        m_ref[...] = jnp.zeros_like(m_ref)

    gts = gt_ref[...] + gb_ref[...]
    lane = lax.broadcasted_iota(jnp.int32, gts.shape, 1)
    lsig = jnp.minimum(gts, 0.0) - jnp.log1p(jnp.exp(-jnp.abs(gts)))
    tri = tri_ref[...]
    p1, p2, p3 = _split3(lsig)
    bc_all = _dot(tri, p1) + _dot(tri, p2) + _dot(tri, p3)
    rows = lax.broadcasted_iota(jnp.int32, (L, L), 0)
    cols = lax.broadcasted_iota(jnp.int32, (L, L), 1)
    causal = rows >= cols

    for head in range(nh):
        hs = slice(head * dh, (head + 1) * dh)
        i_col = jnp.sum(jnp.where(lane == head, gts, 0.0), axis=-1, keepdims=True)
        bcum = jnp.sum(jnp.where(lane == head + nh, bc_all, 0.0), axis=-1, keepdims=True)
        a_col = i_col - bcum
        a_row = jnp.transpose(jnp.broadcast_to(a_col, (L, LANES)))[0:1, :]
        m_st = m_ref[head, 0:1, 0:1]

        d = jnp.where(causal, bcum + a_row, -jnp.inf)
        g = bcum + m_st
        m_t = jnp.maximum(g, jnp.max(d, axis=-1, keepdims=True))
        w_intra = jnp.exp(d - m_t)
        w_inter = jnp.exp(g - m_t)

        qb = q_ref[:, hs]
        kb = k_ref[:, hs]
        vb = v_ref[:, hs]
        s_qk = _dot_nt(qb, kb) * w_intra
        ct = ct_ref[head]
        n_row = n_ref[head]
        num = _dot(s_qk.astype(BF16), vb) + w_inter * _dot(qb, ct.astype(BF16))
        den = (jnp.sum(s_qk, axis=-1, keepdims=True)
               + w_inter * jnp.sum(qb.astype(F32) * n_row, axis=-1, keepdims=True))
        hcell = num / jnp.maximum(jnp.abs(den), jnp.exp(-m_t))

        b_last = bcum[L - 1:L, :]
        d_end = b_last + a_col
        m_new = jnp.maximum(b_last + m_st, jnp.max(d_end, axis=0, keepdims=True))
        w_end = jnp.exp(d_end - m_new)
        decay = jnp.exp(b_last + m_st - m_new)
        kw = kb.astype(F32) * w_end
        ct_ref[head] = decay * ct + _dot_tn(kw.astype(BF16), vb)
        n_ref[head] = decay * n_row + jnp.sum(kw, axis=0, keepdims=True)
        m_ref[head] = jnp.broadcast_to(m_new, m_ref.shape[1:])

        hm = so_ref[:, hs].astype(F32) * hcell
        hm = hm * lax.rsqrt(jnp.mean(hm * hm, axis=-1, keepdims=True) + EPS)
        y_ref[:, hs] = (hm * hg_ref[:, hs]).astype(y_ref.dtype)


def _mlstm(pb, gates, gate_bias, bsz, seq, head_g, col0):
    nh = MLSTM_HEADS
    w = head_g.shape[1]
    dh = w // nh
    L = _tile(seq, 256)
    nc = seq // L
    cq = col0 // w
    tri = jnp.tril(jnp.ones((L, L), BF16))

    def pblock(off):
        return pl.BlockSpec((L, w), lambda b, c: (b * nc + c, off))

    return pl.pallas_call(
        _mlstm_body,
        grid=(bsz, nc),
        in_specs=[pblock(cq), pblock(cq + 1), pblock(cq + 2), pblock(cq + 3),
                  pl.BlockSpec((L, LANES), lambda b, c: (b * nc + c, 0)),
                  pl.BlockSpec((1, LANES), lambda b, c: (0, 0)),
                  pl.BlockSpec((1, w), lambda b, c: (0, 0)),
                  pl.BlockSpec((L, L), lambda b, c: (0, 0))],
        out_specs=pl.BlockSpec((L, w), lambda b, c: (b * nc + c, 0)),
        out_shape=jax.ShapeDtypeStruct((bsz * seq, w), BF16),
        scratch_shapes=[pltpu.VMEM((nh, dh, dh), F32), pltpu.VMEM((nh, 1, dh), F32),
                        pltpu.VMEM((nh, 8, LANES), F32)],
        compiler_params=_cparams("parallel", "arbitrary"),
        name="mlstm",
    )(pb, pb, pb, pb, gates, gate_bias, head_g, tri)


def _outproj_body(ya_ref, yb_ref, wa_ref, wb_ref, x_ref, o_ref, os_ref):
    res = x_ref[...] + _dot(ya_ref[...], wa_ref[...]) + _dot(yb_ref[...], wb_ref[...])
    o_ref[...] = res
    os_ref[...] = _rows_to_slabs(res)


def _out_proj(ya, yb, w, x2):
    t, d = x2.shape
    kh = ya.shape[1]
    assert yb.shape[1] == kh and w.shape[0] == 2 * kh
    tm, tn = _tile(t, 1024), _tile(d, 1024)
    return pl.pallas_call(
        _outproj_body,
        grid=(t // tm, d // tn),
        in_specs=[pl.BlockSpec((tm, kh), lambda i, j: (i, 0)),
                  pl.BlockSpec((tm, kh), lambda i, j: (i, 0)),
                  pl.BlockSpec((kh, tn), lambda i, j: (0, j)),
                  pl.BlockSpec((kh, tn), lambda i, j: (1, j)),
                  pl.BlockSpec((tm, tn), lambda i, j: (i, j))],
        out_specs=[pl.BlockSpec((tm, tn), lambda i, j: (i, j)),
                   pl.BlockSpec((tm, tn // LANES, LANES), lambda i, j: (i, j, 0))],
        out_shape=[jax.ShapeDtypeStruct((t, d), F32), jax.ShapeDtypeStruct((t, d // LANES, LANES), F32)],
        compiler_params=_cparams("parallel", "arbitrary"),
        name="out_proj",
    )(ya, yb, w, w, x2)


def _s5in_body(x_ref, g_ref, wt_ref, z_ref):
    hb = _rms(x_ref[...], g_ref[...]).astype(BF16)
    res = _dot_nt(wt_ref[...], hb)
    z_ref[...] = res.reshape(z_ref.shape)


def _s5_in(xl, g, wt):
    t, d = xl.shape
    ch = wt.shape[0]
    ngrp = ch // S5_GROUP
    nr = t // S5_L
    r = _tile(nr, 512)
    nri = nr // r
    return pl.pallas_call(
        _s5in_body,
        grid=(nri, S5_L),
        in_specs=[pl.BlockSpec((r, d), lambda i, l: (l * nri + i, 0)),
                  pl.BlockSpec((1, d), lambda i, l: (0, 0)),
                  pl.BlockSpec((ch, d), lambda i, l: (0, 0))],
        out_specs=pl.BlockSpec((ngrp, S5_GROUP, r), lambda i, l: (0, l, i)),
        out_shape=jax.ShapeDtypeStruct((ngrp, S5_L * S5_GROUP, nr), F32),
        compiler_params=_cparams("parallel", "arbitrary"),
        name="s5_in",
    )(xl, g, wt)


def _s5core_body(z_ref, min_ref, mintra_ref, cst_ref, lp_ref, y_ref, *, nseg, nlev):
    p = S5_STATE
    zb = z_ref[0].astype(BF16)
    e = _dot(min_ref[0], zb)
    sr, si = e[:p, :], e[p:, :]
    nr = sr.shape[1]
    lane = lax.broadcasted_iota(jnp.int32, (1, nr), 1) % nseg
    lp = lp_ref[0]
    for j in range(nlev):
        s = 1 << j
        cr = lp[:, j:j + 1]
        ci = lp[:, nlev + j:nlev + j + 1]
        keep = lane >= s
        tr = jnp.where(keep, pltpu.roll(sr, s, axis=1), 0.0)
        ti = jnp.where(keep, pltpu.roll(si, s, axis=1), 0.0)
        sr, si = sr + cr * tr - ci * ti, si + cr * ti + ci * tr
    keep = lane >= 1
    xr = jnp.where(keep, pltpu.roll(sr, 1, axis=1), 0.0)
    xi = jnp.where(keep, pltpu.roll(si, 1, axis=1), 0.0)
    xb = jnp.concatenate([xr, xi], axis=0).astype(BF16)
    y = _dot(mintra_ref[0], zb) + _dot(cst_ref[0], xb)
    y_ref[0] = jax.nn.gelu(y).astype(y_ref.dtype)


def _s5_core(z, m_in, m_intra, c_st, lpow, seq):
    ngrp, kk, nr = z.shape
    nseg = seq // S5_L
    nlev = lpow.shape[2] // 2
    return pl.pallas_call(
        functools.partial(_s5core_body, nseg=nseg, nlev=nlev),
        grid=(ngrp,),
        in_specs=[pl.BlockSpec((1, kk, nr), lambda g: (g, 0, 0)),
                  pl.BlockSpec((1,) + m_in.shape[1:], lambda g: (g, 0, 0)),
                  pl.BlockSpec((1,) + m_intra.shape[1:], lambda g: (g, 0, 0)),
                  pl.BlockSpec((1,) + c_st.shape[1:], lambda g: (g, 0, 0)),
                  pl.BlockSpec((1,) + lpow.shape[1:], lambda g: (g, 0, 0))],
        out_specs=pl.BlockSpec((1, kk, nr), lambda g: (g, 0, 0)),
        out_shape=jax.ShapeDtypeStruct((ngrp, kk, nr), BF16),
        compiler_params=_cparams("parallel"),
        name="s5_core",
    )(z, m_in, m_intra, c_st, lpow)


def _glu_body(y_ref, wv_ref, wg_ref, x_ref, o_ref, os_ref):
    ngrp, grp, r = y_ref.shape
    y = y_ref[...].reshape(ngrp * grp, r)
    v = _dot(wv_ref[...], y)
    g = _dot(wg_ref[...], y)
    res = x_ref[...] + jnp.transpose(v * jax.nn.sigmoid(g))
    o_ref[...] = res
    os_ref[...] = _rows_to_slabs(res)


def _s5_glu(y, wvt, wgt, xl):
    t, d = xl.shape
    ngrp, _, nr = y.shape
    r = _tile(nr, 512)
    nri = nr // r
    tn = _tile(d, 1024)
    return pl.pallas_call(
        _glu_body,
        grid=(nri, S5_L, d // tn),
        in_specs=[pl.BlockSpec((ngrp, S5_GROUP, r), lambda i, l, j: (0, l, i)),
                  pl.BlockSpec((tn, ngrp * S5_GROUP), lambda i, l, j: (j, 0)),
                  pl.BlockSpec((tn, ngrp * S5_GROUP), lambda i, l, j: (j, 0)),
                  pl.BlockSpec((r, tn), lambda i, l, j: (l * nri + i, j))],
        out_specs=[pl.BlockSpec((r, tn), lambda i, l, j: (l * nri + i, j)),
                   pl.BlockSpec((r, tn // LANES, LANES), lambda i, l, j: (l * nri + i, j, 0))],
        out_shape=[jax.ShapeDtypeStruct((t, d), F32), jax.ShapeDtypeStruct((t, d // LANES, LANES), F32)],
        compiler_params=_cparams("parallel", "arbitrary", "arbitrary"),
        name="s5_glu",
    )(y, wvt, wgt, xl)


def _s5_operators(a_re, a_im, log_step, b_re, b_im, c_re, c_im, d_skip, nseg):
    hi = lax.Precision.HIGHEST
    L, p, grp = S5_L, S5_STATE, S5_GROUP
    ngrp = a_re.shape[0]
    dt = jnp.exp(log_step)[:, None]
    mag = jnp.exp(a_re * dt)
    lr, li = mag * jnp.cos(a_im * dt), mag * jnp.sin(a_im * dt)
    den = a_re * a_re + a_im * a_im
    fr = ((lr - 1.0) * a_re + li * a_im) / den
    fi = (li * a_re - (lr - 1.0) * a_im) / den
    bbr = fr[..., None] * b_re - fi[..., None] * b_im
    bbi = fr[..., None] * b_im + fi[..., None] * b_re
    pr, pi = [jnp.ones_like(lr)], [jnp.zeros_like(li)]
    for _ in range(L):
        pr_n = pr[-1] * lr - pi[-1] * li
        pi_n = pr[-1] * li + pi[-1] * lr
        pr.append(pr_n)
        pi.append(pi_n)
    pwr, pwi = jnp.stack(pr, 1), jnp.stack(pi, 1)
    lbr = pwr[:, :L, :, None] * bbr[:, None] - pwi[:, :L, :, None] * bbi[:, None]
    lbi = pwr[:, :L, :, None] * bbi[:, None] + pwi[:, :L, :, None] * bbr[:, None]
    m_in = jnp.concatenate([jnp.transpose(lbr[:, ::-1], (0, 2, 1, 3)).reshape(ngrp, p, L * grp),
                            jnp.transpose(lbi[:, ::-1], (0, 2, 1, 3)).reshape(ngrp, p, L * grp)], axis=1)
    taps = (jnp.einsum('gcp,gkpd->gkcd', c_re, lbr, precision=hi)
            - jnp.einsum('gcp,gkpd->gkcd', c_im, lbi, precision=hi))
    taps = taps.at[:, 0].add(jax.vmap(jnp.diag)(d_skip.reshape(ngrp, grp)))
    lag = jnp.arange(L)[:, None] - jnp.arange(L)[None, :]
    blocks = jnp.where((lag >= 0)[None, :, :, None, None], taps[:, jnp.clip(lag, 0, L - 1)], 0.0)
    m_intra = jnp.transpose(blocks, (0, 1, 3, 2, 4)).reshape(ngrp, L * grp, L * grp)
    qr, qi = pwr[:, 1:], pwi[:, 1:]
    cs_r = c_re[:, None] * qr[:, :, None, :] - c_im[:, None] * qi[:, :, None, :]
    cs_i = -c_re[:, None] * qi[:, :, None, :] - c_im[:, None] * qr[:, :, None, :]
    c_st = jnp.concatenate([cs_r, cs_i], axis=-1).reshape(ngrp, L * grp, 2 * p)
    nlev = max(1, int(math.log2(nseg)))
    sr_, si_ = [pwr[:, L]], [pwi[:, L]]
    for _ in range(nlev - 1):
        sr_.append(sr_[-1] * sr_[-1] - si_[-1] * si_[-1])
        si_.append(2.0 * sr_[-2] * si_[-1])
    lpow = jnp.stack(sr_ + si_, axis=-1)
    return m_in.astype(BF16), m_intra.astype(BF16), c_st.astype(BF16), lpow


def _router_body(x_ref, g_ref, whi_ref, wlo_ref, b_ref, tri_ref, o_ref, cnt_ref, carry_ref):
    @pl.when(pl.program_id(0) == 0)
    def _():
        carry_ref[...] = jnp.zeros_like(carry_ref)

    h = _rms(x_ref[...], g_ref[...])
    h_hi = h.astype(BF16)
    h_lo = (h - h_hi.astype(F32)).astype(BF16)
    whi = whi_ref[...]
    logits = _dot(h_hi, whi) + _dot(h_lo, whi) + _dot(h_hi, wlo_ref[...]) + b_ref[...]
    lane = lax.broadcasted_iota(jnp.int32, logits.shape, 1).astype(F32)
    neg, big = -jnp.inf, 1e9

    def first_argmax(vals):
        mx = jnp.max(vals, axis=-1, keepdims=True)
        return mx, jnp.min(jnp.where(vals == mx, lane, big), axis=-1, keepdims=True)

    is_c = lane < MOE_GROUPS
    mx, gi = first_argmax(jnp.where(is_c, logits, neg))
    p_g = 1.0 / jnp.sum(jnp.where(is_c, jnp.exp(logits - mx), 0.0), axis=-1, keepdims=True)
    base = MOE_GROUPS + MOE_PER_GROUP * gi
    lf = jnp.where((lane >= base) & (lane < base + MOE_PER_GROUP), logits, neg)
    v1, i1 = first_argmax(lf)
    v2, i2 = first_argmax(jnp.where(lane == i1, neg, lf))
    e = jnp.exp(v2 - v1)
    w1 = p_g / (1.0 + e)
    w2 = p_g * e / (1.0 + e)
    k1, k2 = i1 - base, i2 - base
    first_low = k1 < k2
    lo, hi = jnp.minimum(k1, k2), jnp.maximum(k1, k2)
    w_lo, w_hi = jnp.where(first_low, w1, w2), jnp.where(first_low, w2, w1)
    cls = gi * N_PAIRS + lo * (7.0 - lo) * 0.5 + (hi - lo - 1.0)
    onehot = jnp.where(lane == cls, 1.0, 0.0)
    before = _dot(tri_ref[...], onehot.astype(BF16))
    carry = carry_ref[...]
    rank = jnp.sum(onehot * (before + carry), axis=-1, keepdims=True)
    carry = carry + jnp.sum(onehot, axis=0, keepdims=True)
    carry_ref[...] = carry
    cnt_ref[...] = carry
    out = jnp.where(lane == 0, cls, jnp.where(lane == 1, rank, jnp.where(lane == 2, w_lo,
                    jnp.where(lane == 3, w_hi, 0.0))))
    o_ref[...] = jnp.transpose(out)[0:8, :]


def _router(x2, g, w_hi, w_lo, bias):
    t, d = x2.shape
    tm = _tile(t, 512)
    tri = jnp.tril(jnp.ones((tm, tm), BF16), -1)
    return pl.pallas_call(
        _router_body,
        grid=(t // tm,),
        in_specs=[pl.BlockSpec((tm, d), lambda i: (i, 0)),
                  pl.BlockSpec((1, d), lambda i: (0, 0)),
                  pl.BlockSpec((d, LANES), lambda i: (0, 0)),
                  pl.BlockSpec((d, LANES), lambda i: (0, 0)),
                  pl.BlockSpec((1, LANES), lambda i: (0, 0)),
                  pl.BlockSpec((tm, tm), lambda i: (0, 0))],
        out_specs=[pl.BlockSpec((8, tm), lambda i: (0, i)),
                   pl.BlockSpec((1, LANES), lambda i: (0, 0))],
        out_shape=[jax.ShapeDtypeStruct((8, t), F32), jax.ShapeDtypeStruct((1, LANES), F32)],
        scratch_shapes=[pltpu.VMEM((1, LANES), F32)],
        compiler_params=_cparams("arbitrary"),
        name="router",
    )(x2, g, w_hi, w_lo, bias, tri)


def _moe_body(src_ref, nv_ref, elo_ref, ehi_ref,
              x_hbm, whi_ref, rb_ref, g_ref, gfin_ref, wg_lo, wu_lo, wd_lo, wg_hi, wu_hi, wd_hi,
              o_hbm, xbuf, obuf, sem_in, sem_out, *, final, out_order, nr):
    i = pl.program_id(0)
    nt = pl.num_programs(0)
    tm = xbuf.shape[1]
    slot = i % 2
    nv = nv_ref[i]

    def divmod_nonneg(v, m):
        if m & (m - 1) == 0:
            sh = m.bit_length() - 1
            return lax.shift_right_logical(v, sh), v & (m - 1)
        return v // m, v % m

    def out_row(row):
        if out_order == "fold16":
            q, r = divmod_nonneg(row, S5_L)
            return r * nr + q
        if out_order == "unfold16":
            q, r = divmod_nonneg(row, nr)
            return r * S5_L + q
        return row

    def gather_row(tile, s, r):
        row = src_ref[tile * tm + r]
        return pltpu.make_async_copy(x_hbm.at[pl.ds(row, 1)], xbuf.at[s, pl.ds(r, 1)], sem_in.at[s])

    def wait_gather(s):
        pltpu.make_async_copy(x_hbm.at[pl.ds(0, tm)], xbuf.at[s], sem_in.at[s]).wait()

    def scatter_row(tile, s, r):
        row = out_row(src_ref[tile * tm + r])
        return pltpu.make_async_copy(obuf.at[s, pl.ds(r, 1), :], o_hbm.at[pl.ds(row, 1), :], sem_out.at[s])

    def issue_scatter(tile, s, n):
        for b in range(tm // 8):
            @pl.when(8 * b + 8 <= n)
            def _(b=b):
                for u in range(8):
                    scatter_row(tile, s, 8 * b + u).start(priority=u % 2)

        def single(r, c):
            scatter_row(tile, s, r).start()
            return c

        lax.fori_loop((n // 8) * 8, n, single, 0)

    def wait_scatter(s, n):
        p = 1
        while p <= tm:
            @pl.when((n & p) != 0)
            def _(p=p):
                pltpu.make_async_copy(obuf.at[s, pl.ds(0, p), :], o_hbm.at[pl.ds(0, p), :], sem_out.at[s]).wait()
            p *= 2

    @pl.when((i == 0) & (nv > 0))
    def _():
        def body(r, c):
            gather_row(0, 0, r).start()
            return c

        lax.fori_loop(0, tm, body, 0, unroll=8)

    prev_used = (i > 0) & (nv_ref[jnp.maximum(i - 1, 0)] > 0)

    def tile_step(s):
        wait_gather(s)
        x = _slabs_to_rows(xbuf[s])
        hb = _rms(x, g_ref[...]).astype(BF16)
        nxt = jnp.minimum(i + 1, nt - 1)
        for r in range(tm):
            gather_row(nxt, 1 - s, r).start(priority=1)

        def expert(wg, wu, wd):
            a = _dot(hb, wg[0, 0])
            u = _dot(hb, wu[0, 0])
            he = (a * jax.nn.sigmoid(a) * u).astype(BF16)
            return _dot(he, wd[0, 0])

        logits = _dot(hb, whi_ref[...]) + rb_ref[...]
        lane = lax.broadcasted_iota(jnp.int32, logits.shape, 1)
        e_lo, e_hi = elo_ref[i], ehi_ref[i]
        is_c = lane < MOE_GROUPS
        mx = jnp.max(jnp.where(is_c, logits, -jnp.inf), axis=-1, keepdims=True)
        ex = jnp.where(is_c, jnp.exp(logits - mx), 0.0)
        p_g = (jnp.sum(jnp.where(lane == e_lo // MOE_PER_GROUP, ex, 0.0), axis=-1, keepdims=True)
               / jnp.sum(ex, axis=-1, keepdims=True))
        v_lo = jnp.sum(jnp.where(lane == MOE_GROUPS + e_lo, logits, 0.0), axis=-1, keepdims=True)
        v_hi = jnp.sum(jnp.where(lane == MOE_GROUPS + e_hi, logits, 0.0), axis=-1, keepdims=True)
        w_lo = p_g / (1.0 + jnp.exp(v_hi - v_lo))
        w_hi = p_g / (1.0 + jnp.exp(v_lo - v_hi))
        out = x + w_lo * expert(wg_lo, wu_lo, wd_lo) + w_hi * expert(wg_hi, wu_hi, wd_hi)
        if final:
            out = _rms(out, gfin_ref[...])
        obuf[s] = out
        issue_scatter(i, s, nv)

    for s in (0, 1):
        @pl.when((nv > 0) & (slot == s))
        def _(s=s):
            tile_step(s)

        @pl.when((nv == 0) & prev_used & (slot == s))
        def _(s=s):
            wait_gather(s)

        @pl.when(prev_used & (slot == s))
        def _(s=s):
            wait_scatter(1 - s, nv_ref[jnp.maximum(i - 1, 0)])

        @pl.when((i == nt - 1) & (nv > 0) & (slot == s))
        def _(s=s):
            wait_gather(1 - s)
            wait_scatter(s, nv)


def _moe(xs, wr_hi, rbias, src, nvalid, elo, ehi, g, gfin, wg, wu, wd, layer, tm, final, out_order):
    t, nk, _ = xs.shape
    d = nk * LANES
    nt = nvalid.shape[0]
    ff = wg.shape[3]
    cm = lambda i, *_: (0, 0)
    lo3 = lambda i, src, nv, elo, ehi: (layer, elo[i], 0, 0)
    hi3 = lambda i, src, nv, elo, ehi: (layer, ehi[i], 0, 0)
    return pl.pallas_call(
        functools.partial(_moe_body, final=final, out_order=out_order, nr=t // S5_L),
        grid_spec=pltpu.PrefetchScalarGridSpec(
            num_scalar_prefetch=4,
            grid=(nt,),
            in_specs=[pl.BlockSpec(memory_space=pl.ANY),
                      pl.BlockSpec((d, LANES), cm),
                      pl.BlockSpec((1, LANES), cm),
                      pl.BlockSpec((1, d), cm),
                      pl.BlockSpec((1, d), cm),
                      pl.BlockSpec((1, 1, d, ff), lo3), pl.BlockSpec((1, 1, d, ff), lo3),
                      pl.BlockSpec((1, 1, ff, d), lo3),
                      pl.BlockSpec((1, 1, d, ff), hi3), pl.BlockSpec((1, 1, d, ff), hi3),
                      pl.BlockSpec((1, 1, ff, d), hi3)],
            out_specs=pl.BlockSpec(memory_space=pl.ANY),
            scratch_shapes=[pltpu.VMEM((2, tm, nk, LANES), F32), pltpu.VMEM((2, tm, d), F32),
                            pltpu.SemaphoreType.DMA((2,)), pltpu.SemaphoreType.DMA((2,))],
        ),
        out_shape=jax.ShapeDtypeStruct((t, d), F32),
        compiler_params=_cparams("arbitrary"),
        name="moe_final" if final else "moe",
    )(src, nvalid, elo, ehi, xs, wr_hi, rbias, g, gfin, wg, wu, wd, wg, wu, wd)


def _invert_body(dest_ref, src_ref):
    def clear(j, c):
        src_ref[j] = 0
        return c

    lax.fori_loop(0, src_ref.shape[0], clear, 0, unroll=8)

    def place(r, c):
        src_ref[dest_ref[r]] = r
        return c

    lax.fori_loop(0, dest_ref.shape[0], place, 0, unroll=8)


def _invert(dest, n):
    return pl.pallas_call(
        _invert_body,
        in_specs=[pl.BlockSpec(memory_space=pltpu.SMEM)],
        out_specs=pl.BlockSpec(memory_space=pltpu.SMEM),
        out_shape=jax.ShapeDtypeStruct((n,), jnp.int32),
        name="invert",
    )(dest)


def _moe_layer(x2, xs, g, gfin, w_coarse, b_coarse, w_fine, b_fine, wg, wu, wd, layer, final, out_order):
    t, d = x2.shape
    wr = jnp.zeros((d, LANES), F32).at[:, :MOE_GROUPS].set(w_coarse)
    wr = wr.at[:, MOE_GROUPS:MOE_GROUPS + w_fine.shape[1]].set(w_fine)
    wr_hi = wr.astype(BF16)
    wr_lo = (wr - wr_hi.astype(F32)).astype(BF16)
    bias = jnp.zeros((1, LANES), F32).at[0, :MOE_GROUPS].set(b_coarse)
    bias = bias.at[0, MOE_GROUPS:MOE_GROUPS + b_fine.shape[0]].set(b_fine)
    rw, cnt = _router(x2, g, wr_hi, wr_lo, bias)

    tm = _tile(t, 256)
    nt = t // tm + N_CLASSES
    i32 = jnp.int32
    cls = rw[0].astype(i32)
    rank = rw[1].astype(i32)
    counts = cnt[0, :N_CLASSES].astype(i32)
    tiles_per = (counts + tm - 1) // tm
    tile_end = jnp.cumsum(tiles_per)
    tile_start = tile_end - tiles_per
    class_ids = jnp.arange(N_CLASSES, dtype=i32)
    dest = jnp.sum(jnp.where(cls[:, None] == class_ids[None, :], (tile_start * tm)[None, :], 0), axis=1) + rank
    src = _invert(dest, nt * tm)
    tile_ids = jnp.arange(nt, dtype=i32)
    total = tile_end[-1]
    used = tile_ids < total
    tcls = jnp.sum((jnp.minimum(tile_ids, total - 1)[:, None] >= tile_end[None, :]).astype(i32), axis=1)
    tcls = jnp.clip(tcls, 0, N_CLASSES - 1)
    sel = tcls[:, None] == class_ids[None, :]
    tcount = jnp.sum(jnp.where(sel, counts[None, :], 0), axis=1)
    tstart = jnp.sum(jnp.where(sel, tile_start[None, :], 0), axis=1)
    nvalid = jnp.where(used, jnp.clip(tcount - (tile_ids - tstart) * tm, 0, tm), 0).astype(i32)
    grp, pair = tcls // N_PAIRS, tcls % N_PAIRS
    pair_ids = jnp.arange(N_PAIRS, dtype=i32)
    psel = pair[:, None] == pair_ids[None, :]
    elo = (grp * MOE_PER_GROUP + jnp.sum(jnp.where(psel, jnp.array(PAIR_LO, i32)[None, :], 0), axis=1)).astype(i32)
    ehi = (grp * MOE_PER_GROUP + jnp.sum(jnp.where(psel, jnp.array(PAIR_HI, i32)[None, :], 0), axis=1)).astype(i32)
    return _moe(xs, wr_hi, bias, src, nvalid, elo, ehi, g, gfin, wg, wu, wd, layer, tm, final, out_order)


def _mixer0(x2, bsz, seq, g, w_in, lru_conv_w, lru_conv_b, lru_w_a, lru_b_a, lru_w_x, lru_b_x, lru_lam,
            m_conv_w, m_conv_b, m_i_bias, m_f_bias, m_head_g, w_out):
    d = x2.shape[1]
    c = lru_conv_w.shape[1]
    mw = m_head_g.shape[0]
    assert mw == c and m_conv_w.shape[1] == 2 * c
    nmain = 2 * c + 4 * mw
    w_gate = jnp.zeros((d, LANES), F32).at[:, :2 * MLSTM_HEADS].set(w_in[:, nmain:]).astype(BF16)
    conv_w = jnp.concatenate([lru_conv_w, m_conv_w], axis=1)
    conv_b = jnp.concatenate([lru_conv_b, m_conv_b])[None]
    pc, pb, gates = _in_proj0(x2, g[None], w_in.astype(BF16), w_gate, conv_w, conv_b, seq, c, mw // MLSTM_HEADS)
    ya = _rg_lru(pc, pb, bsz, seq, lru_w_a.astype(BF16), lru_b_a[None], lru_w_x.astype(BF16), lru_b_x[None],
                 lru_lam[None])
    gate_bias = jnp.zeros((1, LANES), F32).at[0, :MLSTM_HEADS].set(m_i_bias)
    gate_bias = gate_bias.at[0, MLSTM_HEADS:2 * MLSTM_HEADS].set(m_f_bias)
    yb = _mlstm(pb, gates, gate_bias, bsz, seq, m_head_g[None], c)
    return _out_proj(ya, yb, w_out.astype(BF16), x2)


def _mixer1(x2, seq, g, w_in, a_re, a_im, log_step, b_re, b_im, c_re, c_im, d_skip, w_glu_v, w_glu_g):
    ops = _s5_operators(a_re, a_im, log_step, b_re, b_im, c_re, c_im, d_skip, seq // S5_L)
    z = _s5_in(x2, g[None], jnp.transpose(w_in).astype(BF16))
    y = _s5_core(z, *ops, seq)
    return _s5_glu(y, jnp.transpose(w_glu_v).astype(BF16), jnp.transpose(w_glu_g).astype(BF16), x2)


def kernel(x, norm_mix, norm_ffn, norm_final, ab_w_in, lru_conv_w, lru_conv_b, lru_w_a, lru_b_a, lru_w_x, lru_b_x,
           lru_lam, m_conv_w, m_conv_b, m_i_bias, m_f_bias, m_head_g, ab_w_out, s5_w_in, s5_a_re, s5_a_im,
           s5_log_step, s5_b_re, s5_b_im, s5_c_re, s5_c_im, s5_d, s5_w_glu_v, s5_w_glu_g, moe_w_coarse,
           moe_b_coarse, moe_w_fine, moe_b_fine, moe_w_gate, moe_w_up, moe_w_down):
    bsz, seq, d = x.shape
    depth = norm_mix.shape[0]
    x2 = x.reshape(bsz * seq, d)
    gfin = norm_final[None]
    wg16, wu16, wd16 = moe_w_gate.astype(BF16), moe_w_up.astype(BF16), moe_w_down.astype(BF16)
    for layer in range(depth):
        j = layer // 2
        last = layer == depth - 1
        if layer % 2 == 0:
            x2, xs = _mixer0(x2, bsz, seq, norm_mix[layer], ab_w_in[j], lru_conv_w[j], lru_conv_b[j], lru_w_a[j],
                         lru_b_a[j], lru_w_x[j], lru_b_x[j], lru_lam[j], m_conv_w[j], m_conv_b[j], m_i_bias[j],
                         m_f_bias[j], m_head_g[j], ab_w_out[j])
            out_order = "same" if last else "fold16"
        else:
            x2, xs = _mixer1(x2, seq, norm_mix[layer], s5_w_in[j], s5_a_re[j], s5_a_im[j], s5_log_step[j], s5_b_re[j],
                         s5_b_im[j], s5_c_re[j], s5_c_im[j], s5_d[j], s5_w_glu_v[j], s5_w_glu_g[j])
            out_order = "unfold16"
        x2 = _moe_layer(x2, xs, norm_ffn[layer][None], gfin, moe_w_coarse[layer], moe_b_coarse[layer],
                        moe_w_fine[layer], moe_b_fine[layer], wg16, wu16, wd16, layer, last, out_order)
    return x2.reshape(bsz, seq, d)
```

```python
import functools
import math

import jax
import jax.numpy as jnp
from jax import lax
from jax.experimental import pallas as pl
from jax.experimental.pallas import tpu as pltpu

F32 = jnp.float32
BF16 = jnp.bfloat16
EPS = 1e-6
LANES = 128
VMEM_LIMIT = 56 * 1024 * 1024

CONV_K = 4
LRU_BLOCKS = 8
LRU_C = 8.0
MLSTM_HEADS = 4
S5_GROUP = 16
S5_STATE = 64
S5_L = 16
MOE_GROUPS = 4
MOE_PER_GROUP = 4
N_PAIRS = 6
N_CLASSES = MOE_GROUPS * N_PAIRS
PAIR_LO = (0, 0, 0, 1, 1, 2)
PAIR_HI = (1, 2, 3, 2, 3, 3)


def _tile(n, pref):
    return pref if n % pref == 0 else n


def _cparams(*sem):
    return pltpu.CompilerParams(dimension_semantics=sem, vmem_limit_bytes=VMEM_LIMIT)


def _rms(x, g):
    return x * lax.rsqrt(jnp.mean(x * x, axis=-1, keepdims=True) + EPS) * g


def _dot(a, b):
    return jnp.dot(a, b, preferred_element_type=F32)


def _dot_nt(a, b):
    return lax.dot_general(a, b, (((1,), (1,)), ((), ())), preferred_element_type=F32)


def _dot_tn(a, b):
    return lax.dot_general(a, b, (((0,), (0,)), ((), ())), preferred_element_type=F32)


def _softplus(y):
    return jnp.maximum(y, 0.0) + jnp.log1p(jnp.exp(-jnp.abs(y)))


def _split3(x):
    p1 = x.astype(BF16)
    r1 = x - p1.astype(F32)
    p2 = r1.astype(BF16)
    p3 = (r1 - p2.astype(F32)).astype(BF16)
    return p1, p2, p3


def _rows_to_slabs(y):
    k = y.shape[1] // LANES
    return jnp.swapaxes(jnp.stack([y[:, c * LANES:(c + 1) * LANES] for c in range(k)], axis=0), 0, 1)


def _slabs_to_rows(x3):
    xs = jnp.swapaxes(x3, 0, 1)
    return jnp.concatenate([xs[c] for c in range(x3.shape[1])], axis=1)


def _causal_conv(x, tail, w, b):
    n = x.shape[0]
    xp = jnp.concatenate([tail, x], axis=0)
    out = b + w[CONV_K - 1:CONV_K, :] * x
    for j in range(CONV_K - 1):
        off = 8 - (CONV_K - 1) + j
        out = out + w[j:j + 1, :] * xp[off:off + n, :]
    return out


def _in0_body(x_ref, g_ref, w_ref, wg_ref, cw_ref, cb_ref, pc_ref, pb_ref, gate_ref, tail_ref, *, seq_tiles, dh):
    tm, c = pc_ref.shape

    @pl.when(pl.program_id(0) == 0)
    def _():
        tail_ref[...] = jnp.zeros_like(tail_ref)

    hb = _rms(x_ref[...], g_ref[...]).astype(BF16)
    gate_ref[...] = _dot(hb, wg_ref[...])
    seq_start = (pl.program_id(0) % seq_tiles) == 0

    def proj(jt):
        return _dot(hb, w_ref[:, jt * c:(jt + 1) * c])

    def conv(k, raw):
        tail = jnp.where(seq_start, 0.0, tail_ref[k])
        tail_ref[k] = raw[tm - 8:tm, :]
        return _causal_conv(raw, tail, cw_ref[:, k * c:(k + 1) * c], cb_ref[:, k * c:(k + 1) * c])

    pc_ref[...] = conv(0, proj(0))
    pb_ref[:, 0:c] = jax.nn.gelu(proj(1)).astype(BF16)
    q = conv(1, proj(2))
    pb_ref[:, c:2 * c] = (q * jax.nn.sigmoid(q)).astype(BF16)
    k = conv(2, proj(3))
    pb_ref[:, 2 * c:3 * c] = (k * jax.nn.sigmoid(k) * (dh ** -0.5)).astype(BF16)
    pb_ref[:, 3 * c:4 * c] = proj(4).astype(BF16)
    pb_ref[:, 4 * c:5 * c] = jax.nn.sigmoid(proj(5)).astype(BF16)


def _in_proj0(x2, g, w, w_gate, conv_w, conv_b, seq, c, dh):
    t, d = x2.shape
    tm = _tile(seq, 512)
    assert seq % tm == 0
    return pl.pallas_call(
        functools.partial(_in0_body, seq_tiles=seq // tm, dh=dh),
        grid=(t // tm,),
        in_specs=[pl.BlockSpec((tm, d), lambda i: (i, 0)),
                  pl.BlockSpec((1, d), lambda i: (0, 0)),
                  pl.BlockSpec(w.shape, lambda i: (0, 0), pipeline_mode=pl.Buffered(1)),
                  pl.BlockSpec((d, LANES), lambda i: (0, 0)),
                  pl.BlockSpec(conv_w.shape, lambda i: (0, 0)),
                  pl.BlockSpec(conv_b.shape, lambda i: (0, 0))],
        out_specs=[pl.BlockSpec((tm, c), lambda i: (i, 0)),
                   pl.BlockSpec((tm, 5 * c), lambda i: (i, 0)),
                   pl.BlockSpec((tm, LANES), lambda i: (i, 0))],
        out_shape=[jax.ShapeDtypeStruct((t, c), F32), jax.ShapeDtypeStruct((t, 5 * c), BF16),
                   jax.ShapeDtypeStruct((t, LANES), F32)],
        scratch_shapes=[pltpu.VMEM((3, 8, c), F32)],
        compiler_params=_cparams("arbitrary"),
        name="in_proj0",
    )(x2, g, w, w_gate, conv_w, conv_b)


def _lru_body(x_ref, gz_ref, wa_ref, ba_ref, wx_ref, bx_ref, lam_ref, y_ref, h_ref, a_s, g_s):
    ts, c = x_ref.shape
    blk = c // LRU_BLOCKS

    @pl.when(pl.program_id(1) == 0)
    def _():
        h_ref[...] = jnp.zeros_like(h_ref)

    conv = x_ref[...]
    cb16 = conv.astype(BF16)
    ra = jnp.concatenate([_dot(cb16[:, n * blk:(n + 1) * blk], wa_ref[n]) for n in range(LRU_BLOCKS)], axis=1)
    rx = jnp.concatenate([_dot(cb16[:, n * blk:(n + 1) * blk], wx_ref[n]) for n in range(LRU_BLOCKS)], axis=1)
    r = jax.nn.sigmoid(ra + ba_ref[...])
    ig = jax.nn.sigmoid(rx + bx_ref[...])
    log_a = -LRU_C * r * _softplus(-lam_ref[...])
    a = jnp.exp(log_a)
    pa = a
    pg = jnp.sqrt(1.0 - a * a) * (ig * conv)
    row8 = lax.broadcasted_iota(jnp.int32, (ts, c), 0) & 7
    for s in (1, 2, 4):
        keep = row8 >= s
        ga = jnp.where(keep, pltpu.roll(pg, s, axis=0), 0.0)
        aa = jnp.where(keep, pltpu.roll(pa, s, axis=0), 1.0)
        pg = pa * ga + pg
        pa = pa * aa
    a_s[...] = pa
    g_s[...] = pg

    def group(k, h):
        r0 = pl.multiple_of(k * 8, 8)
        h8 = a_s[pl.ds(r0, 8), :] * h + g_s[pl.ds(r0, 8), :]
        g_s[pl.ds(r0, 8), :] = h8
        return h8[7:8, :]

    h_ref[...] = lax.fori_loop(0, ts // 8, group, h_ref[...], unroll=4)
    y_ref[...] = (g_s[...] * gz_ref[...].astype(F32)).astype(y_ref.dtype)


def _rg_lru(pc, pb, bsz, seq, wa, ba, wx, bx, lam):
    c = pc.shape[1]
    ts = _tile(seq, 512)
    ns = seq // ts
    row = lambda b, s: (0, 0)
    return pl.pallas_call(
        _lru_body,
        grid=(bsz, ns),
        in_specs=[pl.BlockSpec((ts, c), lambda b, s: (b * ns + s, 0)),
                  pl.BlockSpec((ts, c), lambda b, s: (b * ns + s, 0)),
                  pl.BlockSpec(wa.shape, lambda b, s: (0, 0, 0)),
                  pl.BlockSpec((1, c), row),
                  pl.BlockSpec(wx.shape, lambda b, s: (0, 0, 0)),
                  pl.BlockSpec((1, c), row),
                  pl.BlockSpec((1, c), row)],
        out_specs=pl.BlockSpec((ts, c), lambda b, s: (b * ns + s, 0)),
        out_shape=jax.ShapeDtypeStruct((bsz * seq, c), BF16),
        scratch_shapes=[pltpu.VMEM((1, c), F32), pltpu.VMEM((ts, c), F32), pltpu.VMEM((ts, c), F32)],
        compiler_params=_cparams("parallel", "arbitrary"),
        name="rg_lru",
    )(pc, pb, wa, ba, wx, bx, lam)


def _mlstm_body(q_ref, k_ref, v_ref, so_ref, gt_ref, gb_ref, hg_ref, tri_ref, y_ref, ct_ref, n_ref, m_ref):
    L = q_ref.shape[0]
    nh = ct_ref.shape[0]
    dh = ct_ref.shape[1]

    @pl.when(pl.program_id(1) == 0)
    def _():
        ct_ref[...] = jnp.zeros_like(ct_ref)
        n_ref[...] = jnp.zeros_like(n_ref)
        m_ref[...] = jnp.zeros_like(m_ref)

    gts = gt_ref[...] + gb_ref[...]
    lane = lax.broadcasted_iota(jnp.int32, gts.shape, 1)
    lsig = jnp.minimum(gts, 0.0) - jnp.log1p(jnp.exp(-jnp.abs(gts)))
    tri = tri_ref[...]
    p1, p2, p3 = _split3(lsig)
    bc_all = _dot(tri, p1) + _dot(tri, p2) + _dot(tri, p3)
    rows = lax.broadcasted_iota(jnp.int32, (L, L), 0)
    cols = lax.broadcasted_iota(jnp.int32, (L, L), 1)
    causal = rows >= cols

    for head in range(nh):
        hs = slice(head * dh, (head + 1) * dh)
        i_col = jnp.sum(jnp.where(lane == head, gts, 0.0), axis=-1, keepdims=True)
        bcum = jnp.sum(jnp.where(lane == head + nh, bc_all, 0.0), axis=-1, keepdims=True)
        a_col = i_col - bcum
        a_row = jnp.transpose(jnp.broadcast_to(a_col, (L, LANES)))[0:1, :]
        m_st = m_ref[head, 0:1, 0:1]

        d = jnp.where(causal, bcum + a_row, -jnp.inf)
        g = bcum + m_st
        m_t = jnp.maximum(g, jnp.max(d, axis=-1, keepdims=True))
        w_intra = jnp.exp(d - m_t)
        w_inter = jnp.exp(g - m_t)

        qb = q_ref[:, hs]
        kb = k_ref[:, hs]
        vb = v_ref[:, hs]
        s_qk = _dot_nt(qb, kb) * w_intra
        ct = ct_ref[head]
        n_row = n_ref[head]
        num = _dot(s_qk.astype(BF16), vb) + w_inter * _dot(qb, ct.astype(BF16))
        den = (jnp.sum(s_qk, axis=-1, keepdims=True)
               + w_inter * jnp.sum(qb.astype(F32) * n_row, axis=-1, keepdims=True))
        hcell = num / jnp.maximum(jnp.abs(den), jnp.exp(-m_t))

        b_last = bcum[L - 1:L, :]
        d_end = b_last + a_col
        m_new = jnp.maximum(b_last + m_st, jnp.max(d_end, axis=0, keepdims=True))
        w_end = jnp.exp(d_end - m_new)
        decay = jnp.exp(b_last + m_st - m_new)
        kw = kb.astype(F32) * w_end
        ct_ref[head] = decay * ct + _dot_tn(kw.astype(BF16), vb)
        n_ref[head] = decay * n_row + jnp.sum(kw, axis=0, keepdims=True)
        m_ref[head] = jnp.broadcast_to(m_new, m_ref.shape[1:])

        hm = so_ref[:, hs].astype(F32) * hcell
        hm = hm * lax.rsqrt(jnp.mean(hm * hm, axis=-1, keepdims=True) + EPS)
        y_ref[:, hs] = (hm * hg_ref[:, hs]).astype(y_ref.dtype)


def _mlstm(pb, gates, gate_bias, bsz, seq, head_g, col0):
    nh = MLSTM_HEADS
    w = head_g.shape[1]
    dh = w // nh
    L = _tile(seq, 256)
    nc = seq // L
    cq = col0 // w
    tri = jnp.tril(jnp.ones((L, L), BF16))

    def pblock(off):
        return pl.BlockSpec((L, w), lambda b, c: (b * nc + c, off))

    return pl.pallas_call(
        _mlstm_body,
        grid=(bsz, nc),
        in_specs=[pblock(cq), pblock(cq + 1), pblock(cq + 2), pblock(cq + 3),
                  pl.BlockSpec((L, LANES), lambda b, c: (b * nc + c, 0)),
                  pl.BlockSpec((1, LANES), lambda b, c: (0, 0)),
                  pl.BlockSpec((1, w), lambda b, c: (0, 0)),
                  pl.BlockSpec((L, L), lambda b, c: (0, 0))],
        out_specs=pl.BlockSpec((L, w), lambda b, c: (b * nc + c, 0)),
        out_shape=jax.ShapeDtypeStruct((bsz * seq, w), BF16),
        scratch_shapes=[pltpu.VMEM((nh, dh, dh), F32), pltpu.VMEM((nh, 1, dh), F32),
                        pltpu.VMEM((nh, 8, LANES), F32)],
        compiler_params=_cparams("parallel", "arbitrary"),
        name="mlstm",
    )(pb, pb, pb, pb, gates, gate_bias, head_g, tri)


def _outproj_body(ya_ref, yb_ref, wa_ref, wb_ref, x_ref, o_ref, os_ref):
    res = x_ref[...] + _dot(ya_ref[...], wa_ref[...]) + _dot(yb_ref[...], wb_ref[...])
    o_ref[...] = res
    os_ref[...] = _rows_to_slabs(res)


def _out_proj(ya, yb, w, x2):
    t, d = x2.shape
    kh = ya.shape[1]
    assert yb.shape[1] == kh and w.shape[0] == 2 * kh
    tm, tn = _tile(t, 512), d
    return pl.pallas_call(
        _outproj_body,
        grid=(t // tm, d // tn),
        in_specs=[pl.BlockSpec((tm, kh), lambda i, j: (i, 0)),
                  pl.BlockSpec((tm, kh), lambda i, j: (i, 0)),
                  pl.BlockSpec((kh, tn), lambda i, j: (0, j)),
                  pl.BlockSpec((kh, tn), lambda i, j: (1, j)),
                  pl.BlockSpec((tm, tn), lambda i, j: (i, j))],
        out_specs=[pl.BlockSpec((tm, tn), lambda i, j: (i, j)),
                   pl.BlockSpec((tm, tn // LANES, LANES), lambda i, j: (i, j, 0))],
        out_shape=[jax.ShapeDtypeStruct((t, d), F32), jax.ShapeDtypeStruct((t, d // LANES, LANES), F32)],
        compiler_params=_cparams("parallel", "arbitrary"),
        name="out_proj",
    )(ya, yb, w, w, x2)


def _s5in_body(x_ref, g_ref, wt_ref, z_ref):
    hb = _rms(x_ref[...], g_ref[...]).astype(BF16)
    res = _dot_nt(wt_ref[...], hb)
    z_ref[...] = res.reshape(z_ref.shape)


def _s5_in(xl, g, wt):
    t, d = xl.shape
    ch = wt.shape[0]
    ngrp = ch // S5_GROUP
    nr = t // S5_L
    r = _tile(nr, 512)
    nri = nr // r
    return pl.pallas_call(
        _s5in_body,
        grid=(nri, S5_L),
        in_specs=[pl.BlockSpec((r, d), lambda i, l: (l * nri + i, 0)),
                  pl.BlockSpec((1, d), lambda i, l: (0, 0)),
                  pl.BlockSpec((ch, d), lambda i, l: (0, 0))],
        out_specs=pl.BlockSpec((ngrp, S5_GROUP, r), lambda i, l: (0, l, i)),
        out_shape=jax.ShapeDtypeStruct((ngrp, S5_L * S5_GROUP, nr), F32),
        compiler_params=_cparams("parallel", "arbitrary"),
        name="s5_in",
    )(xl, g, wt)


def _s5core_body(z_ref, min_ref, mintra_ref, cst_ref, lp_ref, y_ref, *, nseg, nlev):
    p = S5_STATE
    zb = z_ref[0].astype(BF16)
    e = _dot(min_ref[0], zb)
    sr, si = e[:p, :], e[p:, :]
    nr = sr.shape[1]
    lane = lax.broadcasted_iota(jnp.int32, (1, nr), 1) % nseg
    lp = lp_ref[0]
    for j in range(nlev):
        s = 1 << j
        cr = lp[:, j:j + 1]
        ci = lp[:, nlev + j:nlev + j + 1]
        keep = lane >= s
        tr = jnp.where(keep, pltpu.roll(sr, s, axis=1), 0.0)
        ti = jnp.where(keep, pltpu.roll(si, s, axis=1), 0.0)
        sr, si = sr + cr * tr - ci * ti, si + cr * ti + ci * tr
    keep = lane >= 1
    xr = jnp.where(keep, pltpu.roll(sr, 1, axis=1), 0.0)
    xi = jnp.where(keep, pltpu.roll(si, 1, axis=1), 0.0)
    xb = jnp.concatenate([xr, xi], axis=0).astype(BF16)
    y = _dot(mintra_ref[0], zb) + _dot(cst_ref[0], xb)
    y_ref[0] = jax.nn.gelu(y).astype(y_ref.dtype)


def _s5_core(z, m_in, m_intra, c_st, lpow, seq):
    ngrp, kk, nr = z.shape
    nseg = seq // S5_L
    nlev = lpow.shape[2] // 2
    return pl.pallas_call(
        functools.partial(_s5core_body, nseg=nseg, nlev=nlev),
        grid=(ngrp,),
        in_specs=[pl.BlockSpec((1, kk, nr), lambda g: (g, 0, 0)),
                  pl.BlockSpec((1,) + m_in.shape[1:], lambda g: (g, 0, 0)),
                  pl.BlockSpec((1,) + m_intra.shape[1:], lambda g: (g, 0, 0)),
                  pl.BlockSpec((1,) + c_st.shape[1:], lambda g: (g, 0, 0)),
                  pl.BlockSpec((1,) + lpow.shape[1:], lambda g: (g, 0, 0))],
        out_specs=pl.BlockSpec((1, kk, nr), lambda g: (g, 0, 0)),
        out_shape=jax.ShapeDtypeStruct((ngrp, kk, nr), BF16),
        compiler_params=_cparams("parallel"),
        name="s5_core",
    )(z, m_in, m_intra, c_st, lpow)


def _glu_body(y_ref, wv_ref, wg_ref, x_ref, o_ref, os_ref):
    ngrp, grp, r = y_ref.shape
    y = y_ref[...].reshape(ngrp * grp, r)
    v = _dot(wv_ref[...], y)
    g = _dot(wg_ref[...], y)
    res = x_ref[...] + jnp.transpose(v * jax.nn.sigmoid(g))
    o_ref[...] = res
    os_ref[...] = _rows_to_slabs(res)


def _s5_glu(y, wvt, wgt, xl):
    t, d = xl.shape
    ngrp, _, nr = y.shape
    r = _tile(nr, 512)
    nri = nr // r
    tn = _tile(d, 1024)
    return pl.pallas_call(
        _glu_body,
        grid=(d // tn, nri, S5_L),
        in_specs=[pl.BlockSpec((ngrp, S5_GROUP, r), lambda j, i, l: (0, l, i)),
                  pl.BlockSpec((tn, ngrp * S5_GROUP), lambda j, i, l: (j, 0)),
                  pl.BlockSpec((tn, ngrp * S5_GROUP), lambda j, i, l: (j, 0)),
                  pl.BlockSpec((r, tn), lambda j, i, l: (l * nri + i, j))],
        out_specs=[pl.BlockSpec((r, tn), lambda j, i, l: (l * nri + i, j)),
                   pl.BlockSpec((r, tn // LANES, LANES), lambda j, i, l: (l * nri + i, j, 0))],
        out_shape=[jax.ShapeDtypeStruct((t, d), F32), jax.ShapeDtypeStruct((t, d // LANES, LANES), F32)],
        compiler_params=_cparams("arbitrary", "arbitrary", "arbitrary"),
        name="s5_glu",
    )(y, wvt, wgt, xl)


def _s5_operators(a_re, a_im, log_step, b_re, b_im, c_re, c_im, d_skip, nseg):
    hi = lax.Precision.HIGHEST
    L, p, grp = S5_L, S5_STATE, S5_GROUP
    ngrp = a_re.shape[0]
    dt = jnp.exp(log_step)[:, None]
    mag = jnp.exp(a_re * dt)
    lr, li = mag * jnp.cos(a_im * dt), mag * jnp.sin(a_im * dt)
    den = a_re * a_re + a_im * a_im
    fr = ((lr - 1.0) * a_re + li * a_im) / den
    fi = (li * a_re - (lr - 1.0) * a_im) / den
    bbr = fr[..., None] * b_re - fi[..., None] * b_im
    bbi = fr[..., None] * b_im + fi[..., None] * b_re
    pr, pi = [jnp.ones_like(lr)], [jnp.zeros_like(li)]
    for _ in range(L):
        pr_n = pr[-1] * lr - pi[-1] * li
        pi_n = pr[-1] * li + pi[-1] * lr
        pr.append(pr_n)
        pi.append(pi_n)
    pwr, pwi = jnp.stack(pr, 1), jnp.stack(pi, 1)
    lbr = pwr[:, :L, :, None] * bbr[:, None] - pwi[:, :L, :, None] * bbi[:, None]
    lbi = pwr[:, :L, :, None] * bbi[:, None] + pwi[:, :L, :, None] * bbr[:, None]
    m_in = jnp.concatenate([jnp.transpose(lbr[:, ::-1], (0, 2, 1, 3)).reshape(ngrp, p, L * grp),
                            jnp.transpose(lbi[:, ::-1], (0, 2, 1, 3)).reshape(ngrp, p, L * grp)], axis=1)
    taps = (jnp.einsum('gcp,gkpd->gkcd', c_re, lbr, precision=hi)
            - jnp.einsum('gcp,gkpd->gkcd', c_im, lbi, precision=hi))
    taps = taps.at[:, 0].add(jax.vmap(jnp.diag)(d_skip.reshape(ngrp, grp)))
    lag = jnp.arange(L)[:, None] - jnp.arange(L)[None, :]
    blocks = jnp.where((lag >= 0)[None, :, :, None, None], taps[:, jnp.clip(lag, 0, L - 1)], 0.0)
    m_intra = jnp.transpose(blocks, (0, 1, 3, 2, 4)).reshape(ngrp, L * grp, L * grp)
    qr, qi = pwr[:, 1:], pwi[:, 1:]
    cs_r = c_re[:, None] * qr[:, :, None, :] - c_im[:, None] * qi[:, :, None, :]
    cs_i = -c_re[:, None] * qi[:, :, None, :] - c_im[:, None] * qr[:, :, None, :]
    c_st = jnp.concatenate([cs_r, cs_i], axis=-1).reshape(ngrp, L * grp, 2 * p)
    nlev = max(1, int(math.log2(nseg)))
    sr_, si_ = [pwr[:, L]], [pwi[:, L]]
    for _ in range(nlev - 1):
        sr_.append(sr_[-1] * sr_[-1] - si_[-1] * si_[-1])
        si_.append(2.0 * sr_[-2] * si_[-1])
    lpow = jnp.stack(sr_ + si_, axis=-1)
    return m_in.astype(BF16), m_intra.astype(BF16), c_st.astype(BF16), lpow


def _router_body(x_ref, g_ref, whi_ref, wlo_ref, b_ref, tri_ref, o_ref, cnt_ref, carry_ref):
    @pl.when(pl.program_id(0) == 0)
    def _():
        carry_ref[...] = jnp.zeros_like(carry_ref)

    h = _rms(x_ref[...], g_ref[...])
    h_hi = h.astype(BF16)
    h_lo = (h - h_hi.astype(F32)).astype(BF16)
    whi = whi_ref[...]
    logits = _dot(h_hi, whi) + _dot(h_lo, whi) + _dot(h_hi, wlo_ref[...]) + b_ref[...]
    lane = lax.broadcasted_iota(jnp.int32, logits.shape, 1).astype(F32)
    neg, big = -jnp.inf, 1e9

    def first_argmax(vals):
        mx = jnp.max(vals, axis=-1, keepdims=True)
        return mx, jnp.min(jnp.where(vals == mx, lane, big), axis=-1, keepdims=True)

    is_c = lane < MOE_GROUPS
    mx, gi = first_argmax(jnp.where(is_c, logits, neg))
    p_g = 1.0 / jnp.sum(jnp.where(is_c, jnp.exp(logits - mx), 0.0), axis=-1, keepdims=True)
    base = MOE_GROUPS + MOE_PER_GROUP * gi
    lf = jnp.where((lane >= base) & (lane < base + MOE_PER_GROUP), logits, neg)
    v1, i1 = first_argmax(lf)
    v2, i2 = first_argmax(jnp.where(lane == i1, neg, lf))
    e = jnp.exp(v2 - v1)
    w1 = p_g / (1.0 + e)
    w2 = p_g * e / (1.0 + e)
    k1, k2 = i1 - base, i2 - base
    first_low = k1 < k2
    lo, hi = jnp.minimum(k1, k2), jnp.maximum(k1, k2)
    w_lo, w_hi = jnp.where(first_low, w1, w2), jnp.where(first_low, w2, w1)
    cls = gi * N_PAIRS + lo * (7.0 - lo) * 0.5 + (hi - lo - 1.0)
    onehot = jnp.where(lane == cls, 1.0, 0.0)
    before = _dot(tri_ref[...], onehot.astype(BF16))
    carry = carry_ref[...]
    rank = jnp.sum(onehot * (before + carry), axis=-1, keepdims=True)
    carry = carry + jnp.sum(onehot, axis=0, keepdims=True)
    carry_ref[...] = carry
    cnt_ref[...] = carry
    out = jnp.where(lane == 0, cls, jnp.where(lane == 1, rank, jnp.where(lane == 2, w_lo,
                    jnp.where(lane == 3, w_hi, 0.0))))
    o_ref[...] = jnp.transpose(out)[0:8, :]


def _router(x2, g, w_hi, w_lo, bias):
    t, d = x2.shape
    tm = _tile(t, 512)
    tri = jnp.tril(jnp.ones((tm, tm), BF16), -1)
    return pl.pallas_call(
        _router_body,
        grid=(t // tm,),
        in_specs=[pl.BlockSpec((tm, d), lambda i: (i, 0)),
                  pl.BlockSpec((1, d), lambda i: (0, 0)),
                  pl.BlockSpec((d, LANES), lambda i: (0, 0)),
                  pl.BlockSpec((d, LANES), lambda i: (0, 0)),
                  pl.BlockSpec((1, LANES), lambda i: (0, 0)),
                  pl.BlockSpec((tm, tm), lambda i: (0, 0))],
        out_specs=[pl.BlockSpec((8, tm), lambda i: (0, i)),
                   pl.BlockSpec((1, LANES), lambda i: (0, 0))],
        out_shape=[jax.ShapeDtypeStruct((8, t), F32), jax.ShapeDtypeStruct((1, LANES), F32)],
        scratch_shapes=[pltpu.VMEM((1, LANES), F32)],
        compiler_params=_cparams("arbitrary"),
        name="router",
    )(x2, g, w_hi, w_lo, bias, tri)


def _moe_body(src_ref, nv_ref, elo_ref, ehi_ref,
              x_hbm, whi_ref, rb_ref, g_ref, gfin_ref, wg_lo, wu_lo, wd_lo, wg_hi, wu_hi, wd_hi,
              o_hbm, xbuf, obuf, sem_in, sem_out, *, final, out_order, nr):
    i = pl.program_id(0)
    nt = pl.num_programs(0)
    tm = xbuf.shape[1]
    slot = i % 2
    nv = nv_ref[i]

    def divmod_nonneg(v, m):
        if m & (m - 1) == 0:
            sh = m.bit_length() - 1
            return lax.shift_right_logical(v, sh), v & (m - 1)
        return v // m, v % m

    def out_row(row):
        if out_order == "fold16":
            q, r = divmod_nonneg(row, S5_L)
            return r * nr + q
        if out_order == "unfold16":
            q, r = divmod_nonneg(row, nr)
            return r * S5_L + q
        return row

    def gather_row(tile, s, r):
        row = src_ref[tile * tm + r]
        return pltpu.make_async_copy(x_hbm.at[pl.ds(row, 1)], xbuf.at[s, pl.ds(r, 1)], sem_in.at[s])

    def wait_gather(s):
        pltpu.make_async_copy(x_hbm.at[pl.ds(0, tm)], xbuf.at[s], sem_in.at[s]).wait()

    def scatter_row(tile, s, r):
        row = out_row(src_ref[tile * tm + r])
        return pltpu.make_async_copy(obuf.at[s, pl.ds(r, 1), :], o_hbm.at[pl.ds(row, 1), :], sem_out.at[s])

    def issue_scatter(tile, s, n):
        for b in range(tm // 8):
            @pl.when(8 * b + 8 <= n)
            def _(b=b):
                for u in range(8):
                    scatter_row(tile, s, 8 * b + u).start(priority=u % 2)

        def single(r, c):
            scatter_row(tile, s, r).start()
            return c

        lax.fori_loop((n // 8) * 8, n, single, 0)

    def wait_scatter(s, n):
        p = 1
        while p <= tm:
            @pl.when((n & p) != 0)
            def _(p=p):
                pltpu.make_async_copy(obuf.at[s, pl.ds(0, p), :], o_hbm.at[pl.ds(0, p), :], sem_out.at[s]).wait()
            p *= 2

    @pl.when((i == 0) & (nv > 0))
    def _():
        def body(r, c):
            gather_row(0, 0, r).start()
            return c

        lax.fori_loop(0, tm, body, 0, unroll=8)

    prev_used = (i > 0) & (nv_ref[jnp.maximum(i - 1, 0)] > 0)

    def tile_step(s):
        nxt = jnp.minimum(i + 1, nt - 1)
        for r in range(tm):
            gather_row(nxt, 1 - s, r).start(priority=1)
        wait_gather(s)
        x = _slabs_to_rows(xbuf[s])
        hb = _rms(x, g_ref[...]).astype(BF16)

        def expert(wg, wu, wd):
            a = _dot(hb, wg[0, 0])
            u = _dot(hb, wu[0, 0])
            he = (a * jax.nn.sigmoid(a) * u).astype(BF16)
            return _dot(he, wd[0, 0])

        logits = _dot(hb, whi_ref[...]) + rb_ref[...]
        lane = lax.broadcasted_iota(jnp.int32, logits.shape, 1)
        e_lo, e_hi = elo_ref[i], ehi_ref[i]
        is_c = lane < MOE_GROUPS
        mx = jnp.max(jnp.where(is_c, logits, -jnp.inf), axis=-1, keepdims=True)
        ex = jnp.where(is_c, jnp.exp(logits - mx), 0.0)
        p_g = (jnp.sum(jnp.where(lane == e_lo // MOE_PER_GROUP, ex, 0.0), axis=-1, keepdims=True)
               / jnp.sum(ex, axis=-1, keepdims=True))
        v_lo = jnp.sum(jnp.where(lane == MOE_GROUPS + e_lo, logits, 0.0), axis=-1, keepdims=True)
        v_hi = jnp.sum(jnp.where(lane == MOE_GROUPS + e_hi, logits, 0.0), axis=-1, keepdims=True)
        w_lo = p_g / (1.0 + jnp.exp(v_hi - v_lo))
        w_hi = p_g / (1.0 + jnp.exp(v_lo - v_hi))
        out = x + w_lo * expert(wg_lo, wu_lo, wd_lo) + w_hi * expert(wg_hi, wu_hi, wd_hi)
        if final:
            out = _rms(out, gfin_ref[...])
        obuf[s] = out
        issue_scatter(i, s, nv)

    for s in (0, 1):
        @pl.when((nv > 0) & (slot == s))
        def _(s=s):
            tile_step(s)

        @pl.when((nv == 0) & prev_used & (slot == s))
        def _(s=s):
            wait_gather(s)

        @pl.when(prev_used & (slot == s))
        def _(s=s):
            wait_scatter(1 - s, nv_ref[jnp.maximum(i - 1, 0)])

        @pl.when((i == nt - 1) & (nv > 0) & (slot == s))
        def _(s=s):
            wait_gather(1 - s)
            wait_scatter(s, nv)


def _moe(xs, wr_hi, rbias, src, nvalid, elo, ehi, g, gfin, wg, wu, wd, layer, tm, final, out_order):
    t, nk, _ = xs.shape
    d = nk * LANES
    nt = nvalid.shape[0]
    ff = wg.shape[3]
    cm = lambda i, *_: (0, 0)
    lo3 = lambda i, src, nv, elo, ehi: (layer, elo[i], 0, 0)
    hi3 = lambda i, src, nv, elo, ehi: (layer, ehi[i], 0, 0)
    return pl.pallas_call(
        functools.partial(_moe_body, final=final, out_order=out_order, nr=t // S5_L),
        grid_spec=pltpu.PrefetchScalarGridSpec(
            num_scalar_prefetch=4,
            grid=(nt,),
            in_specs=[pl.BlockSpec(memory_space=pl.ANY),
                      pl.BlockSpec((d, LANES), cm),
                      pl.BlockSpec((1, LANES), cm),
                      pl.BlockSpec((1, d), cm),
                      pl.BlockSpec((1, d), cm),
                      pl.BlockSpec((1, 1, d, ff), lo3), pl.BlockSpec((1, 1, d, ff), lo3),
                      pl.BlockSpec((1, 1, ff, d), lo3),
                      pl.BlockSpec((1, 1, d, ff), hi3), pl.BlockSpec((1, 1, d, ff), hi3),
                      pl.BlockSpec((1, 1, ff, d), hi3)],
            out_specs=pl.BlockSpec(memory_space=pl.ANY),
            scratch_shapes=[pltpu.VMEM((2, tm, nk, LANES), F32), pltpu.VMEM((2, tm, d), F32),
                            pltpu.SemaphoreType.DMA((2,)), pltpu.SemaphoreType.DMA((2,))],
        ),
        out_shape=jax.ShapeDtypeStruct((t, d), F32),
        compiler_params=_cparams("arbitrary"),
        name="moe_final" if final else "moe",
    )(src, nvalid, elo, ehi, xs, wr_hi, rbias, g, gfin, wg, wu, wd, wg, wu, wd)


def _invert_body(dest_ref, lo_ref, hi_ref, src_ref):
    def place(r, c):
        src_ref[dest_ref[r]] = r
        return c

    lax.fori_loop(0, dest_ref.shape[0], place, 0, unroll=8)

    for k in range(lo_ref.shape[0]):
        def clear(j, c):
            src_ref[j] = 0
            return c

        lax.fori_loop(lo_ref[k], hi_ref[k], clear, 0)


def _invert(dest, pad_lo, pad_hi, n):
    smem = pl.BlockSpec(memory_space=pltpu.SMEM)
    return pl.pallas_call(
        _invert_body,
        in_specs=[smem, smem, smem],
        out_specs=smem,
        out_shape=jax.ShapeDtypeStruct((n,), jnp.int32),
        name="invert",
    )(dest, pad_lo, pad_hi)


def _moe_layer(x2, xs, g, gfin, w_coarse, b_coarse, w_fine, b_fine, wg, wu, wd, layer, final, out_order):
    t, d = x2.shape
    wr = jnp.zeros((d, LANES), F32).at[:, :MOE_GROUPS].set(w_coarse)
    wr = wr.at[:, MOE_GROUPS:MOE_GROUPS + w_fine.shape[1]].set(w_fine)
    wr_hi = wr.astype(BF16)
    wr_lo = (wr - wr_hi.astype(F32)).astype(BF16)
    bias = jnp.zeros((1, LANES), F32).at[0, :MOE_GROUPS].set(b_coarse)
    bias = bias.at[0, MOE_GROUPS:MOE_GROUPS + b_fine.shape[0]].set(b_fine)
    rw, cnt = _router(x2, g, wr_hi, wr_lo, bias)

    tm = _tile(t, 256)
    nt = t // tm + N_CLASSES
    i32 = jnp.int32
    cls = rw[0].astype(i32)
    rank = rw[1].astype(i32)
    counts = cnt[0, :N_CLASSES].astype(i32)
    tiles_per = (counts + tm - 1) // tm
    tile_end = jnp.cumsum(tiles_per)
    tile_start = tile_end - tiles_per
    class_ids = jnp.arange(N_CLASSES, dtype=i32)
    dest = jnp.sum(jnp.where(cls[:, None] == class_ids[None, :], (tile_start * tm)[None, :], 0), axis=1) + rank
    tile_ids = jnp.arange(nt, dtype=i32)
    total = tile_end[-1]
    pad_lo = jnp.concatenate([tile_start * tm + counts, (total * tm)[None]])
    pad_hi = jnp.concatenate([tile_end * tm, jnp.full((1,), nt * tm, i32)])
    src = _invert(dest, pad_lo.astype(i32), pad_hi.astype(i32), nt * tm)
    used = tile_ids < total
    tcls = jnp.sum((jnp.minimum(tile_ids, total - 1)[:, None] >= tile_end[None, :]).astype(i32), axis=1)
    tcls = jnp.clip(tcls, 0, N_CLASSES - 1)
    sel = tcls[:, None] == class_ids[None, :]
    tcount = jnp.sum(jnp.where(sel, counts[None, :], 0), axis=1)
    tstart = jnp.sum(jnp.where(sel, tile_start[None, :], 0), axis=1)
    nvalid = jnp.where(used, jnp.clip(tcount - (tile_ids - tstart) * tm, 0, tm), 0).astype(i32)
    grp, pair = tcls // N_PAIRS, tcls % N_PAIRS
    pair_ids = jnp.arange(N_PAIRS, dtype=i32)
    psel = pair[:, None] == pair_ids[None, :]
    elo = (grp * MOE_PER_GROUP + jnp.sum(jnp.where(psel, jnp.array(PAIR_LO, i32)[None, :], 0), axis=1)).astype(i32)
    ehi = (grp * MOE_PER_GROUP + jnp.sum(jnp.where(psel, jnp.array(PAIR_HI, i32)[None, :], 0), axis=1)).astype(i32)
    return _moe(xs, wr_hi, bias, src, nvalid, elo, ehi, g, gfin, wg, wu, wd, layer, tm, final, out_order)


def _mixer0(x2, bsz, seq, g, w_in, lru_conv_w, lru_conv_b, lru_w_a, lru_b_a, lru_w_x, lru_b_x, lru_lam,
            m_conv_w, m_conv_b, m_i_bias, m_f_bias, m_head_g, w_out):
    d = x2.shape[1]
    c = lru_conv_w.shape[1]
    mw = m_head_g.shape[0]
    assert mw == c and m_conv_w.shape[1] == 2 * c
    nmain = 2 * c + 4 * mw
    w_gate = jnp.zeros((d, LANES), F32).at[:, :2 * MLSTM_HEADS].set(w_in[:, nmain:]).astype(BF16)
    conv_w = jnp.concatenate([lru_conv_w, m_conv_w], axis=1)
    conv_b = jnp.concatenate([lru_conv_b, m_conv_b])[None]
    pc, pb, gates = _in_proj0(x2, g[None], w_in.astype(BF16), w_gate, conv_w, conv_b, seq, c, mw // MLSTM_HEADS)
    ya = _rg_lru(pc, pb, bsz, seq, lru_w_a.astype(BF16), lru_b_a[None], lru_w_x.astype(BF16), lru_b_x[None],
                 lru_lam[None])
    gate_bias = jnp.zeros((1, LANES), F32).at[0, :MLSTM_HEADS].set(m_i_bias)
    gate_bias = gate_bias.at[0, MLSTM_HEADS:2 * MLSTM_HEADS].set(m_f_bias)
    yb = _mlstm(pb, gates, gate_bias, bsz, seq, m_head_g[None], c)
    return _out_proj(ya, yb, w_out.astype(BF16), x2)


def _mixer1(x2, seq, g, w_in, a_re, a_im, log_step, b_re, b_im, c_re, c_im, d_skip, w_glu_v, w_glu_g):
    ops = _s5_operators(a_re, a_im, log_step, b_re, b_im, c_re, c_im, d_skip, seq // S5_L)
    z = _s5_in(x2, g[None], jnp.transpose(w_in).astype(BF16))
    y = _s5_core(z, *ops, seq)
    return _s5_glu(y, jnp.transpose(w_glu_v).astype(BF16), jnp.transpose(w_glu_g).astype(BF16), x2)


def kernel(x, norm_mix, norm_ffn, norm_final, ab_w_in, lru_conv_w, lru_conv_b, lru_w_a, lru_b_a, lru_w_x, lru_b_x,
           lru_lam, m_conv_w, m_conv_b, m_i_bias, m_f_bias, m_head_g, ab_w_out, s5_w_in, s5_a_re, s5_a_im,
           s5_log_step, s5_b_re, s5_b_im, s5_c_re, s5_c_im, s5_d, s5_w_glu_v, s5_w_glu_g, moe_w_coarse,
           moe_b_coarse, moe_w_fine, moe_b_fine, moe_w_gate, moe_w_up, moe_w_down):
    bsz, seq, d = x.shape
    depth = norm_mix.shape[0]
    x2 = x.reshape(bsz * seq, d)
    gfin = norm_final[None]
    wg16, wu16, wd16 = moe_w_gate.astype(BF16), moe_w_up.astype(BF16), moe_w_down.astype(BF16)
    for layer in range(depth):
        j = layer // 2
        last = layer == depth - 1
        if layer % 2 == 0:
            x2, xs = _mixer0(x2, bsz, seq, norm_mix[layer], ab_w_in[j], lru_conv_w[j], lru_conv_b[j], lru_w_a[j],
                         lru_b_a[j], lru_w_x[j], lru_b_x[j], lru_lam[j], m_conv_w[j], m_conv_b[j], m_i_bias[j],
                         m_f_bias[j], m_head_g[j], ab_w_out[j])
            out_order = "same" if last else "fold16"
        else:
            x2, xs = _mixer1(x2, seq, norm_mix[layer], s5_w_in[j], s5_a_re[j], s5_a_im[j], s5_log_step[j], s5_b_re[j],
                         s5_b_im[j], s5_c_re[j], s5_c_im[j], s5_d[j], s5_w_glu_v[j], s5_w_glu_g[j])
            out_order = "unfold16"
        x2 = _moe_layer(x2, xs, norm_ffn[layer][None], gfin, moe_w_coarse[layer], moe_b_coarse[layer],
                        moe_w_fine[layer], moe_b_fine[layer], wg16, wu16, wd16, layer, last, out_order)
    return x2.reshape(bsz, seq, d)
```

```python
import functools
import math

import jax
import jax.numpy as jnp
from jax import lax
from jax.experimental import pallas as pl
from jax.experimental.pallas import tpu as pltpu

F32 = jnp.float32
BF16 = jnp.bfloat16
EPS = 1e-6
LANES = 128
VMEM_LIMIT = 56 * 1024 * 1024

CONV_K = 4
LRU_BLOCKS = 8
LRU_C = 8.0
MLSTM_HEADS = 4
S5_GROUP = 16
S5_STATE = 64
S5_L = 16
MOE_GROUPS = 4
MOE_PER_GROUP = 4
N_PAIRS = 6
N_CLASSES = MOE_GROUPS * N_PAIRS
PAIR_LO = (0, 0, 0, 1, 1, 2)
PAIR_HI = (1, 2, 3, 2, 3, 3)


def _tile(n, pref):
    return pref if n % pref == 0 else n


def _cparams(*sem):
    return pltpu.CompilerParams(dimension_semantics=sem, vmem_limit_bytes=VMEM_LIMIT)


def _rms(x, g):
    return x * lax.rsqrt(jnp.mean(x * x, axis=-1, keepdims=True) + EPS) * g


def _dot(a, b):
    return jnp.dot(a, b, preferred_element_type=F32)


def _dot_nt(a, b):
    return lax.dot_general(a, b, (((1,), (1,)), ((), ())), preferred_element_type=F32)


def _dot_tn(a, b):
    return lax.dot_general(a, b, (((0,), (0,)), ((), ())), preferred_element_type=F32)


def _softplus(y):
    return jnp.maximum(y, 0.0) + jnp.log1p(jnp.exp(-jnp.abs(y)))


def _split3(x):
    p1 = x.astype(BF16)
    r1 = x - p1.astype(F32)
    p2 = r1.astype(BF16)
    p3 = (r1 - p2.astype(F32)).astype(BF16)
    return p1, p2, p3


def _rows_to_slabs(y):
    k = y.shape[1] // LANES
    return jnp.swapaxes(jnp.stack([y[:, c * LANES:(c + 1) * LANES] for c in range(k)], axis=0), 0, 1)


def _slabs_to_rows(x3):
    xs = jnp.swapaxes(x3, 0, 1)
    return jnp.concatenate([xs[c] for c in range(x3.shape[1])], axis=1)


def _causal_conv(x, tail, w, b):
    n = x.shape[0]
    xp = jnp.concatenate([tail, x], axis=0)
    out = b + w[CONV_K - 1:CONV_K, :] * x
    for j in range(CONV_K - 1):
        off = 8 - (CONV_K - 1) + j
        out = out + w[j:j + 1, :] * xp[off:off + n, :]
    return out


def _in0_body(x_ref, g_ref, w_ref, wg_ref, cw_ref, cb_ref, pc_ref, pb_ref, gate_ref, tail_ref, *, seq_tiles, dh):
    tm, c = pc_ref.shape

    @pl.when(pl.program_id(0) == 0)
    def _():
        tail_ref[...] = jnp.zeros_like(tail_ref)

    hb = _rms(x_ref[...], g_ref[...]).astype(BF16)
    gate_ref[...] = _dot(hb, wg_ref[...])
    seq_start = (pl.program_id(0) % seq_tiles) == 0

    def proj(jt):
        return _dot(hb, w_ref[:, jt * c:(jt + 1) * c])

    def conv(k, raw):
        tail = jnp.where(seq_start, 0.0, tail_ref[k])
        tail_ref[k] = raw[tm - 8:tm, :]
        return _causal_conv(raw, tail, cw_ref[:, k * c:(k + 1) * c], cb_ref[:, k * c:(k + 1) * c])

    pc_ref[...] = conv(0, proj(0))
    pb_ref[:, 0:c] = jax.nn.gelu(proj(1)).astype(BF16)
    q = conv(1, proj(2))
    pb_ref[:, c:2 * c] = (q * jax.nn.sigmoid(q)).astype(BF16)
    k = conv(2, proj(3))
    pb_ref[:, 2 * c:3 * c] = (k * jax.nn.sigmoid(k) * (dh ** -0.5)).astype(BF16)
    pb_ref[:, 3 * c:4 * c] = proj(4).astype(BF16)
    pb_ref[:, 4 * c:5 * c] = jax.nn.sigmoid(proj(5)).astype(BF16)


def _in_proj0(x2, g, w, w_gate, conv_w, conv_b, seq, c, dh):
    t, d = x2.shape
    tm = _tile(seq, 512)
    assert seq % tm == 0
    return pl.pallas_call(
        functools.partial(_in0_body, seq_tiles=seq // tm, dh=dh),
        grid=(t // tm,),
        in_specs=[pl.BlockSpec((tm, d), lambda i: (i, 0)),
                  pl.BlockSpec((1, d), lambda i: (0, 0)),
                  pl.BlockSpec(w.shape, lambda i: (0, 0), pipeline_mode=pl.Buffered(1)),
                  pl.BlockSpec((d, LANES), lambda i: (0, 0)),
                  pl.BlockSpec(conv_w.shape, lambda i: (0, 0)),
                  pl.BlockSpec(conv_b.shape, lambda i: (0, 0))],
        out_specs=[pl.BlockSpec((tm, c), lambda i: (i, 0)),
                   pl.BlockSpec((tm, 5 * c), lambda i: (i, 0)),
                   pl.BlockSpec((tm, LANES), lambda i: (i, 0))],
        out_shape=[jax.ShapeDtypeStruct((t, c), F32), jax.ShapeDtypeStruct((t, 5 * c), BF16),
                   jax.ShapeDtypeStruct((t, LANES), F32)],
        scratch_shapes=[pltpu.VMEM((3, 8, c), F32)],
        compiler_params=_cparams("arbitrary"),
        name="in_proj0",
    )(x2, g, w, w_gate, conv_w, conv_b)


def _lru_body(x_ref, gz_ref, wa_ref, ba_ref, wx_ref, bx_ref, lam_ref, y_ref, h_ref, a_s, g_s):
    ts, c = x_ref.shape
    blk = c // LRU_BLOCKS

    @pl.when(pl.program_id(1) == 0)
    def _():
        h_ref[...] = jnp.zeros_like(h_ref)

    conv = x_ref[...]
    cb16 = conv.astype(BF16)
    ra = jnp.concatenate([_dot(cb16[:, n * blk:(n + 1) * blk], wa_ref[n]) for n in range(LRU_BLOCKS)], axis=1)
    rx = jnp.concatenate([_dot(cb16[:, n * blk:(n + 1) * blk], wx_ref[n]) for n in range(LRU_BLOCKS)], axis=1)
    r = jax.nn.sigmoid(ra + ba_ref[...])
    ig = jax.nn.sigmoid(rx + bx_ref[...])
    log_a = -LRU_C * r * _softplus(-lam_ref[...])
    a = jnp.exp(log_a)
    pa = a
    pg = jnp.sqrt(1.0 - a * a) * (ig * conv)
    row8 = lax.broadcasted_iota(jnp.int32, (ts, c), 0) & 7
    for s in (1, 2, 4):
        keep = row8 >= s
        ga = jnp.where(keep, pltpu.roll(pg, s, axis=0), 0.0)
        aa = jnp.where(keep, pltpu.roll(pa, s, axis=0), 1.0)
        pg = pa * ga + pg
        pa = pa * aa
    a_s[...] = pa
    g_s[...] = pg

    def group(k, h):
        r0 = pl.multiple_of(k * 8, 8)
        h8 = a_s[pl.ds(r0, 8), :] * h + g_s[pl.ds(r0, 8), :]
        g_s[pl.ds(r0, 8), :] = h8
        return h8[7:8, :]

    h_ref[...] = lax.fori_loop(0, ts // 8, group, h_ref[...], unroll=4)
    y_ref[...] = (g_s[...] * gz_ref[...].astype(F32)).astype(y_ref.dtype)


def _rg_lru(pc, pb, bsz, seq, wa, ba, wx, bx, lam):
    c = pc.shape[1]
    ts = _tile(seq, 512)
    ns = seq // ts
    row = lambda b, s: (0, 0)
    return pl.pallas_call(
        _lru_body,
        grid=(bsz, ns),
        in_specs=[pl.BlockSpec((ts, c), lambda b, s: (b * ns + s, 0)),
                  pl.BlockSpec((ts, c), lambda b, s: (b * ns + s, 0)),
                  pl.BlockSpec(wa.shape, lambda b, s: (0, 0, 0)),
                  pl.BlockSpec((1, c), row),
                  pl.BlockSpec(wx.shape, lambda b, s: (0, 0, 0)),
                  pl.BlockSpec((1, c), row),
                  pl.BlockSpec((1, c), row)],
        out_specs=pl.BlockSpec((ts, c), lambda b, s: (b * ns + s, 0)),
        out_shape=jax.ShapeDtypeStruct((bsz * seq, c), BF16),
        scratch_shapes=[pltpu.VMEM((1, c), F32), pltpu.VMEM((ts, c), F32), pltpu.VMEM((ts, c), F32)],
        compiler_params=_cparams("parallel", "arbitrary"),
        name="rg_lru",
    )(pc, pb, wa, ba, wx, bx, lam)


def _mlstm_body(q_ref, k_ref, v_ref, so_ref, gt_ref, gb_ref, hg_ref, tri_ref, y_ref, ct_ref, n_ref, m_ref):
    L = q_ref.shape[0]
    nh = ct_ref.shape[0]
    dh = ct_ref.shape[1]

    @pl.when(pl.program_id(1) == 0)
    def _():
        ct_ref[...] = jnp.zeros_like(ct_ref)
        n_ref[...] = jnp.zeros_like(n_ref)
        m_ref[...] = jnp.zeros_like(m_ref)

    gts = gt_ref[...] + gb_ref[...]
    lane = lax.broadcasted_iota(jnp.int32, gts.shape, 1)
    lsig = jnp.minimum(gts, 0.0) - jnp.log1p(jnp.exp(-jnp.abs(gts)))
    tri = tri_ref[...]
    p1, p2, p3 = _split3(lsig)
    bc_all = _dot(tri, p1) + _dot(tri, p2) + _dot(tri, p3)
    rows = lax.broadcasted_iota(jnp.int32, (L, L), 0)
    cols = lax.broadcasted_iota(jnp.int32, (L, L), 1)
    causal = rows >= cols

    for head in range(nh):
        hs = slice(head * dh, (head + 1) * dh)
        i_col = jnp.sum(jnp.where(lane == head, gts, 0.0), axis=-1, keepdims=True)
        bcum = jnp.sum(jnp.where(lane == head + nh, bc_all, 0.0), axis=-1, keepdims=True)
        a_col = i_col - bcum
        a_row = jnp.transpose(jnp.broadcast_to(a_col, (L, LANES)))[0:1, :]
        m_st = m_ref[head, 0:1, 0:1]

        d = jnp.where(causal, bcum + a_row, -jnp.inf)
        g = bcum + m_st
        m_t = jnp.maximum(g, jnp.max(d, axis=-1, keepdims=True))
        w_intra = jnp.exp(d - m_t)
        w_inter = jnp.exp(g - m_t)

        qb = q_ref[:, hs]
        kb = k_ref[:, hs]
        vb = v_ref[:, hs]
        s_qk = _dot_nt(qb, kb) * w_intra
        ct = ct_ref[head]
        n_row = n_ref[head]
        num = _dot(s_qk.astype(BF16), vb) + w_inter * _dot(qb, ct.astype(BF16))
        den = (jnp.sum(s_qk, axis=-1, keepdims=True)
               + w_inter * jnp.sum(qb.astype(F32) * n_row, axis=-1, keepdims=True))
        hcell = num / jnp.maximum(jnp.abs(den), jnp.exp(-m_t))

        b_last = bcum[L - 1:L, :]
        d_end = b_last + a_col
        m_new = jnp.maximum(b_last + m_st, jnp.max(d_end, axis=0, keepdims=True))
        w_end = jnp.exp(d_end - m_new)
        decay = jnp.exp(b_last + m_st - m_new)
        kw = kb.astype(F32) * w_end
        ct_ref[head] = decay * ct + _dot_tn(kw.astype(BF16), vb)
        n_ref[head] = decay * n_row + jnp.sum(kw, axis=0, keepdims=True)
        m_ref[head] = jnp.broadcast_to(m_new, m_ref.shape[1:])

        hm = so_ref[:, hs].astype(F32) * hcell
        hm = hm * lax.rsqrt(jnp.mean(hm * hm, axis=-1, keepdims=True) + EPS)
        y_ref[:, hs] = (hm * hg_ref[:, hs]).astype(y_ref.dtype)


def _mlstm(pb, gates, gate_bias, bsz, seq, head_g, col0):
    nh = MLSTM_HEADS
    w = head_g.shape[1]
    dh = w // nh
    L = _tile(seq, 256)
    nc = seq // L
    cq = col0 // w
    tri = jnp.tril(jnp.ones((L, L), BF16))

    def pblock(off):
        return pl.BlockSpec((L, w), lambda b, c: (b * nc + c, off))

    return pl.pallas_call(
        _mlstm_body,
        grid=(bsz, nc),
        in_specs=[pblock(cq), pblock(cq + 1), pblock(cq + 2), pblock(cq + 3),
                  pl.BlockSpec((L, LANES), lambda b, c: (b * nc + c, 0)),
                  pl.BlockSpec((1, LANES), lambda b, c: (0, 0)),
                  pl.BlockSpec((1, w), lambda b, c: (0, 0)),
                  pl.BlockSpec((L, L), lambda b, c: (0, 0))],
        out_specs=pl.BlockSpec((L, w), lambda b, c: (b * nc + c, 0)),
        out_shape=jax.ShapeDtypeStruct((bsz * seq, w), BF16),
        scratch_shapes=[pltpu.VMEM((nh, dh, dh), F32), pltpu.VMEM((nh, 1, dh), F32),
                        pltpu.VMEM((nh, 8, LANES), F32)],
        compiler_params=_cparams("parallel", "arbitrary"),
        name="mlstm",
    )(pb, pb, pb, pb, gates, gate_bias, head_g, tri)


def _outproj_body(ya_ref, yb_ref, wa_ref, wb_ref, x_ref, g_ref, whi_ref, wlo_ref, b_ref, tri_ref,
                  os_ref, rw_ref, cnt_ref, carry_ref):
    @pl.when(pl.program_id(0) == 0)
    def _():
        carry_ref[...] = jnp.zeros_like(carry_ref)

    res = x_ref[...] + _dot(ya_ref[...], wa_ref[...]) + _dot(yb_ref[...], wb_ref[...])
    os_ref[...] = _rows_to_slabs(res)
    rows, carry = _route_tile(res, g_ref[...], whi_ref[...], wlo_ref[...], b_ref[...], tri_ref[...], carry_ref[...])
    rw_ref[...] = rows
    carry_ref[...] = carry
    cnt_ref[...] = carry


def _out_proj(ya, yb, w, x2, g_ffn, wr_hi, wr_lo, rbias):
    t, d = x2.shape
    kh = ya.shape[1]
    assert yb.shape[1] == kh and w.shape[0] == 2 * kh
    tm = _tile(t, 512)
    tri = jnp.tril(jnp.ones((tm, tm), BF16), -1)
    cm = lambda i: (0, 0)
    return pl.pallas_call(
        _outproj_body,
        grid=(t // tm,),
        in_specs=[pl.BlockSpec((tm, kh), lambda i: (i, 0)),
                  pl.BlockSpec((tm, kh), lambda i: (i, 0)),
                  pl.BlockSpec((kh, d), lambda i: (0, 0)),
                  pl.BlockSpec((kh, d), lambda i: (1, 0)),
                  pl.BlockSpec((tm, d), lambda i: (i, 0)),
                  pl.BlockSpec((1, d), cm), pl.BlockSpec((d, LANES), cm), pl.BlockSpec((d, LANES), cm),
                  pl.BlockSpec((1, LANES), cm), pl.BlockSpec((tm, tm), cm)],
        out_specs=[pl.BlockSpec((tm, d // LANES, LANES), lambda i: (i, 0, 0)),
                   pl.BlockSpec((8, tm), lambda i: (0, i)),
                   pl.BlockSpec((1, LANES), cm)],
        out_shape=[jax.ShapeDtypeStruct((t, d // LANES, LANES), F32), jax.ShapeDtypeStruct((8, t), F32),
                   jax.ShapeDtypeStruct((1, LANES), F32)],
        scratch_shapes=[pltpu.VMEM((1, LANES), F32)],
        compiler_params=_cparams("arbitrary"),
        name="out_proj",
    )(ya, yb, w, w, x2, g_ffn, wr_hi, wr_lo, rbias, tri)


def _s5in_body(x_ref, g_ref, wt_ref, z_ref):
    hb = _rms(_slabs_to_rows(x_ref[...]), g_ref[...]).astype(BF16)
    res = _dot_nt(wt_ref[...], hb)
    z_ref[...] = res.reshape(z_ref.shape)


def _s5_in(xl, g, wt):
    t, nk, _ = xl.shape
    d = nk * LANES
    ch = wt.shape[0]
    ngrp = ch // S5_GROUP
    nr = t // S5_L
    r = _tile(nr, 512)
    nri = nr // r
    return pl.pallas_call(
        _s5in_body,
        grid=(nri, S5_L),
        in_specs=[pl.BlockSpec((r, nk, LANES), lambda i, l: (l * nri + i, 0, 0)),
                  pl.BlockSpec((1, d), lambda i, l: (0, 0)),
                  pl.BlockSpec((ch, d), lambda i, l: (0, 0))],
        out_specs=pl.BlockSpec((ngrp, S5_GROUP, r), lambda i, l: (0, l, i)),
        out_shape=jax.ShapeDtypeStruct((ngrp, S5_L * S5_GROUP, nr), F32),
        compiler_params=_cparams("parallel", "arbitrary"),
        name="s5_in",
    )(xl, g, wt)


def _s5core_body(z_ref, min_ref, mintra_ref, cst_ref, lp_ref, y_ref, *, nseg, nlev):
    p = S5_STATE
    zb = z_ref[0].astype(BF16)
    e = _dot(min_ref[0], zb)
    sr, si = e[:p, :], e[p:, :]
    nr = sr.shape[1]
    lane = lax.broadcasted_iota(jnp.int32, (1, nr), 1) % nseg
    lp = lp_ref[0]
    for j in range(nlev):
        s = 1 << j
        cr = lp[:, j:j + 1]
        ci = lp[:, nlev + j:nlev + j + 1]
        keep = lane >= s
        tr = jnp.where(keep, pltpu.roll(sr, s, axis=1), 0.0)
        ti = jnp.where(keep, pltpu.roll(si, s, axis=1), 0.0)
        sr, si = sr + cr * tr - ci * ti, si + cr * ti + ci * tr
    keep = lane >= 1
    xr = jnp.where(keep, pltpu.roll(sr, 1, axis=1), 0.0)
    xi = jnp.where(keep, pltpu.roll(si, 1, axis=1), 0.0)
    xb = jnp.concatenate([xr, xi], axis=0).astype(BF16)
    y = _dot(mintra_ref[0], zb) + _dot(cst_ref[0], xb)
    y_ref[0] = jax.nn.gelu(y).astype(y_ref.dtype)


def _s5_core(z, m_in, m_intra, c_st, lpow, seq):
    ngrp, kk, nr = z.shape
    nseg = seq // S5_L
    nlev = lpow.shape[2] // 2
    return pl.pallas_call(
        functools.partial(_s5core_body, nseg=nseg, nlev=nlev),
        grid=(ngrp,),
        in_specs=[pl.BlockSpec((1, kk, nr), lambda g: (g, 0, 0)),
                  pl.BlockSpec((1,) + m_in.shape[1:], lambda g: (g, 0, 0)),
                  pl.BlockSpec((1,) + m_intra.shape[1:], lambda g: (g, 0, 0)),
                  pl.BlockSpec((1,) + c_st.shape[1:], lambda g: (g, 0, 0)),
                  pl.BlockSpec((1,) + lpow.shape[1:], lambda g: (g, 0, 0))],
        out_specs=pl.BlockSpec((1, kk, nr), lambda g: (g, 0, 0)),
        out_shape=jax.ShapeDtypeStruct((ngrp, kk, nr), BF16),
        compiler_params=_cparams("parallel"),
        name="s5_core",
    )(z, m_in, m_intra, c_st, lpow)


def _glu_body(y_ref, wv_ref, wg_ref, x_ref, o_ref, os_ref):
    ngrp, grp, r = y_ref.shape
    y = y_ref[...].reshape(ngrp * grp, r)
    v = _dot(wv_ref[...], y)
    g = _dot(wg_ref[...], y)
    res = _slabs_to_rows(x_ref[...]) + jnp.transpose(v * jax.nn.sigmoid(g))
    o_ref[...] = res
    os_ref[...] = _rows_to_slabs(res)


def _s5_glu(y, wvt, wgt, xl):
    t, nk, _ = xl.shape
    d = nk * LANES
    ngrp, _, nr = y.shape
    r = _tile(nr, 512)
    nri = nr // r
    tn = _tile(d, 1024)
    return pl.pallas_call(
        _glu_body,
        grid=(d // tn, nri, S5_L),
        in_specs=[pl.BlockSpec((ngrp, S5_GROUP, r), lambda j, i, l: (0, l, i)),
                  pl.BlockSpec((tn, ngrp * S5_GROUP), lambda j, i, l: (j, 0)),
                  pl.BlockSpec((tn, ngrp * S5_GROUP), lambda j, i, l: (j, 0)),
                  pl.BlockSpec((r, tn // LANES, LANES), lambda j, i, l: (l * nri + i, j, 0))],
        out_specs=[pl.BlockSpec((r, tn), lambda j, i, l: (l * nri + i, j)),
                   pl.BlockSpec((r, tn // LANES, LANES), lambda j, i, l: (l * nri + i, j, 0))],
        out_shape=[jax.ShapeDtypeStruct((t, d), F32), jax.ShapeDtypeStruct((t, d // LANES, LANES), F32)],
        compiler_params=_cparams("arbitrary", "arbitrary", "arbitrary"),
        name="s5_glu",
    )(y, wvt, wgt, xl)


def _s5_operators(a_re, a_im, log_step, b_re, b_im, c_re, c_im, d_skip, nseg):
    hi = lax.Precision.HIGHEST
    L, p, grp = S5_L, S5_STATE, S5_GROUP
    ngrp = a_re.shape[0]
    dt = jnp.exp(log_step)[:, None]
    mag = jnp.exp(a_re * dt)
    lr, li = mag * jnp.cos(a_im * dt), mag * jnp.sin(a_im * dt)
    den = a_re * a_re + a_im * a_im
    fr = ((lr - 1.0) * a_re + li * a_im) / den
    fi = (li * a_re - (lr - 1.0) * a_im) / den
    bbr = fr[..., None] * b_re - fi[..., None] * b_im
    bbi = fr[..., None] * b_im + fi[..., None] * b_re
    pr, pi = [jnp.ones_like(lr)], [jnp.zeros_like(li)]
    for _ in range(L):
        pr_n = pr[-1] * lr - pi[-1] * li
        pi_n = pr[-1] * li + pi[-1] * lr
        pr.append(pr_n)
        pi.append(pi_n)
    pwr, pwi = jnp.stack(pr, 1), jnp.stack(pi, 1)
    lbr = pwr[:, :L, :, None] * bbr[:, None] - pwi[:, :L, :, None] * bbi[:, None]
    lbi = pwr[:, :L, :, None] * bbi[:, None] + pwi[:, :L, :, None] * bbr[:, None]
    m_in = jnp.concatenate([jnp.transpose(lbr[:, ::-1], (0, 2, 1, 3)).reshape(ngrp, p, L * grp),
                            jnp.transpose(lbi[:, ::-1], (0, 2, 1, 3)).reshape(ngrp, p, L * grp)], axis=1)
    taps = (jnp.einsum('gcp,gkpd->gkcd', c_re, lbr, precision=hi)
            - jnp.einsum('gcp,gkpd->gkcd', c_im, lbi, precision=hi))
    taps = taps.at[:, 0].add(jax.vmap(jnp.diag)(d_skip.reshape(ngrp, grp)))
    lag = jnp.arange(L)[:, None] - jnp.arange(L)[None, :]
    shift = (lag[None] == jnp.arange(L)[:, None, None]).astype(BF16)
    m_intra = jnp.einsum('klm,gkcd->glcmd', shift, taps.astype(BF16),
                         preferred_element_type=F32).reshape(ngrp, L * grp, L * grp)
    qr, qi = pwr[:, 1:], pwi[:, 1:]
    cs_r = c_re[:, None] * qr[:, :, None, :] - c_im[:, None] * qi[:, :, None, :]
    cs_i = -c_re[:, None] * qi[:, :, None, :] - c_im[:, None] * qr[:, :, None, :]
    c_st = jnp.concatenate([cs_r, cs_i], axis=-1).reshape(ngrp, L * grp, 2 * p)
    nlev = max(1, int(math.log2(nseg)))
    sr_, si_ = [pwr[:, L]], [pwi[:, L]]
    for _ in range(nlev - 1):
        sr_.append(sr_[-1] * sr_[-1] - si_[-1] * si_[-1])
        si_.append(2.0 * sr_[-2] * si_[-1])
    lpow = jnp.stack(sr_ + si_, axis=-1)
    return m_in.astype(BF16), m_intra.astype(BF16), c_st.astype(BF16), lpow


def _route_tile(x, g, whi, wlo, bias, tri, carry):
    h = _rms(x, g)
    h_hi = h.astype(BF16)
    h_lo = (h - h_hi.astype(F32)).astype(BF16)
    logits = _dot(h_hi, whi) + _dot(h_lo, whi) + _dot(h_hi, wlo) + bias
    lane = lax.broadcasted_iota(jnp.int32, logits.shape, 1).astype(F32)
    neg, big = -jnp.inf, 1e9

    def first_argmax(vals):
        mx = jnp.max(vals, axis=-1, keepdims=True)
        return jnp.min(jnp.where(vals == mx, lane, big), axis=-1, keepdims=True)

    gi = first_argmax(jnp.where(lane < MOE_GROUPS, logits, neg))
    base = MOE_GROUPS + MOE_PER_GROUP * gi
    lf = jnp.where((lane >= base) & (lane < base + MOE_PER_GROUP), logits, neg)
    i1 = first_argmax(lf)
    i2 = first_argmax(jnp.where(lane == i1, neg, lf))
    k1, k2 = i1 - base, i2 - base
    lo, hi = jnp.minimum(k1, k2), jnp.maximum(k1, k2)
    cls = gi * N_PAIRS + lo * (7.0 - lo) * 0.5 + (hi - lo - 1.0)
    onehot = jnp.where(lane == cls, 1.0, 0.0)
    before = _dot(tri, onehot.astype(BF16))
    rank = jnp.sum(onehot * (before + carry), axis=-1, keepdims=True)
    out = jnp.where(lane == 0, cls, jnp.where(lane == 1, rank, 0.0))
    return jnp.transpose(out)[0:8, :], carry + jnp.sum(onehot, axis=0, keepdims=True)


def _router_body(x_ref, g_ref, whi_ref, wlo_ref, b_ref, tri_ref, o_ref, cnt_ref, carry_ref):
    @pl.when(pl.program_id(0) == 0)
    def _():
        carry_ref[...] = jnp.zeros_like(carry_ref)

    rows, carry = _route_tile(x_ref[...], g_ref[...], whi_ref[...], wlo_ref[...], b_ref[...], tri_ref[...],
                              carry_ref[...])
    o_ref[...] = rows
    carry_ref[...] = carry
    cnt_ref[...] = carry


def _router(x2, g, w_hi, w_lo, bias):
    t, d = x2.shape
    tm = _tile(t, 512)
    tri = jnp.tril(jnp.ones((tm, tm), BF16), -1)
    return pl.pallas_call(
        _router_body,
        grid=(t // tm,),
        in_specs=[pl.BlockSpec((tm, d), lambda i: (i, 0)),
                  pl.BlockSpec((1, d), lambda i: (0, 0)),
                  pl.BlockSpec((d, LANES), lambda i: (0, 0)),
                  pl.BlockSpec((d, LANES), lambda i: (0, 0)),
                  pl.BlockSpec((1, LANES), lambda i: (0, 0)),
                  pl.BlockSpec((tm, tm), lambda i: (0, 0))],
        out_specs=[pl.BlockSpec((8, tm), lambda i: (0, i)),
                   pl.BlockSpec((1, LANES), lambda i: (0, 0))],
        out_shape=[jax.ShapeDtypeStruct((8, t), F32), jax.ShapeDtypeStruct((1, LANES), F32)],
        scratch_shapes=[pltpu.VMEM((1, LANES), F32)],
        compiler_params=_cparams("arbitrary"),
        name="router",
    )(x2, g, w_hi, w_lo, bias, tri)


def _moe_body(src_ref, nv_ref, elo_ref, ehi_ref,
              x_hbm, whi_ref, rb_ref, g_ref, gfin_ref, wg_lo, wu_lo, wd_lo, wg_hi, wu_hi, wd_hi,
              o_hbm, xbuf, obuf, sem_in, sem_out, *, final, out_order, nr):
    i = pl.program_id(0)
    nt = pl.num_programs(0)
    tm = xbuf.shape[1]
    slot = i % 2
    nv = nv_ref[i]

    def divmod_nonneg(v, m):
        if m & (m - 1) == 0:
            sh = m.bit_length() - 1
            return lax.shift_right_logical(v, sh), v & (m - 1)
        return v // m, v % m

    def out_row(row):
        if out_order == "fold16":
            q, r = divmod_nonneg(row, S5_L)
            return r * nr + q
        if out_order == "unfold16":
            q, r = divmod_nonneg(row, nr)
            return r * S5_L + q
        return row

    def gather_row(tile, s, r):
        row = src_ref[tile * tm + r]
        return pltpu.make_async_copy(x_hbm.at[pl.ds(row, 1)], xbuf.at[s, pl.ds(r, 1)], sem_in.at[s])

    def wait_gather(s):
        pltpu.make_async_copy(x_hbm.at[pl.ds(0, tm)], xbuf.at[s], sem_in.at[s]).wait()

    def scatter_row(tile, s, r):
        row = out_row(src_ref[tile * tm + r])
        return pltpu.make_async_copy(obuf.at[s, pl.ds(r, 1)], o_hbm.at[pl.ds(row, 1)], sem_out.at[s])

    def issue_scatter(tile, s, n):
        for b in range(tm // 8):
            @pl.when(8 * b + 8 <= n)
            def _(b=b):
                for u in range(8):
                    scatter_row(tile, s, 8 * b + u).start(priority=u % 2)

        def single(r, c):
            scatter_row(tile, s, r).start()
            return c

        lax.fori_loop((n // 8) * 8, n, single, 0)

    def wait_scatter(s, n):
        p = 1
        while p <= tm:
            @pl.when((n & p) != 0)
            def _(p=p):
                pltpu.make_async_copy(obuf.at[s, pl.ds(0, p)], o_hbm.at[pl.ds(0, p)], sem_out.at[s]).wait()
            p *= 2

    @pl.when((i == 0) & (nv > 0))
    def _():
        def body(r, c):
            gather_row(0, 0, r).start()
            return c

        lax.fori_loop(0, tm, body, 0, unroll=8)

    prev_used = (i > 0) & (nv_ref[jnp.maximum(i - 1, 0)] > 0)

    def tile_step(s):
        nxt = jnp.minimum(i + 1, nt - 1)
        for r in range(tm):
            gather_row(nxt, 1 - s, r).start(priority=1)
        wait_gather(s)
        x = _slabs_to_rows(xbuf[s])
        hb = _rms(x, g_ref[...]).astype(BF16)

        def expert(wg, wu, wd):
            a = _dot(hb, wg[0, 0])
            u = _dot(hb, wu[0, 0])
            he = (a * jax.nn.sigmoid(a) * u).astype(BF16)
            return _dot(he, wd[0, 0])

        logits = _dot(hb, whi_ref[...]) + rb_ref[...]
        lane = lax.broadcasted_iota(jnp.int32, logits.shape, 1)
        e_lo, e_hi = elo_ref[i], ehi_ref[i]
        is_c = lane < MOE_GROUPS
        mx = jnp.max(jnp.where(is_c, logits, -jnp.inf), axis=-1, keepdims=True)
        ex = jnp.where(is_c, jnp.exp(logits - mx), 0.0)
        p_g = (jnp.sum(jnp.where(lane == e_lo // MOE_PER_GROUP, ex, 0.0), axis=-1, keepdims=True)
               / jnp.sum(ex, axis=-1, keepdims=True))
        v_lo = jnp.sum(jnp.where(lane == MOE_GROUPS + e_lo, logits, 0.0), axis=-1, keepdims=True)
        v_hi = jnp.sum(jnp.where(lane == MOE_GROUPS + e_hi, logits, 0.0), axis=-1, keepdims=True)
        w_lo = p_g / (1.0 + jnp.exp(v_hi - v_lo))
        w_hi = p_g / (1.0 + jnp.exp(v_lo - v_hi))
        out = x + w_lo * expert(wg_lo, wu_lo, wd_lo) + w_hi * expert(wg_hi, wu_hi, wd_hi)
        if final:
            out = _rms(out, gfin_ref[...])
        obuf[s] = _rows_to_slabs(out) if len(obuf.shape) == 4 else out
        issue_scatter(i, s, nv)

    for s in (0, 1):
        @pl.when((nv > 0) & (slot == s))
        def _(s=s):
            tile_step(s)

        @pl.when((nv == 0) & prev_used & (slot == s))
        def _(s=s):
            wait_gather(s)

        @pl.when(prev_used & (slot == s))
        def _(s=s):
            wait_scatter(1 - s, nv_ref[jnp.maximum(i - 1, 0)])

        @pl.when((i == nt - 1) & (nv > 0) & (slot == s))
        def _(s=s):
            wait_gather(1 - s)
            wait_scatter(s, nv)


def _moe(xs, wr_hi, rbias, src, nvalid, elo, ehi, g, gfin, wg, wu, wd, layer, tm, final, out_order, out_slabs):
    t, nk, _ = xs.shape
    d = nk * LANES
    oshape = (nk, LANES) if out_slabs else (d,)
    nt = nvalid.shape[0]
    ff = wg.shape[3]
    cm = lambda i, *_: (0, 0)
    lo3 = lambda i, src, nv, elo, ehi: (layer, elo[i], 0, 0)
    hi3 = lambda i, src, nv, elo, ehi: (layer, ehi[i], 0, 0)
    return pl.pallas_call(
        functools.partial(_moe_body, final=final, out_order=out_order, nr=t // S5_L),
        grid_spec=pltpu.PrefetchScalarGridSpec(
            num_scalar_prefetch=4,
            grid=(nt,),
            in_specs=[pl.BlockSpec(memory_space=pl.ANY),
                      pl.BlockSpec((d, LANES), cm),
                      pl.BlockSpec((1, LANES), cm),
                      pl.BlockSpec((1, d), cm),
                      pl.BlockSpec((1, d), cm),
                      pl.BlockSpec((1, 1, d, ff), lo3), pl.BlockSpec((1, 1, d, ff), lo3),
                      pl.BlockSpec((1, 1, ff, d), lo3),
                      pl.BlockSpec((1, 1, d, ff), hi3), pl.BlockSpec((1, 1, d, ff), hi3),
                      pl.BlockSpec((1, 1, ff, d), hi3)],
            out_specs=pl.BlockSpec(memory_space=pl.ANY),
            scratch_shapes=[pltpu.VMEM((2, tm, nk, LANES), F32), pltpu.VMEM((2, tm) + oshape, F32),
                            pltpu.SemaphoreType.DMA((2,)), pltpu.SemaphoreType.DMA((2,))],
        ),
        out_shape=jax.ShapeDtypeStruct((t,) + oshape, F32),
        compiler_params=_cparams("arbitrary"),
        name="moe_final" if final else "moe",
    )(src, nvalid, elo, ehi, xs, wr_hi, rbias, g, gfin, wg, wu, wd, wg, wu, wd)


def _invert_body(dest_ref, lo_ref, hi_ref, src_ref):
    def place(r, c):
        src_ref[dest_ref[r]] = r
        return c

    lax.fori_loop(0, dest_ref.shape[0], place, 0, unroll=8)

    for k in range(lo_ref.shape[0]):
        def clear(j, c):
            src_ref[j] = 0
            return c

        lax.fori_loop(lo_ref[k], hi_ref[k], clear, 0)


def _invert(dest, pad_lo, pad_hi, n):
    smem = pl.BlockSpec(memory_space=pltpu.SMEM)
    return pl.pallas_call(
        _invert_body,
        in_specs=[smem, smem, smem],
        out_specs=smem,
        out_shape=jax.ShapeDtypeStruct((n,), jnp.int32),
        name="invert",
    )(dest, pad_lo, pad_hi)


def _router_weights(w_coarse, b_coarse, w_fine, b_fine):
    d = w_coarse.shape[0]
    wr = jnp.zeros((d, LANES), F32).at[:, :MOE_GROUPS].set(w_coarse)
    wr = wr.at[:, MOE_GROUPS:MOE_GROUPS + w_fine.shape[1]].set(w_fine)
    wr_hi = wr.astype(BF16)
    wr_lo = (wr - wr_hi.astype(F32)).astype(BF16)
    bias = jnp.zeros((1, LANES), F32).at[0, :MOE_GROUPS].set(b_coarse)
    bias = bias.at[0, MOE_GROUPS:MOE_GROUPS + b_fine.shape[0]].set(b_fine)
    return wr_hi, wr_lo, bias


def _moe_layer(xs, rw, cnt, g, gfin, wr_hi, bias, wg, wu, wd, layer, final, out_order):
    t = xs.shape[0]
    tm = _tile(t, 256)
    nt = t // tm + N_CLASSES
    i32 = jnp.int32
    cls = rw[0].astype(i32)
    rank = rw[1].astype(i32)
    counts = cnt[0, :N_CLASSES].astype(i32)
    tiles_per = (counts + tm - 1) // tm
    tile_end = jnp.cumsum(tiles_per)
    tile_start = tile_end - tiles_per
    class_ids = jnp.arange(N_CLASSES, dtype=i32)
    dest = jnp.sum(jnp.where(cls[:, None] == class_ids[None, :], (tile_start * tm)[None, :], 0), axis=1) + rank
    tile_ids = jnp.arange(nt, dtype=i32)
    total = tile_end[-1]
    pad_lo = jnp.concatenate([tile_start * tm + counts, (total * tm)[None]])
    pad_hi = jnp.concatenate([tile_end * tm, jnp.full((1,), nt * tm, i32)])
    src = _invert(dest, pad_lo.astype(i32), pad_hi.astype(i32), nt * tm)
    used = tile_ids < total
    tcls = jnp.sum((jnp.minimum(tile_ids, total - 1)[:, None] >= tile_end[None, :]).astype(i32), axis=1)
    tcls = jnp.clip(tcls, 0, N_CLASSES - 1)
    sel = tcls[:, None] == class_ids[None, :]
    tcount = jnp.sum(jnp.where(sel, counts[None, :], 0), axis=1)
    tstart = jnp.sum(jnp.where(sel, tile_start[None, :], 0), axis=1)
    nvalid = jnp.where(used, jnp.clip(tcount - (tile_ids - tstart) * tm, 0, tm), 0).astype(i32)
    grp, pair = tcls // N_PAIRS, tcls % N_PAIRS
    pair_ids = jnp.arange(N_PAIRS, dtype=i32)
    psel = pair[:, None] == pair_ids[None, :]
    elo = (grp * MOE_PER_GROUP + jnp.sum(jnp.where(psel, jnp.array(PAIR_LO, i32)[None, :], 0), axis=1)).astype(i32)
    ehi = (grp * MOE_PER_GROUP + jnp.sum(jnp.where(psel, jnp.array(PAIR_HI, i32)[None, :], 0), axis=1)).astype(i32)
    return _moe(xs, wr_hi, bias, src, nvalid, elo, ehi, g, gfin, wg, wu, wd, layer, tm, final, out_order,
                out_order == "fold16")


def _mixer0(x2, bsz, seq, g, w_in, lru_conv_w, lru_conv_b, lru_w_a, lru_b_a, lru_w_x, lru_b_x, lru_lam,
            m_conv_w, m_conv_b, m_i_bias, m_f_bias, m_head_g, w_out, g_ffn, wr_hi, wr_lo, rbias):
    d = x2.shape[1]
    c = lru_conv_w.shape[1]
    mw = m_head_g.shape[0]
    assert mw == c and m_conv_w.shape[1] == 2 * c
    nmain = 2 * c + 4 * mw
    w_gate = jnp.zeros((d, LANES), F32).at[:, :2 * MLSTM_HEADS].set(w_in[:, nmain:]).astype(BF16)
    conv_w = jnp.concatenate([lru_conv_w, m_conv_w], axis=1)
    conv_b = jnp.concatenate([lru_conv_b, m_conv_b])[None]
    pc, pb, gates = _in_proj0(x2, g[None], w_in.astype(BF16), w_gate, conv_w, conv_b, seq, c, mw // MLSTM_HEADS)
    ya = _rg_lru(pc, pb, bsz, seq, lru_w_a.astype(BF16), lru_b_a[None], lru_w_x.astype(BF16), lru_b_x[None],
                 lru_lam[None])
    gate_bias = jnp.zeros((1, LANES), F32).at[0, :MLSTM_HEADS].set(m_i_bias)
    gate_bias = gate_bias.at[0, MLSTM_HEADS:2 * MLSTM_HEADS].set(m_f_bias)
    yb = _mlstm(pb, gates, gate_bias, bsz, seq, m_head_g[None], c)
    return _out_proj(ya, yb, w_out.astype(BF16), x2, g_ffn, wr_hi, wr_lo, rbias)


def _mixer1(xs, seq, g, w_in, a_re, a_im, log_step, b_re, b_im, c_re, c_im, d_skip, w_glu_v, w_glu_g):
    ops = _s5_operators(a_re, a_im, log_step, b_re, b_im, c_re, c_im, d_skip, seq // S5_L)
    z = _s5_in(xs, g[None], jnp.transpose(w_in).astype(BF16))
    y = _s5_core(z, *ops, seq)
    return _s5_glu(y, jnp.transpose(w_glu_v).astype(BF16), jnp.transpose(w_glu_g).astype(BF16), xs)


def kernel(x, norm_mix, norm_ffn, norm_final, ab_w_in, lru_conv_w, lru_conv_b, lru_w_a, lru_b_a, lru_w_x, lru_b_x,
           lru_lam, m_conv_w, m_conv_b, m_i_bias, m_f_bias, m_head_g, ab_w_out, s5_w_in, s5_a_re, s5_a_im,
           s5_log_step, s5_b_re, s5_b_im, s5_c_re, s5_c_im, s5_d, s5_w_glu_v, s5_w_glu_g, moe_w_coarse,
           moe_b_coarse, moe_w_fine, moe_b_fine, moe_w_gate, moe_w_up, moe_w_down):
    bsz, seq, d = x.shape
    depth = norm_mix.shape[0]
    x2 = x.reshape(bsz * seq, d)
    gfin = norm_final[None]
    wg16, wu16, wd16 = moe_w_gate.astype(BF16), moe_w_up.astype(BF16), moe_w_down.astype(BF16)
    for layer in range(depth):
        j = layer // 2
        last = layer == depth - 1
        g_ffn = norm_ffn[layer][None]
        wr_hi, wr_lo, rbias = _router_weights(moe_w_coarse[layer], moe_b_coarse[layer], moe_w_fine[layer],
                                              moe_b_fine[layer])
        if layer % 2 == 0:
            xs, rw, cnt = _mixer0(x2, bsz, seq, norm_mix[layer], ab_w_in[j], lru_conv_w[j], lru_conv_b[j],
                                  lru_w_a[j], lru_b_a[j], lru_w_x[j], lru_b_x[j], lru_lam[j], m_conv_w[j],
                                  m_conv_b[j], m_i_bias[j], m_f_bias[j], m_head_g[j], ab_w_out[j],
                                  g_ffn, wr_hi, wr_lo, rbias)
            out_order = "same" if last else "fold16"
        else:
            xr, xs = _mixer1(x2, seq, norm_mix[layer], s5_w_in[j], s5_a_re[j], s5_a_im[j], s5_log_step[j],
                             s5_b_re[j], s5_b_im[j], s5_c_re[j], s5_c_im[j], s5_d[j], s5_w_glu_v[j], s5_w_glu_g[j])
            rw, cnt = _router(xr, g_ffn, wr_hi, wr_lo, rbias)
            out_order = "unfold16"
        x2 = _moe_layer(xs, rw, cnt, g_ffn, gfin, wr_hi, rbias, wg16, wu16, wd16, layer, last, out_order)
    return x2.reshape(bsz, seq, d)
```

```python
import functools
import math

import jax
import jax.numpy as jnp
from jax import lax
from jax.experimental import pallas as pl
from jax.experimental.pallas import tpu as pltpu

F32 = jnp.float32
BF16 = jnp.bfloat16
EPS = 1e-6
LANES = 128
VMEM_LIMIT = 56 * 1024 * 1024

CONV_K = 4
LRU_BLOCKS = 8
LRU_C = 8.0
MLSTM_HEADS = 4
S5_GROUP = 16
S5_STATE = 64
S5_L = 16
MOE_GROUPS = 4
MOE_PER_GROUP = 4
N_PAIRS = 6
N_CLASSES = MOE_GROUPS * N_PAIRS
PAIR_LO = (0, 0, 0, 1, 1, 2)
PAIR_HI = (1, 2, 3, 2, 3, 3)


def _tile(n, pref):
    return pref if n % pref == 0 else n


def _cparams(*sem):
    return pltpu.CompilerParams(dimension_semantics=sem, vmem_limit_bytes=VMEM_LIMIT)


def _rms(x, g):
    return x * lax.rsqrt(jnp.mean(x * x, axis=-1, keepdims=True) + EPS) * g


def _dot(a, b):
    return jnp.dot(a, b, preferred_element_type=F32)


def _dot_nt(a, b):
    return lax.dot_general(a, b, (((1,), (1,)), ((), ())), preferred_element_type=F32)


def _dot_tn(a, b):
    return lax.dot_general(a, b, (((0,), (0,)), ((), ())), preferred_element_type=F32)


def _softplus(y):
    return jnp.maximum(y, 0.0) + jnp.log1p(jnp.exp(-jnp.abs(y)))


def _split3(x):
    p1 = x.astype(BF16)
    r1 = x - p1.astype(F32)
    p2 = r1.astype(BF16)
    p3 = (r1 - p2.astype(F32)).astype(BF16)
    return p1, p2, p3


def _rows_to_slabs(y):
    k = y.shape[1] // LANES
    return jnp.swapaxes(jnp.stack([y[:, c * LANES:(c + 1) * LANES] for c in range(k)], axis=0), 0, 1)


def _slabs_to_rows(x3):
    xs = jnp.swapaxes(x3, 0, 1)
    return jnp.concatenate([xs[c] for c in range(x3.shape[1])], axis=1)


def _causal_conv(x, tail, w, b):
    n = x.shape[0]
    xp = jnp.concatenate([tail, x], axis=0)
    out = b + w[CONV_K - 1:CONV_K, :] * x
    for j in range(CONV_K - 1):
        off = 8 - (CONV_K - 1) + j
        out = out + w[j:j + 1, :] * xp[off:off + n, :]
    return out


def _in0_body(x_ref, g_ref, w_ref, wg_ref, cw_ref, cb_ref, pc_ref, pb_ref, gate_ref, tail_ref, *, seq_tiles, dh):
    tm, c = pc_ref.shape

    @pl.when(pl.program_id(0) == 0)
    def _():
        tail_ref[...] = jnp.zeros_like(tail_ref)

    hb = _rms(x_ref[...], g_ref[...]).astype(BF16)
    gate_ref[...] = _dot(hb, wg_ref[...])
    seq_start = (pl.program_id(0) % seq_tiles) == 0

    def proj(jt):
        return _dot(hb, w_ref[:, jt * c:(jt + 1) * c])

    def conv(k, raw):
        tail = jnp.where(seq_start, 0.0, tail_ref[k])
        tail_ref[k] = raw[tm - 8:tm, :]
        return _causal_conv(raw, tail, cw_ref[:, k * c:(k + 1) * c], cb_ref[:, k * c:(k + 1) * c])

    pc_ref[...] = conv(0, proj(0))
    pb_ref[:, 0:c] = jax.nn.gelu(proj(1)).astype(BF16)
    q = conv(1, proj(2))
    pb_ref[:, c:2 * c] = (q * jax.nn.sigmoid(q)).astype(BF16)
    k = conv(2, proj(3))
    pb_ref[:, 2 * c:3 * c] = (k * jax.nn.sigmoid(k) * (dh ** -0.5)).astype(BF16)
    pb_ref[:, 3 * c:4 * c] = proj(4).astype(BF16)
    pb_ref[:, 4 * c:5 * c] = jax.nn.sigmoid(proj(5)).astype(BF16)


def _in_proj0(x2, g, w, w_gate, conv_w, conv_b, seq, c, dh):
    t, d = x2.shape
    tm = _tile(seq, 512)
    assert seq % tm == 0
    return pl.pallas_call(
        functools.partial(_in0_body, seq_tiles=seq // tm, dh=dh),
        grid=(t // tm,),
        in_specs=[pl.BlockSpec((tm, d), lambda i: (i, 0)),
                  pl.BlockSpec((1, d), lambda i: (0, 0)),
                  pl.BlockSpec(w.shape, lambda i: (0, 0), pipeline_mode=pl.Buffered(1)),
                  pl.BlockSpec((d, LANES), lambda i: (0, 0)),
                  pl.BlockSpec(conv_w.shape, lambda i: (0, 0)),
                  pl.BlockSpec(conv_b.shape, lambda i: (0, 0))],
        out_specs=[pl.BlockSpec((tm, c), lambda i: (i, 0)),
                   pl.BlockSpec((tm, 5 * c), lambda i: (i, 0)),
                   pl.BlockSpec((tm, LANES), lambda i: (i, 0))],
        out_shape=[jax.ShapeDtypeStruct((t, c), F32), jax.ShapeDtypeStruct((t, 5 * c), BF16),
                   jax.ShapeDtypeStruct((t, LANES), F32)],
        scratch_shapes=[pltpu.VMEM((3, 8, c), F32)],
        compiler_params=_cparams("arbitrary"),
        name="in_proj0",
    )(x2, g, w, w_gate, conv_w, conv_b)


def _lru_body(x_ref, gz_ref, wa_ref, ba_ref, wx_ref, bx_ref, lam_ref, y_ref, h_ref, a_s, g_s):
    ts, c = x_ref.shape
    blk = c // LRU_BLOCKS

    @pl.when(pl.program_id(1) == 0)
    def _():
        h_ref[...] = jnp.zeros_like(h_ref)

    conv = x_ref[...]
    cb16 = conv.astype(BF16)
    ra = jnp.concatenate([_dot(cb16[:, n * blk:(n + 1) * blk], wa_ref[n]) for n in range(LRU_BLOCKS)], axis=1)
    rx = jnp.concatenate([_dot(cb16[:, n * blk:(n + 1) * blk], wx_ref[n]) for n in range(LRU_BLOCKS)], axis=1)
    r = jax.nn.sigmoid(ra + ba_ref[...])
    ig = jax.nn.sigmoid(rx + bx_ref[...])
    log_a = -LRU_C * r * _softplus(-lam_ref[...])
    a = jnp.exp(log_a)
    pa = a
    pg = jnp.sqrt(1.0 - a * a) * (ig * conv)
    row8 = lax.broadcasted_iota(jnp.int32, (ts, c), 0) & 7
    for s in (1, 2, 4):
        keep = row8 >= s
        ga = jnp.where(keep, pltpu.roll(pg, s, axis=0), 0.0)
        aa = jnp.where(keep, pltpu.roll(pa, s, axis=0), 1.0)
        pg = pa * ga + pg
        pa = pa * aa
    a_s[...] = pa
    g_s[...] = pg

    def group(k, h):
        r0 = pl.multiple_of(k * 8, 8)
        h8 = a_s[pl.ds(r0, 8), :] * h + g_s[pl.ds(r0, 8), :]
        g_s[pl.ds(r0, 8), :] = h8
        return h8[7:8, :]

    h_ref[...] = lax.fori_loop(0, ts // 8, group, h_ref[...], unroll=4)
    y_ref[...] = (g_s[...] * gz_ref[...].astype(F32)).astype(y_ref.dtype)


def _rg_lru(pc, pb, bsz, seq, wa, ba, wx, bx, lam):
    c = pc.shape[1]
    ts = _tile(seq, 512)
    ns = seq // ts
    row = lambda b, s: (0, 0)
    return pl.pallas_call(
        _lru_body,
        grid=(bsz, ns),
        in_specs=[pl.BlockSpec((ts, c), lambda b, s: (b * ns + s, 0)),
                  pl.BlockSpec((ts, c), lambda b, s: (b * ns + s, 0)),
                  pl.BlockSpec(wa.shape, lambda b, s: (0, 0, 0)),
                  pl.BlockSpec((1, c), row),
                  pl.BlockSpec(wx.shape, lambda b, s: (0, 0, 0)),
                  pl.BlockSpec((1, c), row),
                  pl.BlockSpec((1, c), row)],
        out_specs=pl.BlockSpec((ts, c), lambda b, s: (b * ns + s, 0)),
        out_shape=jax.ShapeDtypeStruct((bsz * seq, c), BF16),
        scratch_shapes=[pltpu.VMEM((1, c), F32), pltpu.VMEM((ts, c), F32), pltpu.VMEM((ts, c), F32)],
        compiler_params=_cparams("parallel", "arbitrary"),
        name="rg_lru",
    )(pc, pb, wa, ba, wx, bx, lam)


def _mlstm_body(q_ref, k_ref, v_ref, so_ref, gt_ref, gb_ref, hg_ref, tri_ref, y_ref, ct_ref, n_ref, m_ref):
    L = q_ref.shape[0]
    nh = ct_ref.shape[0]
    dh = ct_ref.shape[1]

    @pl.when(pl.program_id(1) == 0)
    def _():
        ct_ref[...] = jnp.zeros_like(ct_ref)
        n_ref[...] = jnp.zeros_like(n_ref)
        m_ref[...] = jnp.zeros_like(m_ref)

    gts = gt_ref[...] + gb_ref[...]
    lane = lax.broadcasted_iota(jnp.int32, gts.shape, 1)
    lsig = jnp.minimum(gts, 0.0) - jnp.log1p(jnp.exp(-jnp.abs(gts)))
    tri = tri_ref[...]
    p1, p2, p3 = _split3(lsig)
    bc_all = _dot(tri, p1) + _dot(tri, p2) + _dot(tri, p3)
    rows = lax.broadcasted_iota(jnp.int32, (L, L), 0)
    cols = lax.broadcasted_iota(jnp.int32, (L, L), 1)
    causal = rows >= cols

    for head in range(nh):
        hs = slice(head * dh, (head + 1) * dh)
        i_col = jnp.sum(jnp.where(lane == head, gts, 0.0), axis=-1, keepdims=True)
        bcum = jnp.sum(jnp.where(lane == head + nh, bc_all, 0.0), axis=-1, keepdims=True)
        a_col = i_col - bcum
        a_row = jnp.transpose(jnp.broadcast_to(a_col, (L, LANES)))[0:1, :]
        m_st = m_ref[head, 0:1, 0:1]

        d = jnp.where(causal, bcum + a_row, -jnp.inf)
        g = bcum + m_st
        m_t = jnp.maximum(g, jnp.max(d, axis=-1, keepdims=True))
        w_intra = jnp.exp(d - m_t)
        w_inter = jnp.exp(g - m_t)

        qb = q_ref[:, hs]
        kb = k_ref[:, hs]
        vb = v_ref[:, hs]
        s_qk = _dot_nt(qb, kb) * w_intra
        ct = ct_ref[head]
        n_row = n_ref[head]
        num = _dot(s_qk.astype(BF16), vb) + w_inter * _dot(qb, ct.astype(BF16))
        den = (jnp.sum(s_qk, axis=-1, keepdims=True)
               + w_inter * jnp.sum(qb.astype(F32) * n_row, axis=-1, keepdims=True))
        hcell = num / jnp.maximum(jnp.abs(den), jnp.exp(-m_t))

        b_last = bcum[L - 1:L, :]
        d_end = b_last + a_col
        m_new = jnp.maximum(b_last + m_st, jnp.max(d_end, axis=0, keepdims=True))
        w_end = jnp.exp(d_end - m_new)
        decay = jnp.exp(b_last + m_st - m_new)
        kw = kb.astype(F32) * w_end
        ct_ref[head] = decay * ct + _dot_tn(kw.astype(BF16), vb)
        n_ref[head] = decay * n_row + jnp.sum(kw, axis=0, keepdims=True)
        m_ref[head] = jnp.broadcast_to(m_new, m_ref.shape[1:])

        hm = so_ref[:, hs].astype(F32) * hcell
        hm = hm * lax.rsqrt(jnp.mean(hm * hm, axis=-1, keepdims=True) + EPS)
        y_ref[:, hs] = (hm * hg_ref[:, hs]).astype(y_ref.dtype)


def _mlstm(pb, gates, gate_bias, bsz, seq, head_g, col0):
    nh = MLSTM_HEADS
    w = head_g.shape[1]
    dh = w // nh
    L = _tile(seq, 256)
    nc = seq // L
    cq = col0 // w
    tri = jnp.tril(jnp.ones((L, L), BF16))

    def pblock(off):
        return pl.BlockSpec((L, w), lambda b, c: (b * nc + c, off))

    return pl.pallas_call(
        _mlstm_body,
        grid=(bsz, nc),
        in_specs=[pblock(cq), pblock(cq + 1), pblock(cq + 2), pblock(cq + 3),
                  pl.BlockSpec((L, LANES), lambda b, c: (b * nc + c, 0)),
                  pl.BlockSpec((1, LANES), lambda b, c: (0, 0)),
                  pl.BlockSpec((1, w), lambda b, c: (0, 0)),
                  pl.BlockSpec((L, L), lambda b, c: (0, 0))],
        out_specs=pl.BlockSpec((L, w), lambda b, c: (b * nc + c, 0)),
        out_shape=jax.ShapeDtypeStruct((bsz * seq, w), BF16),
        scratch_shapes=[pltpu.VMEM((nh, dh, dh), F32), pltpu.VMEM((nh, 1, dh), F32),
                        pltpu.VMEM((nh, 8, LANES), F32)],
        compiler_params=_cparams("parallel", "arbitrary"),
        name="mlstm",
    )(pb, pb, pb, pb, gates, gate_bias, head_g, tri)


def _outproj_body(ya_ref, yb_ref, wa_ref, wb_ref, x_ref, g_ref, whi_ref, wlo_ref, b_ref, tri_ref,
                  os_ref, rw_ref, cnt_ref, carry_ref):
    @pl.when(pl.program_id(0) == 0)
    def _():
        carry_ref[...] = jnp.zeros_like(carry_ref)

    res = x_ref[...] + _dot(ya_ref[...], wa_ref[...]) + _dot(yb_ref[...], wb_ref[...])
    os_ref[...] = _rows_to_slabs(res)
    rows, carry = _route_tile(res, g_ref[...], whi_ref[...], wlo_ref[...], b_ref[...], tri_ref[...], carry_ref[...])
    rw_ref[...] = rows
    carry_ref[...] = carry
    cnt_ref[...] = carry


def _out_proj(ya, yb, w, x2, g_ffn, wr_hi, wr_lo, rbias):
    t, d = x2.shape
    kh = ya.shape[1]
    assert yb.shape[1] == kh and w.shape[0] == 2 * kh
    tm = _tile(t, 512)
    tri = jnp.tril(jnp.ones((tm, tm), BF16), -1)
    cm = lambda i: (0, 0)
    return pl.pallas_call(
        _outproj_body,
        grid=(t // tm,),
        in_specs=[pl.BlockSpec((tm, kh), lambda i: (i, 0)),
                  pl.BlockSpec((tm, kh), lambda i: (i, 0)),
                  pl.BlockSpec((kh, d), lambda i: (0, 0)),
                  pl.BlockSpec((kh, d), lambda i: (1, 0)),
                  pl.BlockSpec((tm, d), lambda i: (i, 0)),
                  pl.BlockSpec((1, d), cm), pl.BlockSpec((d, LANES), cm), pl.BlockSpec((d, LANES), cm),
                  pl.BlockSpec((1, LANES), cm), pl.BlockSpec((tm, tm), cm)],
        out_specs=[pl.BlockSpec((tm, d // LANES, LANES), lambda i: (i, 0, 0)),
                   pl.BlockSpec((8, tm), lambda i: (0, i)),
                   pl.BlockSpec((1, LANES), cm)],
        out_shape=[jax.ShapeDtypeStruct((t, d // LANES, LANES), F32), jax.ShapeDtypeStruct((8, t), F32),
                   jax.ShapeDtypeStruct((1, LANES), F32)],
        scratch_shapes=[pltpu.VMEM((1, LANES), F32)],
        compiler_params=_cparams("arbitrary"),
        name="out_proj",
    )(ya, yb, w, w, x2, g_ffn, wr_hi, wr_lo, rbias, tri)


def _s5in_body(x_ref, g_ref, wt_ref, z_ref):
    hb = _rms(_slabs_to_rows(x_ref[...]), g_ref[...]).astype(BF16)
    res = _dot_nt(wt_ref[...], hb)
    z_ref[...] = res.reshape(z_ref.shape)


def _s5_in(xl, g, wt):
    t, nk, _ = xl.shape
    d = nk * LANES
    ch = wt.shape[0]
    ngrp = ch // S5_GROUP
    nr = t // S5_L
    r = _tile(nr, 512)
    nri = nr // r
    return pl.pallas_call(
        _s5in_body,
        grid=(nri, S5_L),
        in_specs=[pl.BlockSpec((r, nk, LANES), lambda i, l: (l * nri + i, 0, 0)),
                  pl.BlockSpec((1, d), lambda i, l: (0, 0)),
                  pl.BlockSpec((ch, d), lambda i, l: (0, 0))],
        out_specs=pl.BlockSpec((ngrp, S5_GROUP, r), lambda i, l: (0, l, i)),
        out_shape=jax.ShapeDtypeStruct((ngrp, S5_L * S5_GROUP, nr), F32),
        compiler_params=_cparams("parallel", "arbitrary"),
        name="s5_in",
    )(xl, g, wt)


def _s5core_body(z_ref, min_ref, mintra_ref, cst_ref, lp_ref, y_ref, *, nseg, nlev):
    p = S5_STATE
    zb = z_ref[0].astype(BF16)
    e = _dot(min_ref[0], zb)
    sr, si = e[:p, :], e[p:, :]
    nr = sr.shape[1]
    lane = lax.broadcasted_iota(jnp.int32, (1, nr), 1) % nseg
    lp = lp_ref[0]
    for j in range(nlev):
        s = 1 << j
        cr = lp[:, j:j + 1]
        ci = lp[:, nlev + j:nlev + j + 1]
        keep = lane >= s
        tr = jnp.where(keep, pltpu.roll(sr, s, axis=1), 0.0)
        ti = jnp.where(keep, pltpu.roll(si, s, axis=1), 0.0)
        sr, si = sr + cr * tr - ci * ti, si + cr * ti + ci * tr
    keep = lane >= 1
    xr = jnp.where(keep, pltpu.roll(sr, 1, axis=1), 0.0)
    xi = jnp.where(keep, pltpu.roll(si, 1, axis=1), 0.0)
    xb = jnp.concatenate([xr, xi], axis=0).astype(BF16)
    y = _dot(mintra_ref[0], zb) + _dot(cst_ref[0], xb)
    y_ref[0] = jax.nn.gelu(y).astype(y_ref.dtype)


def _s5_core(z, m_in, m_intra, c_st, lpow, seq):
    ngrp, kk, nr = z.shape
    nseg = seq // S5_L
    nlev = lpow.shape[2] // 2
    return pl.pallas_call(
        functools.partial(_s5core_body, nseg=nseg, nlev=nlev),
        grid=(ngrp,),
        in_specs=[pl.BlockSpec((1, kk, nr), lambda g: (g, 0, 0)),
                  pl.BlockSpec((1,) + m_in.shape[1:], lambda g: (g, 0, 0)),
                  pl.BlockSpec((1,) + m_intra.shape[1:], lambda g: (g, 0, 0)),
                  pl.BlockSpec((1,) + c_st.shape[1:], lambda g: (g, 0, 0)),
                  pl.BlockSpec((1,) + lpow.shape[1:], lambda g: (g, 0, 0))],
        out_specs=pl.BlockSpec((1, kk, nr), lambda g: (g, 0, 0)),
        out_shape=jax.ShapeDtypeStruct((ngrp, kk, nr), BF16),
        compiler_params=_cparams("parallel"),
        name="s5_core",
    )(z, m_in, m_intra, c_st, lpow)


def _glu_body(y_ref, wv_ref, wg_ref, x_ref, o_ref, os_ref):
    ngrp, grp, r = y_ref.shape
    y = y_ref[...].reshape(ngrp * grp, r)
    v = _dot(wv_ref[...], y)
    g = _dot(wg_ref[...], y)
    res = _slabs_to_rows(x_ref[...]) + jnp.transpose(v * jax.nn.sigmoid(g))
    o_ref[...] = res
    os_ref[...] = _rows_to_slabs(res)


def _s5_glu(y, wvt, wgt, xl):
    t, nk, _ = xl.shape
    d = nk * LANES
    ngrp, _, nr = y.shape
    r = _tile(nr, 512)
    nri = nr // r
    tn = _tile(d, 1024)
    return pl.pallas_call(
        _glu_body,
        grid=(d // tn, nri, S5_L),
        in_specs=[pl.BlockSpec((ngrp, S5_GROUP, r), lambda j, i, l: (0, l, i)),
                  pl.BlockSpec((tn, ngrp * S5_GROUP), lambda j, i, l: (j, 0)),
                  pl.BlockSpec((tn, ngrp * S5_GROUP), lambda j, i, l: (j, 0)),
                  pl.BlockSpec((r, tn // LANES, LANES), lambda j, i, l: (l * nri + i, j, 0))],
        out_specs=[pl.BlockSpec((r, tn), lambda j, i, l: (l * nri + i, j)),
                   pl.BlockSpec((r, tn // LANES, LANES), lambda j, i, l: (l * nri + i, j, 0))],
        out_shape=[jax.ShapeDtypeStruct((t, d), F32), jax.ShapeDtypeStruct((t, d // LANES, LANES), F32)],
        compiler_params=_cparams("arbitrary", "arbitrary", "arbitrary"),
        name="s5_glu",
    )(y, wvt, wgt, xl)


def _s5_operators(a_re, a_im, log_step, b_re, b_im, c_re, c_im, d_skip, nseg):
    hi = lax.Precision.HIGHEST
    L, p, grp = S5_L, S5_STATE, S5_GROUP
    ngrp = a_re.shape[0]
    dt = jnp.exp(log_step)[:, None]
    mag = jnp.exp(a_re * dt)
    lr, li = mag * jnp.cos(a_im * dt), mag * jnp.sin(a_im * dt)
    den = a_re * a_re + a_im * a_im
    fr = ((lr - 1.0) * a_re + li * a_im) / den
    fi = (li * a_re - (lr - 1.0) * a_im) / den
    bbr = fr[..., None] * b_re - fi[..., None] * b_im
    bbi = fr[..., None] * b_im + fi[..., None] * b_re
    pr, pi = [jnp.ones_like(lr)], [jnp.zeros_like(li)]
    for _ in range(L):
        pr_n = pr[-1] * lr - pi[-1] * li
        pi_n = pr[-1] * li + pi[-1] * lr
        pr.append(pr_n)
        pi.append(pi_n)
    pwr, pwi = jnp.stack(pr, 1), jnp.stack(pi, 1)
    lbr = pwr[:, :L, :, None] * bbr[:, None] - pwi[:, :L, :, None] * bbi[:, None]
    lbi = pwr[:, :L, :, None] * bbi[:, None] + pwi[:, :L, :, None] * bbr[:, None]
    m_in = jnp.concatenate([jnp.transpose(lbr[:, ::-1], (0, 2, 1, 3)).reshape(ngrp, p, L * grp),
                            jnp.transpose(lbi[:, ::-1], (0, 2, 1, 3)).reshape(ngrp, p, L * grp)], axis=1)
    taps = (jnp.einsum('gcp,gkpd->gkcd', c_re, lbr, precision=hi)
            - jnp.einsum('gcp,gkpd->gkcd', c_im, lbi, precision=hi))
    taps = taps.at[:, 0].add(jax.vmap(jnp.diag)(d_skip.reshape(ngrp, grp)))
    lag = jnp.arange(L)[:, None] - jnp.arange(L)[None, :]
    shift = (lag[None] == jnp.arange(L)[:, None, None]).astype(BF16)
    m_intra = jnp.einsum('klm,gkcd->glcmd', shift, taps.astype(BF16),
                         preferred_element_type=F32).reshape(ngrp, L * grp, L * grp)
    qr, qi = pwr[:, 1:], pwi[:, 1:]
    cs_r = c_re[:, None] * qr[:, :, None, :] - c_im[:, None] * qi[:, :, None, :]
    cs_i = -c_re[:, None] * qi[:, :, None, :] - c_im[:, None] * qr[:, :, None, :]
    c_st = jnp.concatenate([cs_r, cs_i], axis=-1).reshape(ngrp, L * grp, 2 * p)
    nlev = max(1, int(math.log2(nseg)))
    sr_, si_ = [pwr[:, L]], [pwi[:, L]]
    for _ in range(nlev - 1):
        sr_.append(sr_[-1] * sr_[-1] - si_[-1] * si_[-1])
        si_.append(2.0 * sr_[-2] * si_[-1])
    lpow = jnp.stack(sr_ + si_, axis=-1)
    return m_in.astype(BF16), m_intra.astype(BF16), c_st.astype(BF16), lpow


def _route_tile(x, g, whi, wlo, bias, tri, carry):
    h = _rms(x, g)
    h_hi = h.astype(BF16)
    h_lo = (h - h_hi.astype(F32)).astype(BF16)
    logits = _dot(h_hi, whi) + _dot(h_lo, whi) + _dot(h_hi, wlo) + bias
    lane = lax.broadcasted_iota(jnp.int32, logits.shape, 1).astype(F32)
    neg, big = -jnp.inf, 1e9

    def first_argmax(vals):
        mx = jnp.max(vals, axis=-1, keepdims=True)
        return jnp.min(jnp.where(vals == mx, lane, big), axis=-1, keepdims=True)

    gi = first_argmax(jnp.where(lane < MOE_GROUPS, logits, neg))
    base = MOE_GROUPS + MOE_PER_GROUP * gi
    lf = jnp.where((lane >= base) & (lane < base + MOE_PER_GROUP), logits, neg)
    i1 = first_argmax(lf)
    i2 = first_argmax(jnp.where(lane == i1, neg, lf))
    k1, k2 = i1 - base, i2 - base
    lo, hi = jnp.minimum(k1, k2), jnp.maximum(k1, k2)
    cls = gi * N_PAIRS + lo * (7.0 - lo) * 0.5 + (hi - lo - 1.0)
    onehot = jnp.where(lane == cls, 1.0, 0.0)
    before = _dot(tri, onehot.astype(BF16))
    rank = jnp.sum(onehot * (before + carry), axis=-1, keepdims=True)
    out = jnp.where(lane == 0, cls, jnp.where(lane == 1, rank, 0.0))
    return jnp.transpose(out)[0:8, :], carry + jnp.sum(onehot, axis=0, keepdims=True)


def _router_body(x_ref, g_ref, whi_ref, wlo_ref, b_ref, tri_ref, o_ref, cnt_ref, carry_ref):
    @pl.when(pl.program_id(0) == 0)
    def _():
        carry_ref[...] = jnp.zeros_like(carry_ref)

    rows, carry = _route_tile(x_ref[...], g_ref[...], whi_ref[...], wlo_ref[...], b_ref[...], tri_ref[...],
                              carry_ref[...])
    o_ref[...] = rows
    carry_ref[...] = carry
    cnt_ref[...] = carry


def _router(x2, g, w_hi, w_lo, bias):
    t, d = x2.shape
    tm = _tile(t, 512)
    tri = jnp.tril(jnp.ones((tm, tm), BF16), -1)
    return pl.pallas_call(
        _router_body,
        grid=(t // tm,),
        in_specs=[pl.BlockSpec((tm, d), lambda i: (i, 0)),
                  pl.BlockSpec((1, d), lambda i: (0, 0)),
                  pl.BlockSpec((d, LANES), lambda i: (0, 0)),
                  pl.BlockSpec((d, LANES), lambda i: (0, 0)),
                  pl.BlockSpec((1, LANES), lambda i: (0, 0)),
                  pl.BlockSpec((tm, tm), lambda i: (0, 0))],
        out_specs=[pl.BlockSpec((8, tm), lambda i: (0, i)),
                   pl.BlockSpec((1, LANES), lambda i: (0, 0))],
        out_shape=[jax.ShapeDtypeStruct((8, t), F32), jax.ShapeDtypeStruct((1, LANES), F32)],
        scratch_shapes=[pltpu.VMEM((1, LANES), F32)],
        compiler_params=_cparams("arbitrary"),
        name="router",
    )(x2, g, w_hi, w_lo, bias, tri)


def _moe_body(src_ref, nv_ref, elo_ref, ehi_ref,
              x_hbm, whi_ref, rb_ref, g_ref, gfin_ref, wg_lo, wu_lo, wd_lo, wg_hi, wu_hi, wd_hi,
              o_hbm, xbuf, obuf, sem_in, sem_out, *, final, out_order, nr):
    i = pl.program_id(0)
    nt = pl.num_programs(0)
    tm = xbuf.shape[1]
    slot = i % 2
    nv = nv_ref[i]

    def divmod_nonneg(v, m):
        if m & (m - 1) == 0:
            sh = m.bit_length() - 1
            return lax.shift_right_logical(v, sh), v & (m - 1)
        return v // m, v % m

    def out_row(row):
        if out_order == "fold16":
            q, r = divmod_nonneg(row, S5_L)
            return r * nr + q
        if out_order == "unfold16":
            q, r = divmod_nonneg(row, nr)
            return r * S5_L + q
        return row

    def gather_row(tile, s, r):
        row = src_ref[tile * tm + r]
        return pltpu.make_async_copy(x_hbm.at[pl.ds(row, 1)], xbuf.at[s, pl.ds(r, 1)], sem_in.at[s])

    def wait_gather(s):
        pltpu.make_async_copy(x_hbm.at[pl.ds(0, tm)], xbuf.at[s], sem_in.at[s]).wait()

    def scatter_row(tile, s, r):
        row = out_row(src_ref[tile * tm + r])
        return pltpu.make_async_copy(obuf.at[s, pl.ds(r, 1)], o_hbm.at[pl.ds(row, 1)], sem_out.at[s])

    def issue_scatter(tile, s, n):
        for b in range(tm // 8):
            @pl.when(8 * b + 8 <= n)
            def _(b=b):
                for u in range(8):
                    scatter_row(tile, s, 8 * b + u).start(priority=u % 2)

        def single(r, c):
            scatter_row(tile, s, r).start()
            return c

        lax.fori_loop((n // 8) * 8, n, single, 0)

    def wait_scatter(s, n):
        p = 1
        while p <= tm:
            @pl.when((n & p) != 0)
            def _(p=p):
                pltpu.make_async_copy(obuf.at[s, pl.ds(0, p)], o_hbm.at[pl.ds(0, p)], sem_out.at[s]).wait()
            p *= 2

    def started(k):
        return nv_ref[jnp.clip(k - 2, 0, nt - 1)] > 0

    @pl.when((i == 0) & (nv > 0))
    def _():
        for k in (0, 1):
            def body(r, c, k=k):
                gather_row(jnp.minimum(k, nt - 1), k, r).start()
                return c

            lax.fori_loop(0, tm, body, 0, unroll=8)

    prev_used = (i > 0) & (nv_ref[jnp.maximum(i - 1, 0)] > 0)
    xslot = i % 3

    def tile_step(s):
        wait_gather(xslot)
        x = _slabs_to_rows(xbuf[xslot])
        hb = _rms(x, g_ref[...]).astype(BF16)
        ahead = jnp.minimum(i + 2, nt - 1)
        aslot = (i + 2) % 3
        for r in range(tm):
            gather_row(ahead, aslot, r).start(priority=1)

        def expert(wg, wu, wd):
            a = _dot(hb, wg[0, 0])
            u = _dot(hb, wu[0, 0])
            he = (a * jax.nn.sigmoid(a) * u).astype(BF16)
            return _dot(he, wd[0, 0])

        logits = _dot(hb, whi_ref[...]) + rb_ref[...]
        lane = lax.broadcasted_iota(jnp.int32, logits.shape, 1)
        e_lo, e_hi = elo_ref[i], ehi_ref[i]
        is_c = lane < MOE_GROUPS
        mx = jnp.max(jnp.where(is_c, logits, -jnp.inf), axis=-1, keepdims=True)
        ex = jnp.where(is_c, jnp.exp(logits - mx), 0.0)
        p_g = (jnp.sum(jnp.where(lane == e_lo // MOE_PER_GROUP, ex, 0.0), axis=-1, keepdims=True)
               / jnp.sum(ex, axis=-1, keepdims=True))
        v_lo = jnp.sum(jnp.where(lane == MOE_GROUPS + e_lo, logits, 0.0), axis=-1, keepdims=True)
        v_hi = jnp.sum(jnp.where(lane == MOE_GROUPS + e_hi, logits, 0.0), axis=-1, keepdims=True)
        w_lo = p_g / (1.0 + jnp.exp(v_hi - v_lo))
        w_hi = p_g / (1.0 + jnp.exp(v_lo - v_hi))
        out = x + w_lo * expert(wg_lo, wu_lo, wd_lo) + w_hi * expert(wg_hi, wu_hi, wd_hi)
        if final:
            out = _rms(out, gfin_ref[...])
        obuf[s] = _rows_to_slabs(out) if len(obuf.shape) == 4 else out
        issue_scatter(i, s, nv)

    for s in (0, 1):
        @pl.when((nv > 0) & (slot == s))
        def _(s=s):
            tile_step(s)

        @pl.when(prev_used & (slot == s))
        def _(s=s):
            wait_scatter(1 - s, nv_ref[jnp.maximum(i - 1, 0)])

        @pl.when((i == nt - 1) & (nv > 0) & (slot == s))
        def _(s=s):
            wait_scatter(s, nv)

    @pl.when((nv == 0) & started(i))
    def _():
        wait_gather(xslot)

    for k in (1, 2):
        @pl.when((i == nt - 1) & started(nt - 1 + k))
        def _(k=k):
            wait_gather((nt - 1 + k) % 3)


def _moe(xs, wr_hi, rbias, src, nvalid, elo, ehi, g, gfin, wg, wu, wd, layer, tm, final, out_order, out_slabs):
    t, nk, _ = xs.shape
    d = nk * LANES
    oshape = (nk, LANES) if out_slabs else (d,)
    nt = nvalid.shape[0]
    ff = wg.shape[3]
    cm = lambda i, *_: (0, 0)
    lo3 = lambda i, src, nv, elo, ehi: (layer, elo[i], 0, 0)
    hi3 = lambda i, src, nv, elo, ehi: (layer, ehi[i], 0, 0)
    return pl.pallas_call(
        functools.partial(_moe_body, final=final, out_order=out_order, nr=t // S5_L),
        grid_spec=pltpu.PrefetchScalarGridSpec(
            num_scalar_prefetch=4,
            grid=(nt,),
            in_specs=[pl.BlockSpec(memory_space=pl.ANY),
                      pl.BlockSpec((d, LANES), cm),
                      pl.BlockSpec((1, LANES), cm),
                      pl.BlockSpec((1, d), cm),
                      pl.BlockSpec((1, d), cm),
                      pl.BlockSpec((1, 1, d, ff), lo3), pl.BlockSpec((1, 1, d, ff), lo3),
                      pl.BlockSpec((1, 1, ff, d), lo3),
                      pl.BlockSpec((1, 1, d, ff), hi3), pl.BlockSpec((1, 1, d, ff), hi3),
                      pl.BlockSpec((1, 1, ff, d), hi3)],
            out_specs=pl.BlockSpec(memory_space=pl.ANY),
            scratch_shapes=[pltpu.VMEM((3, tm, nk, LANES), F32), pltpu.VMEM((2, tm) + oshape, F32),
                            pltpu.SemaphoreType.DMA((3,)), pltpu.SemaphoreType.DMA((2,))],
        ),
        out_shape=jax.ShapeDtypeStruct((t,) + oshape, F32),
        compiler_params=_cparams("arbitrary"),
        name="moe_final" if final else "moe",
    )(src, nvalid, elo, ehi, xs, wr_hi, rbias, g, gfin, wg, wu, wd, wg, wu, wd)


def _invert_body(dest_ref, lo_ref, hi_ref, src_ref):
    def place(r, c):
        src_ref[dest_ref[r]] = r
        return c

    lax.fori_loop(0, dest_ref.shape[0], place, 0, unroll=8)

    for k in range(lo_ref.shape[0]):
        def clear(j, c):
            src_ref[j] = 0
            return c

        lax.fori_loop(lo_ref[k], hi_ref[k], clear, 0)


def _invert(dest, pad_lo, pad_hi, n):
    smem = pl.BlockSpec(memory_space=pltpu.SMEM)
    return pl.pallas_call(
        _invert_body,
        in_specs=[smem, smem, smem],
        out_specs=smem,
        out_shape=jax.ShapeDtypeStruct((n,), jnp.int32),
        name="invert",
    )(dest, pad_lo, pad_hi)


def _router_weights(w_coarse, b_coarse, w_fine, b_fine):
    d = w_coarse.shape[0]
    wr = jnp.zeros((d, LANES), F32).at[:, :MOE_GROUPS].set(w_coarse)
    wr = wr.at[:, MOE_GROUPS:MOE_GROUPS + w_fine.shape[1]].set(w_fine)
    wr_hi = wr.astype(BF16)
    wr_lo = (wr - wr_hi.astype(F32)).astype(BF16)
    bias = jnp.zeros((1, LANES), F32).at[0, :MOE_GROUPS].set(b_coarse)
    bias = bias.at[0, MOE_GROUPS:MOE_GROUPS + b_fine.shape[0]].set(b_fine)
    return wr_hi, wr_lo, bias


def _moe_layer(xs, rw, cnt, g, gfin, wr_hi, bias, wg, wu, wd, layer, final, out_order):
    t = xs.shape[0]
    tm = _tile(t, 256)
    nt = t // tm + N_CLASSES
    i32 = jnp.int32
    cls = rw[0].astype(i32)
    rank = rw[1].astype(i32)
    counts = cnt[0, :N_CLASSES].astype(i32)
    tiles_per = (counts + tm - 1) // tm
    tile_end = jnp.cumsum(tiles_per)
    tile_start = tile_end - tiles_per
    class_ids = jnp.arange(N_CLASSES, dtype=i32)
    dest = jnp.sum(jnp.where(cls[:, None] == class_ids[None, :], (tile_start * tm)[None, :], 0), axis=1) + rank
    tile_ids = jnp.arange(nt, dtype=i32)
    total = tile_end[-1]
    pad_lo = jnp.concatenate([tile_start * tm + counts, (total * tm)[None]])
    pad_hi = jnp.concatenate([tile_end * tm, jnp.full((1,), nt * tm, i32)])
    src = _invert(dest, pad_lo.astype(i32), pad_hi.astype(i32), nt * tm)
    used = tile_ids < total
    tcls = jnp.sum((jnp.minimum(tile_ids, total - 1)[:, None] >= tile_end[None, :]).astype(i32), axis=1)
    tcls = jnp.clip(tcls, 0, N_CLASSES - 1)
    sel = tcls[:, None] == class_ids[None, :]
    tcount = jnp.sum(jnp.where(sel, counts[None, :], 0), axis=1)
    tstart = jnp.sum(jnp.where(sel, tile_start[None, :], 0), axis=1)
    nvalid = jnp.where(used, jnp.clip(tcount - (tile_ids - tstart) * tm, 0, tm), 0).astype(i32)
    grp, pair = tcls // N_PAIRS, tcls % N_PAIRS
    pair_ids = jnp.arange(N_PAIRS, dtype=i32)
    psel = pair[:, None] == pair_ids[None, :]
    elo = (grp * MOE_PER_GROUP + jnp.sum(jnp.where(psel, jnp.array(PAIR_LO, i32)[None, :], 0), axis=1)).astype(i32)
    ehi = (grp * MOE_PER_GROUP + jnp.sum(jnp.where(psel, jnp.array(PAIR_HI, i32)[None, :], 0), axis=1)).astype(i32)
    return _moe(xs, wr_hi, bias, src, nvalid, elo, ehi, g, gfin, wg, wu, wd, layer, tm, final, out_order,
                out_order == "fold16")


def _mixer0(x2, bsz, seq, g, w_in, lru_conv_w, lru_conv_b, lru_w_a, lru_b_a, lru_w_x, lru_b_x, lru_lam,
            m_conv_w, m_conv_b, m_i_bias, m_f_bias, m_head_g, w_out, g_ffn, wr_hi, wr_lo, rbias):
    d = x2.shape[1]
    c = lru_conv_w.shape[1]
    mw = m_head_g.shape[0]
    assert mw == c and m_conv_w.shape[1] == 2 * c
    nmain = 2 * c + 4 * mw
    w_gate = jnp.zeros((d, LANES), F32).at[:, :2 * MLSTM_HEADS].set(w_in[:, nmain:]).astype(BF16)
    conv_w = jnp.concatenate([lru_conv_w, m_conv_w], axis=1)
    conv_b = jnp.concatenate([lru_conv_b, m_conv_b])[None]
    pc, pb, gates = _in_proj0(x2, g[None], w_in.astype(BF16), w_gate, conv_w, conv_b, seq, c, mw // MLSTM_HEADS)
    ya = _rg_lru(pc, pb, bsz, seq, lru_w_a.astype(BF16), lru_b_a[None], lru_w_x.astype(BF16), lru_b_x[None],
                 lru_lam[None])
    gate_bias = jnp.zeros((1, LANES), F32).at[0, :MLSTM_HEADS].set(m_i_bias)
    gate_bias = gate_bias.at[0, MLSTM_HEADS:2 * MLSTM_HEADS].set(m_f_bias)
    yb = _mlstm(pb, gates, gate_bias, bsz, seq, m_head_g[None], c)
    return _out_proj(ya, yb, w_out.astype(BF16), x2, g_ffn, wr_hi, wr_lo, rbias)


def _mixer1(xs, seq, g, w_in, a_re, a_im, log_step, b_re, b_im, c_re, c_im, d_skip, w_glu_v, w_glu_g):
    ops = _s5_operators(a_re, a_im, log_step, b_re, b_im, c_re, c_im, d_skip, seq // S5_L)
    z = _s5_in(xs, g[None], jnp.transpose(w_in).astype(BF16))
    y = _s5_core(z, *ops, seq)
    return _s5_glu(y, jnp.transpose(w_glu_v).astype(BF16), jnp.transpose(w_glu_g).astype(BF16), xs)


def kernel(x, norm_mix, norm_ffn, norm_final, ab_w_in, lru_conv_w, lru_conv_b, lru_w_a, lru_b_a, lru_w_x, lru_b_x,
           lru_lam, m_conv_w, m_conv_b, m_i_bias, m_f_bias, m_head_g, ab_w_out, s5_w_in, s5_a_re, s5_a_im,
           s5_log_step, s5_b_re, s5_b_im, s5_c_re, s5_c_im, s5_d, s5_w_glu_v, s5_w_glu_g, moe_w_coarse,
           moe_b_coarse, moe_w_fine, moe_b_fine, moe_w_gate, moe_w_up, moe_w_down):
    bsz, seq, d = x.shape
    depth = norm_mix.shape[0]
    x2 = x.reshape(bsz * seq, d)
    gfin = norm_final[None]
    wg16, wu16, wd16 = moe_w_gate.astype(BF16), moe_w_up.astype(BF16), moe_w_down.astype(BF16)
    for layer in range(depth):
        j = layer // 2
        last = layer == depth - 1
        g_ffn = norm_ffn[layer][None]
        wr_hi, wr_lo, rbias = _router_weights(moe_w_coarse[layer], moe_b_coarse[layer], moe_w_fine[layer],
                                              moe_b_fine[layer])
        if layer % 2 == 0:
            xs, rw, cnt = _mixer0(x2, bsz, seq, norm_mix[layer], ab_w_in[j], lru_conv_w[j], lru_conv_b[j],
                                  lru_w_a[j], lru_b_a[j], lru_w_x[j], lru_b_x[j], lru_lam[j], m_conv_w[j],
                                  m_conv_b[j], m_i_bias[j], m_f_bias[j], m_head_g[j], ab_w_out[j],
                                  g_ffn, wr_hi, wr_lo, rbias)
            out_order = "same" if last else "fold16"
        else:
            xr, xs = _mixer1(x2, seq, norm_mix[layer], s5_w_in[j], s5_a_re[j], s5_a_im[j], s5_log_step[j],
                             s5_b_re[j], s5_b_im[j], s5_c_re[j], s5_c_im[j], s5_d[j], s5_w_glu_v[j], s5_w_glu_g[j])
            rw, cnt = _router(xr, g_ffn, wr_hi, wr_lo, rbias)
            out_order = "unfold16"
        x2 = _moe_layer(xs, rw, cnt, g_ffn, gfin, wr_hi, rbias, wg16, wu16, wd16, layer, last, out_order)
    return x2.reshape(bsz, seq, d)
```

```python
import functools
import math

import jax
import jax.numpy as jnp
from jax import lax
from jax.experimental import pallas as pl
from jax.experimental.pallas import tpu as pltpu

F32 = jnp.float32
BF16 = jnp.bfloat16
EPS = 1e-6
LANES = 128
VMEM_LIMIT = 56 * 1024 * 1024

CONV_K = 4
LRU_BLOCKS = 8
LRU_C = 8.0
MLSTM_HEADS = 4
S5_GROUP = 16
S5_STATE = 64
S5_L = 16
MOE_GROUPS = 4
MOE_PER_GROUP = 4
N_PAIRS = 6
N_CLASSES = MOE_GROUPS * N_PAIRS
PAIR_LO = (0, 0, 0, 1, 1, 2)
PAIR_HI = (1, 2, 3, 2, 3, 3)


def _tile(n, pref):
    return pref if n % pref == 0 else n


def _cparams(*sem):
    return pltpu.CompilerParams(dimension_semantics=sem, vmem_limit_bytes=VMEM_LIMIT)


def _rms(x, g):
    return x * lax.rsqrt(jnp.mean(x * x, axis=-1, keepdims=True) + EPS) * g


def _dot(a, b):
    return jnp.dot(a, b, preferred_element_type=F32)


def _dot_nt(a, b):
    return lax.dot_general(a, b, (((1,), (1,)), ((), ())), preferred_element_type=F32)


def _dot_tn(a, b):
    return lax.dot_general(a, b, (((0,), (0,)), ((), ())), preferred_element_type=F32)


def _softplus(y):
    return jnp.maximum(y, 0.0) + jnp.log1p(jnp.exp(-jnp.abs(y)))


def _split3(x):
    p1 = x.astype(BF16)
    r1 = x - p1.astype(F32)
    p2 = r1.astype(BF16)
    p3 = (r1 - p2.astype(F32)).astype(BF16)
    return p1, p2, p3


def _rows_to_slabs(y):
    k = y.shape[1] // LANES
    return jnp.swapaxes(jnp.stack([y[:, c * LANES:(c + 1) * LANES] for c in range(k)], axis=0), 0, 1)


def _slabs_to_rows(x3):
    xs = jnp.swapaxes(x3, 0, 1)
    return jnp.concatenate([xs[c] for c in range(x3.shape[1])], axis=1)


def _causal_conv(x, tail, w, b):
    n = x.shape[0]
    xp = jnp.concatenate([tail, x], axis=0)
    out = b + w[CONV_K - 1:CONV_K, :] * x
    for j in range(CONV_K - 1):
        off = 8 - (CONV_K - 1) + j
        out = out + w[j:j + 1, :] * xp[off:off + n, :]
    return out


def _in0_body(x_ref, g_ref, w_ref, wg_ref, cw_ref, cb_ref, pc_ref, pb_ref, gate_ref, tail_ref, *, seq_tiles, dh):
    tm, c = pc_ref.shape

    @pl.when(pl.program_id(0) == 0)
    def _():
        tail_ref[...] = jnp.zeros_like(tail_ref)

    hb = _rms(x_ref[...], g_ref[...]).astype(BF16)
    gate_ref[...] = _dot(hb, wg_ref[...])
    seq_start = (pl.program_id(0) % seq_tiles) == 0

    def proj(jt):
        return _dot(hb, w_ref[:, jt * c:(jt + 1) * c])

    def conv(k, raw):
        tail = jnp.where(seq_start, 0.0, tail_ref[k])
        tail_ref[k] = raw[tm - 8:tm, :]
        return _causal_conv(raw, tail, cw_ref[:, k * c:(k + 1) * c], cb_ref[:, k * c:(k + 1) * c])

    pc_ref[...] = conv(0, proj(0))
    pb_ref[:, 0:c] = jax.nn.gelu(proj(1)).astype(BF16)
    q = conv(1, proj(2))
    pb_ref[:, c:2 * c] = (q * jax.nn.sigmoid(q)).astype(BF16)
    k = conv(2, proj(3))
    pb_ref[:, 2 * c:3 * c] = (k * jax.nn.sigmoid(k) * (dh ** -0.5)).astype(BF16)
    pb_ref[:, 3 * c:4 * c] = proj(4).astype(BF16)
    pb_ref[:, 4 * c:5 * c] = jax.nn.sigmoid(proj(5)).astype(BF16)


def _in_proj0(x2, g, w, w_gate, conv_w, conv_b, seq, c, dh):
    t, d = x2.shape
    tm = _tile(seq, 512)
    assert seq % tm == 0
    return pl.pallas_call(
        functools.partial(_in0_body, seq_tiles=seq // tm, dh=dh),
        grid=(t // tm,),
        in_specs=[pl.BlockSpec((tm, d), lambda i: (i, 0)),
                  pl.BlockSpec((1, d), lambda i: (0, 0)),
                  pl.BlockSpec(w.shape, lambda i: (0, 0), pipeline_mode=pl.Buffered(1)),
                  pl.BlockSpec((d, LANES), lambda i: (0, 0)),
                  pl.BlockSpec(conv_w.shape, lambda i: (0, 0)),
                  pl.BlockSpec(conv_b.shape, lambda i: (0, 0))],
        out_specs=[pl.BlockSpec((tm, c), lambda i: (i, 0)),
                   pl.BlockSpec((tm, 5 * c), lambda i: (i, 0)),
                   pl.BlockSpec((tm, LANES), lambda i: (i, 0))],
        out_shape=[jax.ShapeDtypeStruct((t, c), F32), jax.ShapeDtypeStruct((t, 5 * c), BF16),
                   jax.ShapeDtypeStruct((t, LANES), F32)],
        scratch_shapes=[pltpu.VMEM((3, 8, c), F32)],
        compiler_params=_cparams("arbitrary"),
        name="in_proj0",
    )(x2, g, w, w_gate, conv_w, conv_b)


def _lru_body(x_ref, gz_ref, wa_ref, ba_ref, wx_ref, bx_ref, lam_ref, y_ref, h_ref, a_s, g_s):
    ts, c = x_ref.shape
    blk = c // LRU_BLOCKS

    @pl.when(pl.program_id(1) == 0)
    def _():
        h_ref[...] = jnp.zeros_like(h_ref)

    conv = x_ref[...]
    cb16 = conv.astype(BF16)
    ra = jnp.concatenate([_dot(cb16[:, n * blk:(n + 1) * blk], wa_ref[n]) for n in range(LRU_BLOCKS)], axis=1)
    rx = jnp.concatenate([_dot(cb16[:, n * blk:(n + 1) * blk], wx_ref[n]) for n in range(LRU_BLOCKS)], axis=1)
    r = jax.nn.sigmoid(ra + ba_ref[...])
    ig = jax.nn.sigmoid(rx + bx_ref[...])
    log_a = -LRU_C * r * _softplus(-lam_ref[...])
    a = jnp.exp(log_a)
    pa = a
    pg = jnp.sqrt(1.0 - a * a) * (ig * conv)
    row8 = lax.broadcasted_iota(jnp.int32, (ts, c), 0) & 7
    for s in (1, 2, 4):
        keep = row8 >= s
        ga = jnp.where(keep, pltpu.roll(pg, s, axis=0), 0.0)
        aa = jnp.where(keep, pltpu.roll(pa, s, axis=0), 1.0)
        pg = pa * ga + pg
        pa = pa * aa
    a_s[...] = pa
    g_s[...] = pg

    def group(k, h):
        r0 = pl.multiple_of(k * 8, 8)
        h8 = a_s[pl.ds(r0, 8), :] * h + g_s[pl.ds(r0, 8), :]
        g_s[pl.ds(r0, 8), :] = h8
        return h8[7:8, :]

    h_ref[...] = lax.fori_loop(0, ts // 8, group, h_ref[...], unroll=4)
    y_ref[...] = (g_s[...] * gz_ref[...].astype(F32)).astype(y_ref.dtype)


def _rg_lru(pc, pb, bsz, seq, wa, ba, wx, bx, lam):
    c = pc.shape[1]
    ts = _tile(seq, 512)
    ns = seq // ts
    row = lambda b, s: (0, 0)
    return pl.pallas_call(
        _lru_body,
        grid=(bsz, ns),
        in_specs=[pl.BlockSpec((ts, c), lambda b, s: (b * ns + s, 0)),
                  pl.BlockSpec((ts, c), lambda b, s: (b * ns + s, 0)),
                  pl.BlockSpec(wa.shape, lambda b, s: (0, 0, 0)),
                  pl.BlockSpec((1, c), row),
                  pl.BlockSpec(wx.shape, lambda b, s: (0, 0, 0)),
                  pl.BlockSpec((1, c), row),
                  pl.BlockSpec((1, c), row)],
        out_specs=pl.BlockSpec((ts, c), lambda b, s: (b * ns + s, 0)),
        out_shape=jax.ShapeDtypeStruct((bsz * seq, c), BF16),
        scratch_shapes=[pltpu.VMEM((1, c), F32), pltpu.VMEM((ts, c), F32), pltpu.VMEM((ts, c), F32)],
        compiler_params=_cparams("parallel", "arbitrary"),
        name="rg_lru",
    )(pc, pb, wa, ba, wx, bx, lam)


def _mlstm_body(q_ref, k_ref, v_ref, so_ref, gt_ref, gb_ref, hg_ref, tri_ref, y_ref, ct_ref, n_ref, m_ref):
    L = q_ref.shape[0]
    nh = ct_ref.shape[0]
    dh = ct_ref.shape[1]

    @pl.when(pl.program_id(1) == 0)
    def _():
        ct_ref[...] = jnp.zeros_like(ct_ref)
        n_ref[...] = jnp.zeros_like(n_ref)
        m_ref[...] = jnp.zeros_like(m_ref)

    gts = gt_ref[...] + gb_ref[...]
    lane = lax.broadcasted_iota(jnp.int32, gts.shape, 1)
    lsig = jnp.minimum(gts, 0.0) - jnp.log1p(jnp.exp(-jnp.abs(gts)))
    tri = tri_ref[...]
    p1, p2, p3 = _split3(lsig)
    bc_all = _dot(tri, p1) + _dot(tri, p2) + _dot(tri, p3)
    rows = lax.broadcasted_iota(jnp.int32, (L, L), 0)
    cols = lax.broadcasted_iota(jnp.int32, (L, L), 1)
    causal = rows >= cols

    for head in range(nh):
        hs = slice(head * dh, (head + 1) * dh)
        i_col = jnp.sum(jnp.where(lane == head, gts, 0.0), axis=-1, keepdims=True)
        bcum = jnp.sum(jnp.where(lane == head + nh, bc_all, 0.0), axis=-1, keepdims=True)
        a_col = i_col - bcum
        a_row = jnp.transpose(jnp.broadcast_to(a_col, (L, LANES)))[0:1, :]
        m_st = m_ref[head, 0:1, 0:1]

        d = jnp.where(causal, bcum + a_row, -jnp.inf)
        g = bcum + m_st
        m_t = jnp.maximum(g, jnp.max(d, axis=-1, keepdims=True))
        w_intra = jnp.exp(d - m_t)
        w_inter = jnp.exp(g - m_t)

        qb = q_ref[:, hs]
        kb = k_ref[:, hs]
        vb = v_ref[:, hs]
        s_qk = _dot_nt(qb, kb) * w_intra
        ct = ct_ref[head]
        n_row = n_ref[head]
        num = _dot(s_qk.astype(BF16), vb) + w_inter * _dot(qb, ct.astype(BF16))
        den = (jnp.sum(s_qk, axis=-1, keepdims=True)
               + w_inter * jnp.sum(qb.astype(F32) * n_row, axis=-1, keepdims=True))
        hcell = num / jnp.maximum(jnp.abs(den), jnp.exp(-m_t))

        b_last = bcum[L - 1:L, :]
        d_end = b_last + a_col
        m_new = jnp.maximum(b_last + m_st, jnp.max(d_end, axis=0, keepdims=True))
        w_end = jnp.exp(d_end - m_new)
        decay = jnp.exp(b_last + m_st - m_new)
        kw = kb.astype(F32) * w_end
        ct_ref[head] = decay * ct + _dot_tn(kw.astype(BF16), vb)
        n_ref[head] = decay * n_row + jnp.sum(kw, axis=0, keepdims=True)
        m_ref[head] = jnp.broadcast_to(m_new, m_ref.shape[1:])

        hm = so_ref[:, hs].astype(F32) * hcell
        hm = hm * lax.rsqrt(jnp.mean(hm * hm, axis=-1, keepdims=True) + EPS)
        y_ref[:, hs] = (hm * hg_ref[:, hs]).astype(y_ref.dtype)


def _mlstm(pb, gates, gate_bias, bsz, seq, head_g, col0):
    nh = MLSTM_HEADS
    w = head_g.shape[1]
    dh = w // nh
    L = _tile(seq, 256)
    nc = seq // L
    cq = col0 // w
    tri = jnp.tril(jnp.ones((L, L), BF16))

    def pblock(off):
        return pl.BlockSpec((L, w), lambda b, c: (b * nc + c, off))

    return pl.pallas_call(
        _mlstm_body,
        grid=(bsz, nc),
        in_specs=[pblock(cq), pblock(cq + 1), pblock(cq + 2), pblock(cq + 3),
                  pl.BlockSpec((L, LANES), lambda b, c: (b * nc + c, 0)),
                  pl.BlockSpec((1, LANES), lambda b, c: (0, 0)),
                  pl.BlockSpec((1, w), lambda b, c: (0, 0)),
                  pl.BlockSpec((L, L), lambda b, c: (0, 0))],
        out_specs=pl.BlockSpec((L, w), lambda b, c: (b * nc + c, 0)),
        out_shape=jax.ShapeDtypeStruct((bsz * seq, w), BF16),
        scratch_shapes=[pltpu.VMEM((nh, dh, dh), F32), pltpu.VMEM((nh, 1, dh), F32),
                        pltpu.VMEM((nh, 8, LANES), F32)],
        compiler_params=_cparams("parallel", "arbitrary"),
        name="mlstm",
    )(pb, pb, pb, pb, gates, gate_bias, head_g, tri)


def _outproj_body(ya_ref, yb_ref, wa_ref, wb_ref, x_ref, g_ref, whi_ref, wlo_ref, b_ref, tri_ref,
                  os_ref, rw_ref, cnt_ref, carry_ref):
    @pl.when(pl.program_id(0) == 0)
    def _():
        carry_ref[...] = jnp.zeros_like(carry_ref)

    res = x_ref[...] + _dot(ya_ref[...], wa_ref[...]) + _dot(yb_ref[...], wb_ref[...])
    os_ref[...] = _rows_to_slabs(res)
    rows, carry = _route_tile(res, g_ref[...], whi_ref[...], wlo_ref[...], b_ref[...], tri_ref[...], carry_ref[...])
    rw_ref[...] = rows
    carry_ref[...] = carry
    cnt_ref[...] = carry


def _out_proj(ya, yb, w, x2, g_ffn, wr_hi, wr_lo, rbias):
    t, d = x2.shape
    kh = ya.shape[1]
    assert yb.shape[1] == kh and w.shape[0] == 2 * kh
    tm = _tile(t, 512)
    tri = jnp.tril(jnp.ones((tm, tm), BF16), -1)
    cm = lambda i: (0, 0)
    return pl.pallas_call(
        _outproj_body,
        grid=(t // tm,),
        in_specs=[pl.BlockSpec((tm, kh), lambda i: (i, 0)),
                  pl.BlockSpec((tm, kh), lambda i: (i, 0)),
                  pl.BlockSpec((kh, d), lambda i: (0, 0)),
                  pl.BlockSpec((kh, d), lambda i: (1, 0)),
                  pl.BlockSpec((tm, d), lambda i: (i, 0)),
                  pl.BlockSpec((1, d), cm), pl.BlockSpec((d, LANES), cm), pl.BlockSpec((d, LANES), cm),
                  pl.BlockSpec((1, LANES), cm), pl.BlockSpec((tm, tm), cm)],
        out_specs=[pl.BlockSpec((tm, d // LANES, LANES), lambda i: (i, 0, 0)),
                   pl.BlockSpec((8, tm), lambda i: (0, i)),
                   pl.BlockSpec((1, LANES), cm)],
        out_shape=[jax.ShapeDtypeStruct((t, d // LANES, LANES), F32), jax.ShapeDtypeStruct((8, t), F32),
                   jax.ShapeDtypeStruct((1, LANES), F32)],
        scratch_shapes=[pltpu.VMEM((1, LANES), F32)],
        compiler_params=_cparams("arbitrary"),
        name="out_proj",
    )(ya, yb, w, w, x2, g_ffn, wr_hi, wr_lo, rbias, tri)


def _s5in_body(x_ref, g_ref, wt_ref, z_ref):
    hb = _rms(_slabs_to_rows(x_ref[...]), g_ref[...]).astype(BF16)
    res = _dot_nt(wt_ref[...], hb)
    z_ref[...] = res.reshape(z_ref.shape)


def _s5_in(xl, g, wt):
    t, nk, _ = xl.shape
    d = nk * LANES
    ch = wt.shape[0]
    ngrp = ch // S5_GROUP
    nr = t // S5_L
    r = _tile(nr, 512)
    nri = nr // r
    return pl.pallas_call(
        _s5in_body,
        grid=(nri, S5_L),
        in_specs=[pl.BlockSpec((r, nk, LANES), lambda i, l: (l * nri + i, 0, 0)),
                  pl.BlockSpec((1, d), lambda i, l: (0, 0)),
                  pl.BlockSpec((ch, d), lambda i, l: (0, 0))],
        out_specs=pl.BlockSpec((ngrp, S5_GROUP, r), lambda i, l: (0, l, i)),
        out_shape=jax.ShapeDtypeStruct((ngrp, S5_L * S5_GROUP, nr), F32),
        compiler_params=_cparams("parallel", "arbitrary"),
        name="s5_in",
    )(xl, g, wt)


def _s5core_body(z_ref, min_ref, mintra_ref, cst_ref, lp_ref, y_ref, *, nseg, nlev):
    p = S5_STATE
    nr = z_ref.shape[2]
    lane = lax.broadcasted_iota(jnp.int32, (1, nr), 1) % nseg
    for b in range(z_ref.shape[0]):
        zb = z_ref[b].astype(BF16)
        e = _dot(min_ref[b], zb)
        sr, si = e[:p, :], e[p:, :]
        lp = lp_ref[b]
        for j in range(nlev):
            s = 1 << j
            cr = lp[:, j:j + 1]
            ci = lp[:, nlev + j:nlev + j + 1]
            keep = lane >= s
            tr = jnp.where(keep, pltpu.roll(sr, s, axis=1), 0.0)
            ti = jnp.where(keep, pltpu.roll(si, s, axis=1), 0.0)
            sr, si = sr + cr * tr - ci * ti, si + cr * ti + ci * tr
        keep = lane >= 1
        xr = jnp.where(keep, pltpu.roll(sr, 1, axis=1), 0.0)
        xi = jnp.where(keep, pltpu.roll(si, 1, axis=1), 0.0)
        xb = jnp.concatenate([xr, xi], axis=0).astype(BF16)
        y = _dot(mintra_ref[b], zb) + _dot(cst_ref[b], xb)
        y_ref[b] = jax.nn.gelu(y).astype(y_ref.dtype)


def _s5_core(z, m_in, m_intra, c_st, lpow, seq):
    ngrp, kk, nr = z.shape
    nseg = seq // S5_L
    nlev = lpow.shape[2] // 2
    gb = 2 if ngrp % 2 == 0 else 1
    return pl.pallas_call(
        functools.partial(_s5core_body, nseg=nseg, nlev=nlev),
        grid=(ngrp // gb,),
        in_specs=[pl.BlockSpec((gb, kk, nr), lambda g: (g, 0, 0)),
                  pl.BlockSpec((gb,) + m_in.shape[1:], lambda g: (g, 0, 0)),
                  pl.BlockSpec((gb,) + m_intra.shape[1:], lambda g: (g, 0, 0)),
                  pl.BlockSpec((gb,) + c_st.shape[1:], lambda g: (g, 0, 0)),
                  pl.BlockSpec((gb,) + lpow.shape[1:], lambda g: (g, 0, 0))],
        out_specs=pl.BlockSpec((gb, kk, nr), lambda g: (g, 0, 0)),
        out_shape=jax.ShapeDtypeStruct((ngrp, kk, nr), BF16),
        compiler_params=_cparams("parallel"),
        name="s5_core",
    )(z, m_in, m_intra, c_st, lpow)


def _glu_body(y_ref, wv_ref, wg_ref, x_ref, o_ref, os_ref):
    ngrp, grp, r = y_ref.shape
    y = y_ref[...].reshape(ngrp * grp, r)
    v = _dot(wv_ref[...], y)
    g = _dot(wg_ref[...], y)
    res = _slabs_to_rows(x_ref[...]) + jnp.transpose(v * jax.nn.sigmoid(g))
    o_ref[...] = res
    os_ref[...] = _rows_to_slabs(res)


def _s5_glu(y, wvt, wgt, xl):
    t, nk, _ = xl.shape
    d = nk * LANES
    ngrp, _, nr = y.shape
    r = _tile(nr, 512)
    nri = nr // r
    tn = _tile(d, 1024)
    return pl.pallas_call(
        _glu_body,
        grid=(d // tn, nri, S5_L),
        in_specs=[pl.BlockSpec((ngrp, S5_GROUP, r), lambda j, i, l: (0, l, i)),
                  pl.BlockSpec((tn, ngrp * S5_GROUP), lambda j, i, l: (j, 0)),
                  pl.BlockSpec((tn, ngrp * S5_GROUP), lambda j, i, l: (j, 0)),
                  pl.BlockSpec((r, tn // LANES, LANES), lambda j, i, l: (l * nri + i, j, 0))],
        out_specs=[pl.BlockSpec((r, tn), lambda j, i, l: (l * nri + i, j)),
                   pl.BlockSpec((r, tn // LANES, LANES), lambda j, i, l: (l * nri + i, j, 0))],
        out_shape=[jax.ShapeDtypeStruct((t, d), F32), jax.ShapeDtypeStruct((t, d // LANES, LANES), F32)],
        compiler_params=_cparams("arbitrary", "arbitrary", "arbitrary"),
        name="s5_glu",
    )(y, wvt, wgt, xl)


def _s5_operators(a_re, a_im, log_step, b_re, b_im, c_re, c_im, d_skip, nseg):
    hi = lax.Precision.HIGHEST
    L, p, grp = S5_L, S5_STATE, S5_GROUP
    ngrp = a_re.shape[0]
    dt = jnp.exp(log_step)[:, None]
    mag = jnp.exp(a_re * dt)
    lr, li = mag * jnp.cos(a_im * dt), mag * jnp.sin(a_im * dt)
    den = a_re * a_re + a_im * a_im
    fr = ((lr - 1.0) * a_re + li * a_im) / den
    fi = (li * a_re - (lr - 1.0) * a_im) / den
    bbr = fr[..., None] * b_re - fi[..., None] * b_im
    bbi = fr[..., None] * b_im + fi[..., None] * b_re
    pr, pi = [jnp.ones_like(lr)], [jnp.zeros_like(li)]
    for _ in range(L):
        pr_n = pr[-1] * lr - pi[-1] * li
        pi_n = pr[-1] * li + pi[-1] * lr
        pr.append(pr_n)
        pi.append(pi_n)
    pwr, pwi = jnp.stack(pr, 1), jnp.stack(pi, 1)
    lbr = pwr[:, :L, :, None] * bbr[:, None] - pwi[:, :L, :, None] * bbi[:, None]
    lbi = pwr[:, :L, :, None] * bbi[:, None] + pwi[:, :L, :, None] * bbr[:, None]
    m_in = jnp.concatenate([jnp.transpose(lbr[:, ::-1], (0, 2, 1, 3)).reshape(ngrp, p, L * grp),
                            jnp.transpose(lbi[:, ::-1], (0, 2, 1, 3)).reshape(ngrp, p, L * grp)], axis=1)
    taps = (jnp.einsum('gcp,gkpd->gkcd', c_re, lbr, precision=hi)
            - jnp.einsum('gcp,gkpd->gkcd', c_im, lbi, precision=hi))
    taps = taps.at[:, 0].add(jax.vmap(jnp.diag)(d_skip.reshape(ngrp, grp)))
    lag = jnp.arange(L)[:, None] - jnp.arange(L)[None, :]
    shift = (lag[None] == jnp.arange(L)[:, None, None]).astype(BF16)
    m_intra = jnp.einsum('klm,gkcd->glcmd', shift, taps.astype(BF16),
                         preferred_element_type=F32).reshape(ngrp, L * grp, L * grp)
    qr, qi = pwr[:, 1:], pwi[:, 1:]
    cs_r = c_re[:, None] * qr[:, :, None, :] - c_im[:, None] * qi[:, :, None, :]
    cs_i = -c_re[:, None] * qi[:, :, None, :] - c_im[:, None] * qr[:, :, None, :]
    c_st = jnp.concatenate([cs_r, cs_i], axis=-1).reshape(ngrp, L * grp, 2 * p)
    nlev = max(1, int(math.log2(nseg)))
    sr_, si_ = [pwr[:, L]], [pwi[:, L]]
    for _ in range(nlev - 1):
        sr_.append(sr_[-1] * sr_[-1] - si_[-1] * si_[-1])
        si_.append(2.0 * sr_[-2] * si_[-1])
    lpow = jnp.stack(sr_ + si_, axis=-1)
    return m_in.astype(BF16), m_intra.astype(BF16), c_st.astype(BF16), lpow


def _route_tile(x, g, whi, wlo, bias, tri, carry):
    h = _rms(x, g)
    h_hi = h.astype(BF16)
    h_lo = (h - h_hi.astype(F32)).astype(BF16)
    logits = _dot(h_hi, whi) + _dot(h_lo, whi) + _dot(h_hi, wlo) + bias
    lane = lax.broadcasted_iota(jnp.int32, logits.shape, 1).astype(F32)
    neg, big = -jnp.inf, 1e9

    def first_argmax(vals):
        mx = jnp.max(vals, axis=-1, keepdims=True)
        return jnp.min(jnp.where(vals == mx, lane, big), axis=-1, keepdims=True)

    gi = first_argmax(jnp.where(lane < MOE_GROUPS, logits, neg))
    base = MOE_GROUPS + MOE_PER_GROUP * gi
    lf = jnp.where((lane >= base) & (lane < base + MOE_PER_GROUP), logits, neg)
    i1 = first_argmax(lf)
    i2 = first_argmax(jnp.where(lane == i1, neg, lf))
    k1, k2 = i1 - base, i2 - base
    lo, hi = jnp.minimum(k1, k2), jnp.maximum(k1, k2)
    cls = gi * N_PAIRS + lo * (7.0 - lo) * 0.5 + (hi - lo - 1.0)
    onehot = jnp.where(lane == cls, 1.0, 0.0)
    before = _dot(tri, onehot.astype(BF16))
    rank = jnp.sum(onehot * (before + carry), axis=-1, keepdims=True)
    out = jnp.where(lane == 0, cls, jnp.where(lane == 1, rank, 0.0))
    return jnp.transpose(out)[0:8, :], carry + jnp.sum(onehot, axis=0, keepdims=True)


def _router_body(x_ref, g_ref, whi_ref, wlo_ref, b_ref, tri_ref, o_ref, cnt_ref, carry_ref):
    @pl.when(pl.program_id(0) == 0)
    def _():
        carry_ref[...] = jnp.zeros_like(carry_ref)

    rows, carry = _route_tile(x_ref[...], g_ref[...], whi_ref[...], wlo_ref[...], b_ref[...], tri_ref[...],
                              carry_ref[...])
    o_ref[...] = rows
    carry_ref[...] = carry
    cnt_ref[...] = carry


def _router(x2, g, w_hi, w_lo, bias):
    t, d = x2.shape
    tm = _tile(t, 512)
    tri = jnp.tril(jnp.ones((tm, tm), BF16), -1)
    return pl.pallas_call(
        _router_body,
        grid=(t // tm,),
        in_specs=[pl.BlockSpec((tm, d), lambda i: (i, 0)),
                  pl.BlockSpec((1, d), lambda i: (0, 0)),
                  pl.BlockSpec((d, LANES), lambda i: (0, 0)),
                  pl.BlockSpec((d, LANES), lambda i: (0, 0)),
                  pl.BlockSpec((1, LANES), lambda i: (0, 0)),
                  pl.BlockSpec((tm, tm), lambda i: (0, 0))],
        out_specs=[pl.BlockSpec((8, tm), lambda i: (0, i)),
                   pl.BlockSpec((1, LANES), lambda i: (0, 0))],
        out_shape=[jax.ShapeDtypeStruct((8, t), F32), jax.ShapeDtypeStruct((1, LANES), F32)],
        scratch_shapes=[pltpu.VMEM((1, LANES), F32)],
        compiler_params=_cparams("arbitrary"),
        name="router",
    )(x2, g, w_hi, w_lo, bias, tri)


def _moe_body(src_ref, nv_ref, elo_ref, ehi_ref,
              x_hbm, whi_ref, rb_ref, g_ref, gfin_ref, wg_lo, wu_lo, wd_lo, wg_hi, wu_hi, wd_hi,
              o_hbm, xbuf, obuf, sem_in, sem_out, *, final, out_order, nr):
    i = pl.program_id(0)
    nt = pl.num_programs(0)
    tm = xbuf.shape[1]
    slot = i % 2
    nv = nv_ref[i]

    def divmod_nonneg(v, m):
        if m & (m - 1) == 0:
            sh = m.bit_length() - 1
            return lax.shift_right_logical(v, sh), v & (m - 1)
        return v // m, v % m

    def out_row(row):
        if out_order == "fold16":
            q, r = divmod_nonneg(row, S5_L)
            return r * nr + q
        if out_order == "unfold16":
            q, r = divmod_nonneg(row, nr)
            return r * S5_L + q
        return row

    def gather_row(tile, s, r):
        row = src_ref[tile * tm + r]
        return pltpu.make_async_copy(x_hbm.at[pl.ds(row, 1)], xbuf.at[s, pl.ds(r, 1)], sem_in.at[s])

    def wait_gather(s):
        pltpu.make_async_copy(x_hbm.at[pl.ds(0, tm)], xbuf.at[s], sem_in.at[s]).wait()

    def scatter_row(tile, s, r):
        row = out_row(src_ref[tile * tm + r])
        return pltpu.make_async_copy(obuf.at[s, pl.ds(r, 1)], o_hbm.at[pl.ds(row, 1)], sem_out.at[s])

    def issue_scatter(tile, s, n):
        for b in range(tm // 8):
            @pl.when(8 * b + 8 <= n)
            def _(b=b):
                for u in range(8):
                    scatter_row(tile, s, 8 * b + u).start(priority=u % 2)

        def single(r, c):
            scatter_row(tile, s, r).start()
            return c

        lax.fori_loop((n // 8) * 8, n, single, 0)

    def wait_scatter(s, n):
        p = 1
        while p <= tm:
            @pl.when((n & p) != 0)
            def _(p=p):
                pltpu.make_async_copy(obuf.at[s, pl.ds(0, p)], o_hbm.at[pl.ds(0, p)], sem_out.at[s]).wait()
            p *= 2

    def started(k):
        return nv_ref[jnp.clip(k - 2, 0, nt - 1)] > 0

    @pl.when((i == 0) & (nv > 0))
    def _():
        for k in (0, 1):
            def body(r, c, k=k):
                gather_row(jnp.minimum(k, nt - 1), k, r).start()
                return c

            lax.fori_loop(0, tm, body, 0, unroll=8)

    prev_used = (i > 0) & (nv_ref[jnp.maximum(i - 1, 0)] > 0)
    xslot = i % 3

    def tile_step(s):
        wait_gather(xslot)
        x = _slabs_to_rows(xbuf[xslot])
        hb = _rms(x, g_ref[...]).astype(BF16)
        ahead = jnp.minimum(i + 2, nt - 1)
        aslot = (i + 2) % 3
        for r in range(tm):
            gather_row(ahead, aslot, r).start(priority=1)

        def expert(wg, wu, wd):
            a = _dot(hb, wg[0, 0])
            u = _dot(hb, wu[0, 0])
            he = (a * jax.nn.sigmoid(a) * u).astype(BF16)
            return _dot(he, wd[0, 0])

        logits = _dot(hb, whi_ref[...]) + rb_ref[...]
        lane = lax.broadcasted_iota(jnp.int32, logits.shape, 1)
        e_lo, e_hi = elo_ref[i], ehi_ref[i]
        is_c = lane < MOE_GROUPS
        mx = jnp.max(jnp.where(is_c, logits, -jnp.inf), axis=-1, keepdims=True)
        ex = jnp.where(is_c, jnp.exp(logits - mx), 0.0)
        p_g = (jnp.sum(jnp.where(lane == e_lo // MOE_PER_GROUP, ex, 0.0), axis=-1, keepdims=True)
               / jnp.sum(ex, axis=-1, keepdims=True))
        v_lo = jnp.sum(jnp.where(lane == MOE_GROUPS + e_lo, logits, 0.0), axis=-1, keepdims=True)
        v_hi = jnp.sum(jnp.where(lane == MOE_GROUPS + e_hi, logits, 0.0), axis=-1, keepdims=True)
        w_lo = p_g / (1.0 + jnp.exp(v_hi - v_lo))
        w_hi = p_g / (1.0 + jnp.exp(v_lo - v_hi))
        out = x + w_lo * expert(wg_lo, wu_lo, wd_lo) + w_hi * expert(wg_hi, wu_hi, wd_hi)
        if final:
            out = _rms(out, gfin_ref[...])
        obuf[s] = _rows_to_slabs(out) if len(obuf.shape) == 4 else out
        issue_scatter(i, s, nv)

    for s in (0, 1):
        @pl.when((nv > 0) & (slot == s))
        def _(s=s):
            tile_step(s)

        @pl.when(prev_used & (slot == s))
        def _(s=s):
            wait_scatter(1 - s, nv_ref[jnp.maximum(i - 1, 0)])

        @pl.when((i == nt - 1) & (nv > 0) & (slot == s))
        def _(s=s):
            wait_scatter(s, nv)

    @pl.when((nv == 0) & started(i))
    def _():
        wait_gather(xslot)

    for k in (1, 2):
        @pl.when((i == nt - 1) & started(nt - 1 + k))
        def _(k=k):
            wait_gather((nt - 1 + k) % 3)


def _moe(xs, wr_hi, rbias, src, nvalid, elo, ehi, g, gfin, wg, wu, wd, layer, tm, final, out_order, out_slabs):
    t, nk, _ = xs.shape
    d = nk * LANES
    oshape = (nk, LANES) if out_slabs else (d,)
    nt = nvalid.shape[0]
    ff = wg.shape[3]
    cm = lambda i, *_: (0, 0)
    lo3 = lambda i, src, nv, elo, ehi: (layer, elo[i], 0, 0)
    hi3 = lambda i, src, nv, elo, ehi: (layer, ehi[i], 0, 0)
    return pl.pallas_call(
        functools.partial(_moe_body, final=final, out_order=out_order, nr=t // S5_L),
        grid_spec=pltpu.PrefetchScalarGridSpec(
            num_scalar_prefetch=4,
            grid=(nt,),
            in_specs=[pl.BlockSpec(memory_space=pl.ANY),
                      pl.BlockSpec((d, LANES), cm),
                      pl.BlockSpec((1, LANES), cm),
                      pl.BlockSpec((1, d), cm),
                      pl.BlockSpec((1, d), cm),
                      pl.BlockSpec((1, 1, d, ff), lo3), pl.BlockSpec((1, 1, d, ff), lo3),
                      pl.BlockSpec((1, 1, ff, d), lo3),
                      pl.BlockSpec((1, 1, d, ff), hi3), pl.BlockSpec((1, 1, d, ff), hi3),
                      pl.BlockSpec((1, 1, ff, d), hi3)],
            out_specs=pl.BlockSpec(memory_space=pl.ANY),
            scratch_shapes=[pltpu.VMEM((3, tm, nk, LANES), F32), pltpu.VMEM((2, tm) + oshape, F32),
                            pltpu.SemaphoreType.DMA((3,)), pltpu.SemaphoreType.DMA((2,))],
        ),
        out_shape=jax.ShapeDtypeStruct((t,) + oshape, F32),
        compiler_params=_cparams("arbitrary"),
        name="moe_final" if final else "moe",
    )(src, nvalid, elo, ehi, xs, wr_hi, rbias, g, gfin, wg, wu, wd, wg, wu, wd)


def _invert_body(dest_ref, lo_ref, hi_ref, src_ref):
    def place(r, c):
        src_ref[dest_ref[r]] = r
        return c

    lax.fori_loop(0, dest_ref.shape[0], place, 0, unroll=8)

    for k in range(lo_ref.shape[0]):
        def clear(j, c):
            src_ref[j] = 0
            return c

        lax.fori_loop(lo_ref[k], hi_ref[k], clear, 0)


def _invert(dest, pad_lo, pad_hi, n):
    smem = pl.BlockSpec(memory_space=pltpu.SMEM)
    return pl.pallas_call(
        _invert_body,
        in_specs=[smem, smem, smem],
        out_specs=smem,
        out_shape=jax.ShapeDtypeStruct((n,), jnp.int32),
        name="invert",
    )(dest, pad_lo, pad_hi)


def _router_weights(w_coarse, b_coarse, w_fine, b_fine):
    d = w_coarse.shape[0]
    wr = jnp.zeros((d, LANES), F32).at[:, :MOE_GROUPS].set(w_coarse)
    wr = wr.at[:, MOE_GROUPS:MOE_GROUPS + w_fine.shape[1]].set(w_fine)
    wr_hi = wr.astype(BF16)
    wr_lo = (wr - wr_hi.astype(F32)).astype(BF16)
    bias = jnp.zeros((1, LANES), F32).at[0, :MOE_GROUPS].set(b_coarse)
    bias = bias.at[0, MOE_GROUPS:MOE_GROUPS + b_fine.shape[0]].set(b_fine)
    return wr_hi, wr_lo, bias


def _moe_layer(xs, rw, cnt, g, gfin, wr_hi, bias, wg, wu, wd, layer, final, out_order):
    t = xs.shape[0]
    tm = _tile(t, 256)
    nt = t // tm + N_CLASSES
    i32 = jnp.int32
    cls = rw[0].astype(i32)
    rank = rw[1].astype(i32)
    counts = cnt[0, :N_CLASSES].astype(i32)
    tiles_per = (counts + tm - 1) // tm
    tile_end = jnp.cumsum(tiles_per)
    tile_start = tile_end - tiles_per
    class_ids = jnp.arange(N_CLASSES, dtype=i32)
    dest = jnp.sum(jnp.where(cls[:, None] == class_ids[None, :], (tile_start * tm)[None, :], 0), axis=1) + rank
    tile_ids = jnp.arange(nt, dtype=i32)
    total = tile_end[-1]
    pad_lo = jnp.concatenate([tile_start * tm + counts, (total * tm)[None]])
    pad_hi = jnp.concatenate([tile_end * tm, jnp.full((1,), nt * tm, i32)])
    src = _invert(dest, pad_lo.astype(i32), pad_hi.astype(i32), nt * tm)
    used = tile_ids < total
    tcls = jnp.sum((jnp.minimum(tile_ids, total - 1)[:, None] >= tile_end[None, :]).astype(i32), axis=1)
    tcls = jnp.clip(tcls, 0, N_CLASSES - 1)
    sel = tcls[:, None] == class_ids[None, :]
    tcount = jnp.sum(jnp.where(sel, counts[None, :], 0), axis=1)
    tstart = jnp.sum(jnp.where(sel, tile_start[None, :], 0), axis=1)
    nvalid = jnp.where(used, jnp.clip(tcount - (tile_ids - tstart) * tm, 0, tm), 0).astype(i32)
    grp, pair = tcls // N_PAIRS, tcls % N_PAIRS
    pair_ids = jnp.arange(N_PAIRS, dtype=i32)
    psel = pair[:, None] == pair_ids[None, :]
    elo = (grp * MOE_PER_GROUP + jnp.sum(jnp.where(psel, jnp.array(PAIR_LO, i32)[None, :], 0), axis=1)).astype(i32)
    ehi = (grp * MOE_PER_GROUP + jnp.sum(jnp.where(psel, jnp.array(PAIR_HI, i32)[None, :], 0), axis=1)).astype(i32)
    return _moe(xs, wr_hi, bias, src, nvalid, elo, ehi, g, gfin, wg, wu, wd, layer, tm, final, out_order,
                out_order == "fold16")


def _mixer0(x2, bsz, seq, g, w_in, lru_conv_w, lru_conv_b, lru_w_a, lru_b_a, lru_w_x, lru_b_x, lru_lam,
            m_conv_w, m_conv_b, m_i_bias, m_f_bias, m_head_g, w_out, g_ffn, wr_hi, wr_lo, rbias):
    d = x2.shape[1]
    c = lru_conv_w.shape[1]
    mw = m_head_g.shape[0]
    assert mw == c and m_conv_w.shape[1] == 2 * c
    nmain = 2 * c + 4 * mw
    w_gate = jnp.zeros((d, LANES), F32).at[:, :2 * MLSTM_HEADS].set(w_in[:, nmain:]).astype(BF16)
    conv_w = jnp.concatenate([lru_conv_w, m_conv_w], axis=1)
    conv_b = jnp.concatenate([lru_conv_b, m_conv_b])[None]
    pc, pb, gates = _in_proj0(x2, g[None], w_in.astype(BF16), w_gate, conv_w, conv_b, seq, c, mw // MLSTM_HEADS)
    ya = _rg_lru(pc, pb, bsz, seq, lru_w_a.astype(BF16), lru_b_a[None], lru_w_x.astype(BF16), lru_b_x[None],
                 lru_lam[None])
    gate_bias = jnp.zeros((1, LANES), F32).at[0, :MLSTM_HEADS].set(m_i_bias)
    gate_bias = gate_bias.at[0, MLSTM_HEADS:2 * MLSTM_HEADS].set(m_f_bias)
    yb = _mlstm(pb, gates, gate_bias, bsz, seq, m_head_g[None], c)
    return _out_proj(ya, yb, w_out.astype(BF16), x2, g_ffn, wr_hi, wr_lo, rbias)


def _mixer1(xs, seq, g, w_in, a_re, a_im, log_step, b_re, b_im, c_re, c_im, d_skip, w_glu_v, w_glu_g):
    ops = _s5_operators(a_re, a_im, log_step, b_re, b_im, c_re, c_im, d_skip, seq // S5_L)
    z = _s5_in(xs, g[None], jnp.transpose(w_in).astype(BF16))
    y = _s5_core(z, *ops, seq)
    return _s5_glu(y, jnp.transpose(w_glu_v).astype(BF16), jnp.transpose(w_glu_g).astype(BF16), xs)


def kernel(x, norm_mix, norm_ffn, norm_final, ab_w_in, lru_conv_w, lru_conv_b, lru_w_a, lru_b_a, lru_w_x, lru_b_x,
           lru_lam, m_conv_w, m_conv_b, m_i_bias, m_f_bias, m_head_g, ab_w_out, s5_w_in, s5_a_re, s5_a_im,
           s5_log_step, s5_b_re, s5_b_im, s5_c_re, s5_c_im, s5_d, s5_w_glu_v, s5_w_glu_g, moe_w_coarse,
           moe_b_coarse, moe_w_fine, moe_b_fine, moe_w_gate, moe_w_up, moe_w_down):
    bsz, seq, d = x.shape
    depth = norm_mix.shape[0]
    x2 = x.reshape(bsz * seq, d)
    gfin = norm_final[None]
    wg16, wu16, wd16 = moe_w_gate.astype(BF16), moe_w_up.astype(BF16), moe_w_down.astype(BF16)
    for layer in range(depth):
        j = layer // 2
        last = layer == depth - 1
        g_ffn = norm_ffn[layer][None]
        wr_hi, wr_lo, rbias = _router_weights(moe_w_coarse[layer], moe_b_coarse[layer], moe_w_fine[layer],
                                              moe_b_fine[layer])
        if layer % 2 == 0:
            xs, rw, cnt = _mixer0(x2, bsz, seq, norm_mix[layer], ab_w_in[j], lru_conv_w[j], lru_conv_b[j],
                                  lru_w_a[j], lru_b_a[j], lru_w_x[j], lru_b_x[j], lru_lam[j], m_conv_w[j],
                                  m_conv_b[j], m_i_bias[j], m_f_bias[j], m_head_g[j], ab_w_out[j],
                                  g_ffn, wr_hi, wr_lo, rbias)
            out_order = "same" if last else "fold16"
        else:
            xr, xs = _mixer1(x2, seq, norm_mix[layer], s5_w_in[j], s5_a_re[j], s5_a_im[j], s5_log_step[j],
                             s5_b_re[j], s5_b_im[j], s5_c_re[j], s5_c_im[j], s5_d[j], s5_w_glu_v[j], s5_w_glu_g[j])
            rw, cnt = _router(xr, g_ffn, wr_hi, wr_lo, rbias)
            out_order = "unfold16"
        x2 = _moe_layer(xs, rw, cnt, g_ffn, gfin, wr_hi, rbias, wg16, wu16, wd16, layer, last, out_order)
    return x2.reshape(bsz, seq, d)
```
